```python
import math
import jax, jax.numpy as jnp
from jax import lax
import numpy as np

D_MODEL = 1024
BATCH = 16
SEQ = 2048
DEPTH = 1

N_MEM = 256
HEAD_DIM = 64
A_HEADS = 8
A_WIDTH = A_HEADS * HEAD_DIM
IDX_HEADS = 4
IDX_DIM = 64
TOPK_MAX = 256
B_HEADS = 4
B_VDIM = 2 * HEAD_DIM
B_WIDTH = B_HEADS * B_VDIM
MIX_WIDTH = A_WIDTH + B_WIDTH
IN_SIZES = (A_WIDTH, HEAD_DIM, HEAD_DIM, IDX_HEADS * IDX_DIM, IDX_DIM, IDX_HEADS,
            2 * B_HEADS * HEAD_DIM, 2 * B_HEADS * HEAD_DIM, B_WIDTH)
IN_COLS = sum(IN_SIZES)
X_HEADS = 4
X_HEAD_DIM = D_MODEL // X_HEADS
D_FF = 2816
CONV_W = 3
ROPE_THETA = 10000.0
EPS = 1e-6
Q_BLOCK = 128

kernel_name = "hybrid_dsa_diffattn_memxattn_convglu"


def rms_normalize(x):
    xf = x.astype(jnp.float32)
    y = xf * lax.rsqrt(jnp.mean(xf * xf, axis=-1, keepdims=True) + EPS)
    return y.astype(x.dtype)


def rmsnorm(x, g):
    xf = x.astype(jnp.float32)
    y = xf * lax.rsqrt(jnp.mean(xf * xf, axis=-1, keepdims=True) + EPS)
    return (y * g.astype(jnp.float32)).astype(x.dtype)


def rope_tables(positions, dim):
    inv_freq = 1.0 / (ROPE_THETA ** (jnp.arange(0, dim, 2, dtype=jnp.float32) / dim))
    ang = positions.astype(jnp.float32)[..., None] * inv_freq
    return jnp.cos(ang), jnp.sin(ang)


def rope(x, cos, sin):
    shape = cos.shape[:2] + (1,) * (x.ndim - 3) + cos.shape[-1:]
    c = cos.reshape(shape).astype(x.dtype)
    s = sin.reshape(shape).astype(x.dtype)
    x1, x2 = jnp.split(x, 2, axis=-1)
    return jnp.concatenate([x1 * c - x2 * s, x2 * c + x1 * s], axis=-1)


def to_blocks(t, n_blk):
    return t.reshape((t.shape[0], n_blk, Q_BLOCK) + t.shape[2:]).swapaxes(0, 1)


def from_blocks(t):
    t = t.swapaxes(0, 1)
    return t.reshape((t.shape[0], t.shape[1] * t.shape[2]) + t.shape[3:])


def dsa_attention(q, k, v, q_idx, k_idx, w_idx):
    B, S, H, D = q.shape
    n_blk = S // Q_BLOCK
    topk = min(TOPK_MAX, S // 4)
    key_pos = jnp.arange(S)

    def block(args):
        qb, qib, wib, start = args
        qpos = start + jnp.arange(Q_BLOCK)
        causal = key_pos[None, :] <= qpos[:, None]
        logits = jnp.einsum('bqhd,bsd->bqhs', qib, k_idx)
        score = jnp.einsum('bqh,bqhs->bqs', wib, jax.nn.relu(logits)).astype(jnp.float32)
        score = jnp.where(causal[None], score, -jnp.inf)
        _, idx = lax.top_k(score, topk)
        k_sel = jax.vmap(lambda kk, ii: kk[ii])(k, idx)
        v_sel = jax.vmap(lambda vv, ii: vv[ii])(v, idx)
        valid = idx <= qpos[None, :, None]
        s = jnp.einsum('bqhd,bqkd->bqhk', qb, k_sel).astype(jnp.float32) * (D ** -0.5)
        s = jnp.where(valid[:, :, None, :], s, -jnp.inf)
        p = jax.nn.softmax(s, axis=-1).astype(v.dtype)
        return jnp.einsum('bqhk,bqkd->bqhd', p, v_sel)

    starts = jnp.arange(n_blk) * Q_BLOCK
    out = lax.map(block, (to_blocks(q, n_blk), to_blocks(q_idx, n_blk), to_blocks(w_idx, n_blk), starts))
    return from_blocks(out)


def diff_attention(q, k, v, lam):
    B, S, H, _, D = q.shape
    n_blk = S // Q_BLOCK
    key_pos = jnp.arange(S)

    def block(args):
        qb, start = args
        qpos = start + jnp.arange(Q_BLOCK)
        causal = key_pos[None, :] <= qpos[:, None]
        s = jnp.einsum('bqhcd,bshcd->bhcqs', qb, k).astype(jnp.float32) * (D ** -0.5)
        s = jnp.where(causal, s, -jnp.inf)
        p = jax.nn.softmax(s, axis=-1)
        a = (p[:, :, 0] - lam * p[:, :, 1]).astype(v.dtype)
        return jnp.einsum('bhqs,bshe->bqhe', a, v)

    starts = jnp.arange(n_blk) * Q_BLOCK
    out = lax.map(block, (to_blocks(q, n_blk), starts))
    return from_blocks(out)


def token_mixer(hn, cos, sin, w_in, g_qa, g_ka, g_qb, g_kb,
                lam_q1, lam_k1, lam_q2, lam_k2, w_out, lambda_init):
    B, S, _ = hn.shape
    proj = hn @ w_in
    offsets = np.cumsum(IN_SIZES)[:-1].tolist()
    q_a, k_a, v_a, q_i, k_i, w_i, q_b, k_b, v_b = jnp.split(proj, offsets, axis=-1)

    q_a = rope(rmsnorm(q_a.reshape(B, S, A_HEADS, HEAD_DIM), g_qa), cos, sin)
    k_a = rope(rmsnorm(k_a, g_ka), cos, sin)
    q_i = rope(q_i.reshape(B, S, IDX_HEADS, IDX_DIM), cos, sin)
    k_i = rope(k_i, cos, sin)
    w_i = w_i * (IDX_HEADS ** -0.5 * IDX_DIM ** -0.5)
    out_a = dsa_attention(q_a, k_a, v_a, q_i, k_i, w_i).reshape(B, S, A_WIDTH)

    q_b = rope(rmsnorm(q_b.reshape(B, S, B_HEADS, 2, HEAD_DIM), g_qb), cos, sin)
    k_b = rope(rmsnorm(k_b.reshape(B, S, B_HEADS, 2, HEAD_DIM), g_kb), cos, sin)
    v_b = v_b.reshape(B, S, B_HEADS, B_VDIM)
    f32 = jnp.float32
    lam = (jnp.exp(jnp.sum(lam_q1.astype(f32) * lam_k1.astype(f32)))
           - jnp.exp(jnp.sum(lam_q2.astype(f32) * lam_k2.astype(f32))) + lambda_init)
    out_b = diff_attention(q_b, k_b, v_b, lam)
    out_b = (rms_normalize(out_b) * (1.0 - lambda_init)).reshape(B, S, B_WIDTH)

    return jnp.concatenate([out_a, out_b], axis=-1) @ w_out


def cross_attention(hn, memn, w_xq, w_xk, w_xv, w_xo, g_xq, g_xk):
    B, S, _ = hn.shape
    M = memn.shape[1]
    q = rmsnorm((hn @ w_xq).reshape(B, S, X_HEADS, X_HEAD_DIM), g_xq)
    k = rmsnorm((memn @ w_xk).reshape(B, M, X_HEADS, X_HEAD_DIM), g_xk)
    v = (memn @ w_xv).reshape(B, M, X_HEADS, X_HEAD_DIM)
    s = jnp.einsum('bshd,bmhd->bhsm', q, k).astype(jnp.float32) * (X_HEAD_DIM ** -0.5)
    p = jax.nn.softmax(s, axis=-1).astype(v.dtype)
    o = jnp.einsum('bhsm,bmhd->bshd', p, v).reshape(B, S, D_MODEL)
    return o @ w_xo


def conv_glu(hn, w_ffn_in, conv_w, conv_b, w_ffn_out):
    S = hn.shape[1]
    a, gate = jnp.split(hn @ w_ffn_in, 2, axis=-1)
    a_pad = jnp.pad(a, ((0, 0), (CONV_W - 1, 0), (0, 0)))
    conv = conv_b + sum(a_pad[:, j:j + S] * conv_w[j] for j in range(CONV_W))
    return (jax.nn.gelu(conv) * gate) @ w_ffn_out


def setup_inputs(seed: int = 0) -> dict:
    key = jax.random.key(seed)
    ks = iter(jax.random.split(key, 48))
    f32 = jnp.float32

    def nrm(shape, fan_in):
        return jax.random.normal(next(ks), shape, f32) * (fan_in ** -0.5)

    def gain(shape):
        return 1.0 + 0.02 * jax.random.normal(next(ks), shape, f32)

    x = jax.random.normal(next(ks), (BATCH, SEQ, D_MODEL), f32)
    mem = jax.random.normal(next(ks), (BATCH, N_MEM, D_MODEL), f32)
    offs = jax.random.randint(next(ks), (BATCH, 1), 0, 1024)
    positions = (offs + jnp.arange(SEQ, dtype=jnp.int32)[None, :]).astype(jnp.int32)
    L = DEPTH
    return {
        "x": x,
        "mem": mem,
        "positions": positions,
        "g_mix": gain((L, D_MODEL)),
        "w_in": nrm((L, D_MODEL, IN_COLS), D_MODEL),
        "g_qa": gain((L, HEAD_DIM)),
        "g_ka": gain((L, HEAD_DIM)),
        "g_qb": gain((L, HEAD_DIM)),
        "g_kb": gain((L, HEAD_DIM)),
        "lam_q1": 0.1 * jax.random.normal(next(ks), (L, HEAD_DIM), f32),
        "lam_k1": 0.1 * jax.random.normal(next(ks), (L, HEAD_DIM), f32),
        "lam_q2": 0.1 * jax.random.normal(next(ks), (L, HEAD_DIM), f32),
        "lam_k2": 0.1 * jax.random.normal(next(ks), (L, HEAD_DIM), f32),
        "w_out": nrm((L, MIX_WIDTH, D_MODEL), MIX_WIDTH),
        "g_xattn": gain((L, D_MODEL)),
        "g_mem": gain((L, D_MODEL)),
        "w_xq": nrm((L, D_MODEL, D_MODEL), D_MODEL),
        "w_xk": nrm((L, D_MODEL, D_MODEL), D_MODEL),
        "w_xv": nrm((L, D_MODEL, D_MODEL), D_MODEL),
        "w_xo": nrm((L, D_MODEL, D_MODEL), D_MODEL),
        "g_xq": gain((L, X_HEAD_DIM)),
        "g_xk": gain((L, X_HEAD_DIM)),
        "g_ffn": gain((L, D_MODEL)),
        "w_ffn_in": nrm((L, D_MODEL, 2 * D_FF), D_MODEL),
        "conv_w": nrm((L, CONV_W, D_FF), CONV_W),
        "conv_b": 0.02 * jax.random.normal(next(ks), (L, D_FF), f32),
        "w_ffn_out": nrm((L, D_FF, D_MODEL), D_FF),
    }


def reference(x, mem, positions, g_mix, w_in, g_qa, g_ka, g_qb, g_kb,
              lam_q1, lam_k1, lam_q2, lam_k2, w_out, g_xattn, g_mem,
              w_xq, w_xk, w_xv, w_xo, g_xq, g_xk, g_ffn, w_ffn_in,
              conv_w, conv_b, w_ffn_out):
    cos, sin = rope_tables(positions, HEAD_DIM)
    h = x
    for l in range(DEPTH):
        lambda_init = 0.8 - 0.6 * math.exp(-0.3 * l)
        h = h + token_mixer(rmsnorm(h, g_mix[l]), cos, sin, w_in[l], g_qa[l], g_ka[l],
                            g_qb[l], g_kb[l], lam_q1[l], lam_k1[l], lam_q2[l], lam_k2[l],
                            w_out[l], lambda_init)
        h = h + cross_attention(rmsnorm(h, g_xattn[l]), rmsnorm(mem, g_mem[l]),
                                w_xq[l], w_xk[l], w_xv[l], w_xo[l], g_xq[l], g_xk[l])
        h = h + conv_glu(rmsnorm(h, g_ffn[l]), w_ffn_in[l], conv_w[l], conv_b[l], w_ffn_out[l])
    return h
```

```python
import functools
import math

import jax
import jax.numpy as jnp
from jax import lax
from jax.experimental import pallas as pl
from jax.experimental.pallas import tpu as pltpu

F32 = jnp.float32
BF16 = jnp.bfloat16

EPS = 1e-6
ROPE_THETA = 10000.0
HEAD_DIM = 64
A_HEADS = 8
IDX_HEADS = 4
TOPK_MAX = 256
B_HEADS = 4
X_HEADS = 4
CONV_W = 3
LANES = 128
MXU_DIM = 256
DSA_QBLK = 128
DIFF_QBLK = 256
VMEM_LIMIT = 56 * 1024 * 1024

_C_QA = 0
_C_QI = 512
_C_KS = 768
_C_QB = 1152
_C_KB = 1664
_C_VB = 2176
_C_END = 2688


def _dot(a, b):
    return jnp.dot(a, b, preferred_element_type=F32)


def _dot_nt(a, b):
    return lax.dot_general(a, b, (((1,), (1,)), ((), ())), preferred_element_type=F32)


def _rms_scale(x):
    return lax.rsqrt(jnp.mean(x * x, axis=-1, keepdims=True) + EPS)


def _const_spec(shape):
    zeros = (0,) * len(shape)
    return pl.BlockSpec(shape, lambda *_: zeros, pipeline_mode=pl.Buffered(1))


def _lane_tile(t, width):
    reps = width // t.shape[1]
    return t if reps == 1 else jnp.concatenate([t] * reps, axis=1)


def _inproj_body(x_ref, gmix_ref, w_ref, cos_ref, sin_ref, bd_ref, gains_ref,
                 qa_ref, qi_ref, ks_ref, wi_ref, qb_ref, kb_ref, vb_ref):
    x = x_ref[...]
    hn = (x * _rms_scale(x) * gmix_ref[...]).astype(BF16)
    cos = cos_ref[...]
    sin = sin_ref[...]
    bd = bd_ref[...]

    def proj(c0, width):
        return _dot(hn, w_ref[:, c0:c0 + width])

    def group_rms_scale(p):
        sq = p * p
        hi = sq.astype(BF16)
        lo = (sq - hi.astype(F32)).astype(BF16)
        outs = []
        for j in range(p.shape[1] // MXU_DIM):
            sl = slice(MXU_DIM * j, MXU_DIM * (j + 1))
            outs.append(_dot(hi[:, sl], bd) + _dot(lo[:, sl], bd))
        ms = outs[0] if len(outs) == 1 else jnp.concatenate(outs, axis=1)
        return lax.rsqrt(ms + EPS)

    def rope(y):
        width = y.shape[1]
        lane = lax.broadcasted_iota(jnp.int32, y.shape, 1)
        first_half = (lane & (HEAD_DIM - 1)) < (HEAD_DIM // 2)
        swapped = jnp.where(first_half, pltpu.roll(y, width - HEAD_DIM // 2, 1),
                            pltpu.roll(y, HEAD_DIM // 2, 1))
        return y * _lane_tile(cos, width) + swapped * _lane_tile(sin, width)

    sm_scale = HEAD_DIM ** -0.5

    p = proj(_C_QA, 512)
    qa_ref[...] = (rope(p * group_rms_scale(p) * gains_ref[0:1, :]) * sm_scale).astype(BF16)

    p = proj(_C_QI, 256)
    qi_ref[...] = rope(p).astype(BF16)

    p = proj(_C_KS, 384)
    p01 = p[:, 0:256]
    lane = lax.broadcasted_iota(jnp.int32, p01.shape, 1)
    y01 = jnp.where(lane < LANES, p01 * group_rms_scale(p01) * gains_ref[1:2, 0:256], p01)
    y01 = rope(y01)
    p2 = p[:, 256:384]
    lane = lax.broadcasted_iota(jnp.int32, p2.shape, 1)
    p2 = jnp.where(lane < HEAD_DIM, p2, p2 * (IDX_HEADS ** -0.5 * HEAD_DIM ** -0.5))
    ks_ref[:, 0:256] = y01.astype(BF16)
    ks_ref[:, 256:384] = p2.astype(BF16)
    wi_ref[...] = p2

    p = proj(_C_QB, 512)
    qb_ref[...] = (rope(p * group_rms_scale(p) * gains_ref[2:3, :]) * sm_scale).astype(BF16)

    p = proj(_C_KB, 512)
    kb_ref[...] = rope(p * group_rms_scale(p) * gains_ref[3:4, :]).astype(BF16)

    vb_ref[...] = proj(_C_VB, 512).astype(BF16)


def _inproj(x2, gmix, w_all, cos128, sin128, bd, gains, tm):
    n, d = x2.shape
    row = lambda i: (i, 0)
    const = lambda i: (0, 0)
    outs = [(512, BF16), (256, BF16), (384, BF16), (LANES, F32), (512, BF16), (512, BF16), (512, BF16)]
    return pl.pallas_call(
        _inproj_body,
        grid=(n // tm,),
        in_specs=[
            pl.BlockSpec((tm, d), row),
            pl.BlockSpec((1, d), const),
            pl.BlockSpec(w_all.shape, const),
            pl.BlockSpec((tm, LANES), row),
            pl.BlockSpec((tm, LANES), row),
            pl.BlockSpec(bd.shape, const),
            pl.BlockSpec(gains.shape, const),
        ],
        out_specs=[pl.BlockSpec((tm, w), row) for w, _ in outs],
        out_shape=[jax.ShapeDtypeStruct((n, w), dt) for w, dt in outs],
        compiler_params=pltpu.CompilerParams(
            dimension_semantics=("arbitrary",), vmem_limit_bytes=VMEM_LIMIT),
        name="inproj",
    )(x2, gmix, w_all, cos128, sin128, bd, gains)


def _dsa_body(qa_ref, qi_ref, wi_ref, ks_ref, o_ref, vt_ref, sc_ref, bias_ref, ot_ref,
              *, topk, n_bisect):
    qblk = pl.program_id(1)
    seq = ks_ref.shape[1]
    tq = DSA_QBLK

    @pl.when(qblk == 0)
    def _():
        vt_ref[...] = ks_ref[0, :, 256:384].astype(F32).T.astype(BF16)

    lane_q = lax.broadcasted_iota(jnp.int32, (tq, LANES), 1)
    first_head = lane_q < HEAD_DIM

    def head_rows(slab, odd):
        return jnp.where(first_head != odd, slab, jnp.zeros_like(slab))

    qi = qi_ref[0]
    qi_stack = jnp.concatenate(
        [head_rows(qi[:, LANES * (h // 2):LANES * (h // 2 + 1)], bool(h % 2)) for h in range(IDX_HEADS)],
        axis=0)
    lg = _dot_nt(ks_ref[0, :, 128:256], qi_stack)
    w_t = wi_ref[0].T
    sc = None
    for h in range(IDX_HEADS):
        term = jnp.maximum(lg[:, tq * h:tq * (h + 1)], 0.0) * w_t[HEAD_DIM + h:HEAD_DIM + h + 1, :]
        sc = term if sc is None else sc + term
    kpos = lax.broadcasted_iota(jnp.int32, (seq, tq), 0)
    qpos = qblk * tq + lax.broadcasted_iota(jnp.int32, (seq, tq), 1)
    causal = kpos <= qpos
    sc_ref[...] = jnp.where(causal, sc, -jnp.inf)

    kf = float(topk)
    qpos_row = qblk * tq + lax.broadcasted_iota(jnp.int32, (1, tq), 1)
    search = qpos_row >= topk

    def count_ge(v):
        return jnp.sum(jnp.where(sc_ref[...] >= v, 1.0, 0.0), axis=0, keepdims=True)

    def max_where(pred_fn):
        s = sc_ref[...]
        return jnp.max(jnp.where(pred_fn(s), s, -jnp.inf), axis=0, keepdims=True)

    s0 = sc_ref[...]
    hi0 = jnp.max(s0, axis=0, keepdims=True)
    lo0 = jnp.min(jnp.where(kpos <= qpos, s0, jnp.inf), axis=0, keepdims=True)

    def bisect(_, carry):
        lo, hi = carry
        mid = 0.5 * (lo + hi)
        ge = count_ge(mid) >= kf
        return jnp.where(ge, mid, lo), jnp.where(ge, hi, mid)

    lo, hi = lax.fori_loop(0, n_bisect, bisect, (lo0, hi0))

    cand0 = max_where(lambda s: s <= hi)

    def not_done(cand):
        ok = jnp.logical_or(count_ge(cand) >= kf, jnp.logical_not(search))
        return ok, jnp.max(jnp.where(ok, 0.0, 1.0))

    def refine_cond(carry):
        return carry[1] > 0.0

    def refine(carry):
        cand, _ = carry
        ok, _ = not_done(cand)
        nxt = max_where(lambda s: s < cand)
        cand = jnp.where(ok, cand, nxt)
        return cand, not_done(cand)[1]

    cand, _ = lax.while_loop(refine_cond, refine, (cand0, not_done(cand0)[1]))

    s = sc_ref[...]
    gt = s > cand
    eq = s == cand
    need = kf - jnp.sum(jnp.where(gt, 1.0, 0.0), axis=0, keepdims=True)

    def tie_bisect(_, carry):
        jlo, jhi = carry
        mid = (jlo + jhi) >> 1
        cnt = jnp.sum(jnp.where(jnp.logical_and(sc_ref[...] == cand, kpos <= mid), 1.0, 0.0),
                      axis=0, keepdims=True)
        ge = cnt >= need
        return jnp.where(ge, jlo, mid), jnp.where(ge, mid, jhi)

    n_tie = max(1, math.ceil(math.log2(seq + 1)))
    _, jlast = lax.fori_loop(0, n_tie, tie_bisect,
                             (jnp.full((1, tq), -1, jnp.int32), jnp.full((1, tq), seq - 1, jnp.int32)))
    picked = jnp.logical_or(gt, jnp.logical_and(eq, kpos <= jlast))
    sel = jnp.logical_and(causal, jnp.logical_or(jnp.logical_not(search), picked))
    bias_ref[...] = jnp.where(sel, 0.0, -jnp.inf)

    qa = qa_ref[0]
    for j in range(A_HEADS // 2):
        slab = qa[:, LANES * j:LANES * (j + 1)]
        q2 = jnp.concatenate([head_rows(slab, False), head_rows(slab, True)], axis=0)
        st = _dot_nt(ks_ref[0, :, 0:128], q2)
        bias = bias_ref[...]
        st = st + jnp.concatenate([bias, bias], axis=1)
        m = jnp.max(st, axis=0, keepdims=True)
        e = jnp.exp(st - m)
        l = jnp.sum(e, axis=0, keepdims=True)
        ot = _dot(vt_ref[...], e.astype(BF16)) * (1.0 / l)
        ot_ref[HEAD_DIM * 2 * j:HEAD_DIM * (2 * j + 1), :] = ot[0:HEAD_DIM, 0:tq]
        ot_ref[HEAD_DIM * (2 * j + 1):HEAD_DIM * (2 * j + 2), :] = ot[0:HEAD_DIM, tq:2 * tq]
    o_ref[0] = ot_ref[...].T.astype(BF16)


def _dsa(qa, qi, wi, ks, topk, n_bisect):
    b, s, _ = qa.shape
    tq = DSA_QBLK
    blk = lambda bi, qi_: (bi, qi_, 0)
    return pl.pallas_call(
        functools.partial(_dsa_body, topk=topk, n_bisect=n_bisect),
        grid=(b, s // tq),
        in_specs=[
            pl.BlockSpec((1, tq, 512), blk),
            pl.BlockSpec((1, tq, 256), blk),
            pl.BlockSpec((1, tq, LANES), blk),
            pl.BlockSpec((1, s, 384), lambda bi, qi_: (bi, 0, 0)),
        ],
        out_specs=pl.BlockSpec((1, tq, 512), blk),
        out_shape=jax.ShapeDtypeStruct((b, s, 512), BF16),
        scratch_shapes=[
            pltpu.VMEM((LANES, s), BF16),
            pltpu.VMEM((s, tq), F32),
            pltpu.VMEM((s, tq), F32),
            pltpu.VMEM((A_HEADS * HEAD_DIM, tq), F32),
        ],
        compiler_params=pltpu.CompilerParams(
            dimension_semantics=("arbitrary", "arbitrary"), vmem_limit_bytes=VMEM_LIMIT),
        name="dsa_attention",
    )(qa, qi, wi, ks)


def _diff_body(q_ref, k_ref, v_ref, lq1_ref, lk1_ref, lq2_ref, lk2_ref, o_ref, vt_ref,
               *, lambda_init):
    qblk = pl.program_id(2)
    seq = k_ref.shape[1]
    tq = q_ref.shape[1]

    @pl.when(qblk == 0)
    def _():
        vt_ref[...] = v_ref[0].astype(F32).T.astype(BF16)

    lam = (jnp.exp(jnp.sum(lq1_ref[...] * lk1_ref[...], axis=1, keepdims=True))
           - jnp.exp(jnp.sum(lq2_ref[...] * lk2_ref[...], axis=1, keepdims=True)) + lambda_init)

    q = q_ref[0]
    lane = lax.broadcasted_iota(jnp.int32, q.shape, 1)
    zero = jnp.zeros_like(q)
    q2 = jnp.concatenate([jnp.where(lane < HEAD_DIM, q, zero), jnp.where(lane >= HEAD_DIM, q, zero)],
                         axis=0)
    st = _dot_nt(k_ref[0], q2)
    kpos = lax.broadcasted_iota(jnp.int32, (seq, tq), 0)
    qpos = qblk * tq + lax.broadcasted_iota(jnp.int32, (seq, tq), 1)
    causal = kpos <= qpos
    causal2 = jnp.concatenate([causal, causal], axis=1)
    st = jnp.where(causal2, st, -jnp.inf)
    m = jnp.max(st, axis=0, keepdims=True)
    e = jnp.exp(st - m)
    p = e * (1.0 / jnp.sum(e, axis=0, keepdims=True))
    a = (p[:, 0:tq] - lam * p[:, tq:2 * tq]).astype(BF16)
    ot = _dot(vt_ref[...], a)
    ot = ot * lax.rsqrt(jnp.mean(ot * ot, axis=0, keepdims=True) + EPS) * (1.0 - lambda_init)
    o_ref[0] = ot.T.astype(BF16)


def _diff(qb, kb, vb, lq1, lk1, lq2, lk2, lambda_init):
    b, s, _ = qb.shape
    tq = min(DIFF_QBLK, s)
    vec = pl.BlockSpec((1, HEAD_DIM), lambda bi, h, qi_: (0, 0))
    return pl.pallas_call(
        functools.partial(_diff_body, lambda_init=lambda_init),
        grid=(b, B_HEADS, s // tq),
        in_specs=[
            pl.BlockSpec((1, tq, LANES), lambda bi, h, qi_: (bi, qi_, h)),
            pl.BlockSpec((1, s, LANES), lambda bi, h, qi_: (bi, 0, h)),
            pl.BlockSpec((1, s, LANES), lambda bi, h, qi_: (bi, 0, h)),
            vec, vec, vec, vec,
        ],
        out_specs=pl.BlockSpec((1, tq, LANES), lambda bi, h, qi_: (bi, qi_, h)),
        out_shape=jax.ShapeDtypeStruct((b, s, B_HEADS * LANES), BF16),
        scratch_shapes=[pltpu.VMEM((LANES, s), BF16)],
        compiler_params=pltpu.CompilerParams(
            dimension_semantics=("arbitrary", "arbitrary", "arbitrary"), vmem_limit_bytes=VMEM_LIMIT),
        name="diff_attention",
    )(qb, kb, vb, lq1, lk1, lq2, lk2)


def _memkv_body(mem_ref, g_ref, wk_ref, wv_ref, gk_ref, k_ref, v_ref):
    mem = mem_ref[0]
    memn = (mem * _rms_scale(mem) * g_ref[...]).astype(BF16)
    k = _dot(memn, wk_ref[...])
    hd = gk_ref.shape[1]
    for h in range(k.shape[1] // hd):
        kh = k[:, hd * h:hd * (h + 1)]
        k_ref[0, :, hd * h:hd * (h + 1)] = (kh * _rms_scale(kh) * gk_ref[...]).astype(BF16)
    v_ref[0] = _dot(memn, wv_ref[...]).astype(BF16)


def _memkv(mem, g_mem, w_xk, w_xv, g_xk):
    b, m, d = mem.shape
    const = lambda bi: (0, 0)
    blk = pl.BlockSpec((1, m, d), lambda bi: (bi, 0, 0))
    return pl.pallas_call(
        _memkv_body,
        grid=(b,),
        in_specs=[blk, pl.BlockSpec((1, d), const), pl.BlockSpec((d, d), const),
                  pl.BlockSpec((d, d), const), pl.BlockSpec(g_xk.shape, const)],
        out_specs=[blk, blk],
        out_shape=[jax.ShapeDtypeStruct((b, m, d), BF16)] * 2,
        compiler_params=pltpu.CompilerParams(
            dimension_semantics=("arbitrary",), vmem_limit_bytes=VMEM_LIMIT),
        name="mem_kv",
    )(mem, g_mem, w_xk, w_xv, g_xk)


def _xattn_body(x_ref, oa_ref, ob_ref, wo_ref, g_ref, wq_ref, gq_ref, k_ref, v_ref, wxo_ref, h_ref):
    half = oa_ref.shape[2]
    h1 = x_ref[0] + _dot(oa_ref[0], wo_ref[0:half, :]) + _dot(ob_ref[0], wo_ref[half:2 * half, :])
    hn = (h1 * _rms_scale(h1) * g_ref[...]).astype(BF16)
    q = _dot(hn, wq_ref[...])
    hd = gq_ref.shape[1]
    outs = []
    for h in range(q.shape[1] // hd):
        sl = slice(hd * h, hd * (h + 1))
        qh = q[:, sl]
        qh = (qh * _rms_scale(qh) * gq_ref[...] * (hd ** -0.5)).astype(BF16)
        s = _dot_nt(qh, k_ref[0, :, sl])
        e = jnp.exp(s - jnp.max(s, axis=-1, keepdims=True))
        p = (e * (1.0 / jnp.sum(e, axis=-1, keepdims=True))).astype(BF16)
        outs.append(_dot(p, v_ref[0, :, sl]).astype(BF16))
    o = jnp.concatenate(outs, axis=1)
    h_ref[0] = h1 + _dot(o, wxo_ref[...])


def _xattn(x, oa, ob, w_out, g_x, w_xq, g_xq, kmem, vmem, w_xo, tm):
    b, s, d = x.shape
    m = kmem.shape[1]
    const = lambda bi, ti: (0, 0)
    tok = lambda w: pl.BlockSpec((1, tm, w), lambda bi, ti: (bi, ti, 0))
    memblk = pl.BlockSpec((1, m, d), lambda bi, ti: (bi, 0, 0))
    return pl.pallas_call(
        _xattn_body,
        grid=(b, s // tm),
        in_specs=[tok(d), tok(oa.shape[2]), tok(ob.shape[2]),
                  pl.BlockSpec(w_out.shape, const), pl.BlockSpec((1, d), const),
                  pl.BlockSpec(w_xq.shape, const), pl.BlockSpec(g_xq.shape, const),
                  memblk, memblk, pl.BlockSpec(w_xo.shape, const)],
        out_specs=tok(d),
        out_shape=jax.ShapeDtypeStruct((b, s, d), F32),
        compiler_params=pltpu.CompilerParams(
            dimension_semantics=("arbitrary", "arbitrary"), vmem_limit_bytes=VMEM_LIMIT),
        name="outproj_xattn",
    )(x, oa, ob, w_out, g_x, w_xq, g_xq, kmem, vmem, w_xo)


HALO = 8


def _ffn_body(h_ref, g_ref, wa_ref, wg_ref, cw_ref, cb_ref, wo_ref, o_ref, a_ref):
    tm = h_ref.shape[1]

    @pl.when(pl.program_id(1) == 0)
    def _():
        a_ref[0:HALO, :] = jnp.zeros((HALO, a_ref.shape[1]), F32)

    h = h_ref[0]
    hn = (h * _rms_scale(h) * g_ref[...]).astype(BF16)
    a_ref[HALO:HALO + tm, :] = _dot(hn, wa_ref[...])
    gate = _dot(hn, wg_ref[...])
    conv = cb_ref[...]
    for j in range(CONV_W):
        off = HALO - (CONV_W - 1) + j
        conv = conv + a_ref[off:off + tm, :] * cw_ref[j:j + 1, :]
    a_ref[0:HALO, :] = a_ref[tm:tm + HALO, :]
    u = (jax.nn.gelu(conv) * gate).astype(BF16)
    o_ref[0] = h + _dot(u, wo_ref[...])


def _ffn(h, g_ffn, w_a, w_g, conv_w, conv_b, w_o, tm):
    b, s, d = h.shape
    dff = w_a.shape[1]
    const = lambda bi, ti: (0, 0)
    tok = pl.BlockSpec((1, tm, d), lambda bi, ti: (bi, ti, 0))
    return pl.pallas_call(
        _ffn_body,
        grid=(b, s // tm),
        in_specs=[tok, pl.BlockSpec((1, d), const), _const_spec(w_a.shape),
                  _const_spec(w_g.shape), pl.BlockSpec(conv_w.shape, const),
                  pl.BlockSpec((1, dff), const), _const_spec(w_o.shape)],
        out_specs=tok,
        out_shape=jax.ShapeDtypeStruct((b, s, d), F32),
        scratch_shapes=[pltpu.VMEM((tm + HALO, dff), F32)],
        compiler_params=pltpu.CompilerParams(
            dimension_semantics=("arbitrary", "arbitrary"), vmem_limit_bytes=VMEM_LIMIT),
        name="conv_glu",
    )(h, g_ffn, w_a, w_g, conv_w, conv_b, w_o)


def _rearranged_w_in(w_in):
    sizes = (A_HEADS * HEAD_DIM, HEAD_DIM, HEAD_DIM, IDX_HEADS * HEAD_DIM, HEAD_DIM, IDX_HEADS,
             2 * B_HEADS * HEAD_DIM, 2 * B_HEADS * HEAD_DIM, B_HEADS * 2 * HEAD_DIM)
    offs = [0]
    for sz in sizes:
        offs.append(offs[-1] + sz)
    q_a, k_a, v_a, q_i, k_i, w_i, q_b, k_b, v_b = [w_in[:, offs[i]:offs[i + 1]] for i in range(9)]
    pad = jnp.zeros((w_in.shape[0], HEAD_DIM - IDX_HEADS), w_in.dtype)
    w_all = jnp.concatenate([q_a, q_i, k_a, k_a, k_i, k_i, v_a, w_i, pad, q_b, k_b, v_b], axis=1)
    assert w_all.shape[1] == _C_END
    return w_all.astype(BF16)


def kernel(x, mem, positions, g_mix, w_in, g_qa, g_ka, g_qb, g_kb, lam_q1, lam_k1, lam_q2, lam_k2,
           w_out, g_xattn, g_mem, w_xq, w_xk, w_xv, w_xo, g_xq, g_xk, g_ffn, w_ffn_in, conv_w, conv_b,
           w_ffn_out):
    b, s, d = x.shape
    depth = g_mix.shape[0]
    topk = min(TOPK_MAX, s // 4)
    tm = min(512, s)

    inv_freq = 1.0 / (ROPE_THETA ** (jnp.arange(0, HEAD_DIM, 2, dtype=F32) / HEAD_DIM))
    ang = positions.astype(F32)[..., None] * inv_freq
    cos, sin = jnp.cos(ang), jnp.sin(ang)
    cos128 = jnp.concatenate([cos] * 4, axis=-1).reshape(b * s, LANES)
    sin128 = jnp.concatenate([-sin, sin, -sin, sin], axis=-1).reshape(b * s, LANES)
    blk = jnp.arange(MXU_DIM) // HEAD_DIM
    bd = jnp.where(blk[:, None] == blk[None, :], 1.0 / HEAD_DIM, 0.0).astype(BF16)

    h = x
    for l in range(depth):
        lambda_init = 0.8 - 0.6 * math.exp(-0.3 * l)
        gains = jnp.stack([jnp.tile(g, 512 // HEAD_DIM) for g in (g_qa[l], g_ka[l], g_qb[l], g_kb[l])])
        gains = jnp.concatenate([gains, jnp.ones((4, 512), F32)], axis=0)
        qa, qi, ks, wi, qb, kb, vb = _inproj(
            h.reshape(b * s, d), g_mix[l][None, :], _rearranged_w_in(w_in[l]), cos128, sin128, bd, gains, tm)
        r3 = lambda t: t.reshape(b, s, t.shape[-1])
        out_a = _dsa(r3(qa), r3(qi), r3(wi), r3(ks), topk, n_bisect=20)
        out_b = _diff(r3(qb), r3(kb), r3(vb), lam_q1[l][None, :], lam_k1[l][None, :],
                      lam_q2[l][None, :], lam_k2[l][None, :], lambda_init)
        kmem, vmem = _memkv(mem, g_mem[l][None, :], w_xk[l].astype(BF16), w_xv[l].astype(BF16),
                            g_xk[l][None, :])
        h = _xattn(h, out_a, out_b, w_out[l].astype(BF16), g_xattn[l][None, :], w_xq[l].astype(BF16),
                   g_xq[l][None, :], kmem, vmem, w_xo[l].astype(BF16), tm)
        dff = w_ffn_out.shape[1]
        w_ffn = w_ffn_in[l].astype(BF16)
        cw = jnp.concatenate([conv_w[l], jnp.zeros((8 - CONV_W, dff), F32)], axis=0)
        h = _ffn(h, g_ffn[l][None, :], w_ffn[:, :dff], w_ffn[:, dff:], cw, conv_b[l][None, :],
                 w_ffn_out[l].astype(BF16), min(256, s))
    return h
```

```python
import functools
import math

import jax
import jax.numpy as jnp
from jax import lax
from jax.experimental import pallas as pl
from jax.experimental.pallas import tpu as pltpu

F32 = jnp.float32
BF16 = jnp.bfloat16

EPS = 1e-6
ROPE_THETA = 10000.0
HEAD_DIM = 64
A_HEADS = 8
IDX_HEADS = 4
TOPK_MAX = 256
B_HEADS = 4
X_HEADS = 4
CONV_W = 3
LANES = 128
MXU_DIM = 256
DSA_QBLK = 128
DIFF_QBLK = 256
CAUSAL_STEP = 256
VMEM_LIMIT = 56 * 1024 * 1024

_C_QA = 0
_C_QI = 512
_C_KS = 768
_C_QB = 1152
_C_KB = 1664
_C_VB = 2176
_C_END = 2688


def _dot(a, b):
    return jnp.dot(a, b, preferred_element_type=F32)


def _dot_nt(a, b):
    return lax.dot_general(a, b, (((1,), (1,)), ((), ())), preferred_element_type=F32)


def _rms_scale(x):
    return lax.rsqrt(jnp.mean(x * x, axis=-1, keepdims=True) + EPS)


def _const_spec(shape):
    zeros = (0,) * len(shape)
    return pl.BlockSpec(shape, lambda *_: zeros, pipeline_mode=pl.Buffered(1))


def _lane_tile(t, width):
    reps = width // t.shape[1]
    return t if reps == 1 else jnp.concatenate([t] * reps, axis=1)


def _col_reduce(x, reduce_fn):
    rows, cols = x.shape
    slab = 8 * max(1, 8 * LANES // cols)
    if rows % slab or rows == slab:
        return reduce_fn(x, axis=0, keepdims=True)
    part = reduce_fn(x.reshape(rows // slab, slab, cols), axis=0)
    return reduce_fn(part, axis=0, keepdims=True)


def _inproj_body(x_ref, gmix_ref, w_ref, cos_ref, sin_ref, bd_ref, gains_ref,
                 qa_ref, qi_ref, ks_ref, wi_ref, qb_ref, kb_ref, vb_ref):
    x = x_ref[...]
    hn = (x * _rms_scale(x) * gmix_ref[...]).astype(BF16)
    cos = cos_ref[...]
    sin = sin_ref[...]
    bd = bd_ref[...]

    def proj(c0, width):
        return _dot(hn, w_ref[:, c0:c0 + width])

    def group_rms_scale(p):
        sq = p * p
        hi = sq.astype(BF16)
        lo = (sq - hi.astype(F32)).astype(BF16)
        outs = []
        for j in range(p.shape[1] // MXU_DIM):
            sl = slice(MXU_DIM * j, MXU_DIM * (j + 1))
            outs.append(_dot(hi[:, sl], bd) + _dot(lo[:, sl], bd))
        ms = outs[0] if len(outs) == 1 else jnp.concatenate(outs, axis=1)
        return lax.rsqrt(ms + EPS)

    def rope(y):
        width = y.shape[1]
        lane = lax.broadcasted_iota(jnp.int32, y.shape, 1)
        first_half = (lane & (HEAD_DIM - 1)) < (HEAD_DIM // 2)
        swapped = jnp.where(first_half, pltpu.roll(y, width - HEAD_DIM // 2, 1),
                            pltpu.roll(y, HEAD_DIM // 2, 1))
        return y * _lane_tile(cos, width) + swapped * _lane_tile(sin, width)

    sm_scale = HEAD_DIM ** -0.5

    p = proj(_C_QA, 512)
    qa_ref[...] = (rope(p * group_rms_scale(p) * gains_ref[0:1, :]) * sm_scale).astype(BF16)

    p = proj(_C_QI, 256)
    qi_ref[...] = rope(p).astype(BF16)

    p = proj(_C_KS, 384)
    p01 = p[:, 0:256]
    lane = lax.broadcasted_iota(jnp.int32, p01.shape, 1)
    y01 = jnp.where(lane < LANES, p01 * group_rms_scale(p01) * gains_ref[1:2, 0:256], p01)
    y01 = rope(y01)
    p2 = p[:, 256:384]
    lane = lax.broadcasted_iota(jnp.int32, p2.shape, 1)
    p2 = jnp.where(lane < HEAD_DIM, p2, p2 * (IDX_HEADS ** -0.5 * HEAD_DIM ** -0.5))
    ks_ref[:, 0:256] = y01.astype(BF16)
    ks_ref[:, 256:384] = p2.astype(BF16)
    wi_ref[...] = p2

    p = proj(_C_QB, 512)
    qb_ref[...] = (rope(p * group_rms_scale(p) * gains_ref[2:3, :]) * sm_scale).astype(BF16)

    p = proj(_C_KB, 512)
    kb_ref[...] = rope(p * group_rms_scale(p) * gains_ref[3:4, :]).astype(BF16)

    vb_ref[...] = proj(_C_VB, 512).astype(BF16)


def _inproj(x2, gmix, w_all, cos128, sin128, bd, gains, tm):
    n, d = x2.shape
    row = lambda i: (i, 0)
    const = lambda i: (0, 0)
    outs = [(512, BF16), (256, BF16), (384, BF16), (LANES, F32), (512, BF16), (512, BF16), (512, BF16)]
    return pl.pallas_call(
        _inproj_body,
        grid=(n // tm,),
        in_specs=[
            pl.BlockSpec((tm, d), row),
            pl.BlockSpec((1, d), const),
            pl.BlockSpec(w_all.shape, const),
            pl.BlockSpec((tm, LANES), row),
            pl.BlockSpec((tm, LANES), row),
            pl.BlockSpec(bd.shape, const),
            pl.BlockSpec(gains.shape, const),
        ],
        out_specs=[pl.BlockSpec((tm, w), row) for w, _ in outs],
        out_shape=[jax.ShapeDtypeStruct((n, w), dt) for w, dt in outs],
        compiler_params=pltpu.CompilerParams(
            dimension_semantics=("arbitrary",), vmem_limit_bytes=VMEM_LIMIT),
        name="inproj",
    )(x2, gmix, w_all, cos128, sin128, bd, gains)


def _dsa_keys(klen, qblk, qa_ref, qi_ref, wi_ref, ks_ref, o_ref, vt_ref, sc_ref, bias_ref, ot_ref,
              topk, n_bisect):
    tq = DSA_QBLK
    tail = klen - CAUSAL_STEP
    neg_inf = -jnp.inf

    lane_q = lax.broadcasted_iota(jnp.int32, (tq, LANES), 1)
    first_head = lane_q < HEAD_DIM

    def head_rows(slab, odd):
        return jnp.where(first_head != odd, slab, jnp.zeros_like(slab))

    qi = qi_ref[0]
    qi_stack = jnp.concatenate(
        [head_rows(qi[:, LANES * (h // 2):LANES * (h // 2 + 1)], bool(h % 2)) for h in range(IDX_HEADS)],
        axis=0)
    lg = _dot_nt(ks_ref[0, 0:klen, 128:256], qi_stack)
    w_t = wi_ref[0].T
    sc = None
    for h in range(IDX_HEADS):
        term = jnp.maximum(lg[:, tq * h:tq * (h + 1)], 0.0) * w_t[HEAD_DIM + h:HEAD_DIM + h + 1, :]
        sc = term if sc is None else sc + term
    kpos_t = tail + lax.broadcasted_iota(jnp.int32, (CAUSAL_STEP, tq), 0)
    qpos_t = qblk * tq + lax.broadcasted_iota(jnp.int32, (CAUSAL_STEP, tq), 1)
    causal_t = kpos_t <= qpos_t
    if tail:
        sc_ref[0:tail, :] = sc[0:tail]
    sc_ref[tail:klen, :] = jnp.where(causal_t, sc[tail:klen], neg_inf)

    kf = float(topk)
    qpos_row = qblk * tq + lax.broadcasted_iota(jnp.int32, (1, tq), 1)
    search = qpos_row >= topk

    def scores():
        return sc_ref[0:klen, :]

    def count(pred):
        return _col_reduce(jnp.where(pred, 1.0, 0.0), jnp.sum)

    def max_below(bound):
        s = scores()
        return _col_reduce(jnp.where(s < bound, s, neg_inf), jnp.max)

    hi0 = _col_reduce(scores(), jnp.max)
    lo0 = _col_reduce(jnp.where(causal_t, sc_ref[tail:klen, :], jnp.inf), jnp.min)
    if tail:
        lo0 = jnp.minimum(lo0, _col_reduce(sc_ref[0:tail, :], jnp.min))

    def bisect(_, carry):
        lo, hi = carry
        mid = 0.5 * (lo + hi)
        ge = count(scores() >= mid) >= kf
        return jnp.where(ge, mid, lo), jnp.where(ge, hi, mid)

    _, hi = lax.fori_loop(0, n_bisect, bisect, (lo0, hi0))

    s = scores()
    cand0 = _col_reduce(jnp.where(s <= hi, s, neg_inf), jnp.max)

    def unsettled(n_ge):
        return jnp.logical_and(n_ge < kf, search)

    def any_lane(pred):
        return jnp.max(jnp.where(pred, 1.0, 0.0))

    def refine(carry):
        cand, n_ge, _ = carry
        cand = jnp.where(unsettled(n_ge), max_below(cand), cand)
        n_ge = count(scores() >= cand)
        return cand, n_ge, any_lane(unsettled(n_ge))

    n_ge0 = count(scores() >= cand0)
    cand, n_ge, _ = lax.while_loop(lambda c: c[2] > 0.0, refine,
                                   (cand0, n_ge0, any_lane(unsettled(n_ge0))))

    need = kf - count(scores() > cand)
    ambiguous = jnp.logical_and(search, n_ge - (kf - need) > need)
    open_row = jnp.where(search, neg_inf, 0.0)
    tied = any_lane(ambiguous) > 0.0

    def store_bias(bias):
        if tail:
            bias_ref[0:tail, :] = bias[0:tail]
        bias_ref[tail:klen, :] = jnp.where(causal_t, bias[tail:klen], neg_inf)

    @pl.when(jnp.logical_not(tied))
    def _():
        store_bias(jnp.maximum(jnp.where(scores() >= cand, 0.0, neg_inf), open_row))

    @pl.when(tied)
    def _():
        kpos = lax.broadcasted_iota(jnp.int32, (klen, tq), 0)

        def tie_bisect(_, carry):
            jlo, jhi = carry
            mid = (jlo + jhi) >> 1
            cnt = count(jnp.where(kpos <= mid, scores(), neg_inf) == cand)
            ge = cnt >= need
            return jnp.where(ge, jlo, mid), jnp.where(ge, mid, jhi)

        n_tie = max(1, math.ceil(math.log2(klen + 1)))
        _, jlast = lax.fori_loop(
            0, n_tie, tie_bisect,
            (jnp.full((1, tq), -1, jnp.int32), jnp.full((1, tq), klen - 1, jnp.int32)))
        s = scores()
        tie_bias = jnp.where(s == cand, jnp.where(kpos <= jlast, 0.0, neg_inf), neg_inf)
        store_bias(jnp.maximum(jnp.where(s > cand, 0.0, tie_bias), open_row))

    qa = qa_ref[0]
    q_all = jnp.concatenate(
        [head_rows(qa[:, LANES * (h // 2):LANES * (h // 2 + 1)], bool(h % 2)) for h in range(A_HEADS)],
        axis=0)
    bias = bias_ref[0:klen, :]
    st = _dot_nt(ks_ref[0, 0:klen, 0:128], q_all) + jnp.concatenate([bias] * A_HEADS, axis=1)
    m = _col_reduce(st, jnp.max)
    e = jnp.exp(st - m)
    l = _col_reduce(e, jnp.sum)
    ot = _dot(vt_ref[0:HEAD_DIM, 0:klen], e.astype(BF16)) * (1.0 / l)
    for h in range(A_HEADS):
        ot_ref[HEAD_DIM * h:HEAD_DIM * (h + 1), :] = ot[:, tq * h:tq * (h + 1)]
    o_ref[0] = ot_ref[...].T.astype(BF16)


def _dsa_body(qa_ref, qi_ref, wi_ref, ks_ref, o_ref, vt_ref, sc_ref, bias_ref, ot_ref,
              *, topk, n_bisect):
    qblk = pl.program_id(1)
    seq = ks_ref.shape[1]

    @pl.when(qblk == 0)
    def _():
        vt_ref[...] = ks_ref[0, :, 256:384].astype(F32).T.astype(BF16)

    blocks_per_step = CAUSAL_STEP // DSA_QBLK
    for c in range(seq // CAUSAL_STEP):
        @pl.when(qblk // blocks_per_step == c)
        def _(c=c):
            _dsa_keys(CAUSAL_STEP * (c + 1), qblk, qa_ref, qi_ref, wi_ref, ks_ref, o_ref, vt_ref,
                      sc_ref, bias_ref, ot_ref, topk, n_bisect)


def _dsa(qa, qi, wi, ks, topk, n_bisect):
    b, s, _ = qa.shape
    tq = DSA_QBLK
    blk = lambda bi, qi_: (bi, qi_, 0)
    return pl.pallas_call(
        functools.partial(_dsa_body, topk=topk, n_bisect=n_bisect),
        grid=(b, s // tq),
        in_specs=[
            pl.BlockSpec((1, tq, 512), blk),
            pl.BlockSpec((1, tq, 256), blk),
            pl.BlockSpec((1, tq, LANES), blk),
            pl.BlockSpec((1, s, 384), lambda bi, qi_: (bi, 0, 0)),
        ],
        out_specs=pl.BlockSpec((1, tq, 512), blk),
        out_shape=jax.ShapeDtypeStruct((b, s, 512), BF16),
        scratch_shapes=[
            pltpu.VMEM((LANES, s), BF16),
            pltpu.VMEM((s, tq), F32),
            pltpu.VMEM((s, tq), F32),
            pltpu.VMEM((A_HEADS * HEAD_DIM, tq), F32),
        ],
        compiler_params=pltpu.CompilerParams(
            dimension_semantics=("arbitrary", "arbitrary"), vmem_limit_bytes=VMEM_LIMIT),
        name="dsa_attention",
    )(qa, qi, wi, ks)


def _diff_body(q_ref, k_ref, v_ref, lq1_ref, lk1_ref, lq2_ref, lk2_ref, o_ref, vt_ref, st_ref,
               *, lambda_init):
    qblk = pl.program_id(2)
    seq = k_ref.shape[1]
    tq = q_ref.shape[1]

    @pl.when(qblk == 0)
    def _():
        vt_ref[...] = v_ref[0].astype(F32).T.astype(BF16)

    lam = (jnp.exp(jnp.sum(lq1_ref[...] * lk1_ref[...], axis=1, keepdims=True))
           - jnp.exp(jnp.sum(lq2_ref[...] * lk2_ref[...], axis=1, keepdims=True)) + lambda_init)

    q = q_ref[0]
    lane = lax.broadcasted_iota(jnp.int32, q.shape, 1)
    zero = jnp.zeros_like(q)
    q2 = jnp.concatenate([jnp.where(lane < HEAD_DIM, q, zero), jnp.where(lane >= HEAD_DIM, q, zero)],
                         axis=0)
    diag = (lax.broadcasted_iota(jnp.int32, (tq, 2 * tq), 0)
            <= (lax.broadcasted_iota(jnp.int32, (tq, 2 * tq), 1) & (tq - 1)))

    def attend(klen):
        tail = klen - tq
        st = _dot_nt(k_ref[0, 0:klen, :], q2)
        if tail:
            st_ref[0:tail, :] = st[0:tail]
        st_ref[tail:klen, :] = jnp.where(diag, st[tail:klen], -jnp.inf)
        m = _col_reduce(st_ref[0:klen, :], jnp.max)
        e = jnp.exp(st_ref[0:klen, :] - m)
        l = _col_reduce(e, jnp.sum)
        acc = _dot(vt_ref[:, 0:klen], e.astype(BF16)) * (1.0 / l)
        ot = acc[:, 0:tq] - lam * acc[:, tq:2 * tq]
        ot = ot * lax.rsqrt(jnp.mean(ot * ot, axis=0, keepdims=True) + EPS) * (1.0 - lambda_init)
        o_ref[0] = ot.T.astype(BF16)

    for c in range(seq // tq):
        @pl.when(qblk == c)
        def _(c=c):
            attend(tq * (c + 1))


def _diff(qb, kb, vb, lq1, lk1, lq2, lk2, lambda_init):
    b, s, _ = qb.shape
    tq = min(DIFF_QBLK, s)
    vec = pl.BlockSpec((1, HEAD_DIM), lambda bi, h, qi_: (0, 0))
    return pl.pallas_call(
        functools.partial(_diff_body, lambda_init=lambda_init),
        grid=(b, B_HEADS, s // tq),
        in_specs=[
            pl.BlockSpec((1, tq, LANES), lambda bi, h, qi_: (bi, qi_, h)),
            pl.BlockSpec((1, s, LANES), lambda bi, h, qi_: (bi, 0, h)),
            pl.BlockSpec((1, s, LANES), lambda bi, h, qi_: (bi, 0, h)),
            vec, vec, vec, vec,
        ],
        out_specs=pl.BlockSpec((1, tq, LANES), lambda bi, h, qi_: (bi, qi_, h)),
        out_shape=jax.ShapeDtypeStruct((b, s, B_HEADS * LANES), BF16),
        scratch_shapes=[pltpu.VMEM((LANES, s), BF16), pltpu.VMEM((s, 2 * tq), F32)],
        compiler_params=pltpu.CompilerParams(
            dimension_semantics=("arbitrary", "arbitrary", "arbitrary"), vmem_limit_bytes=VMEM_LIMIT),
        name="diff_attention",
    )(qb, kb, vb, lq1, lk1, lq2, lk2)


def _memkv_body(mem_ref, g_ref, wk_ref, wv_ref, gk_ref, k_ref, v_ref):
    mem = mem_ref[0]
    memn = (mem * _rms_scale(mem) * g_ref[...]).astype(BF16)
    k = _dot(memn, wk_ref[...])
    hd = gk_ref.shape[1]
    for h in range(k.shape[1] // hd):
        kh = k[:, hd * h:hd * (h + 1)]
        k_ref[0, :, hd * h:hd * (h + 1)] = (kh * _rms_scale(kh) * gk_ref[...]).astype(BF16)
    v_ref[0] = _dot(memn, wv_ref[...]).astype(BF16)


def _memkv(mem, g_mem, w_xk, w_xv, g_xk):
    b, m, d = mem.shape
    const = lambda bi: (0, 0)
    blk = pl.BlockSpec((1, m, d), lambda bi: (bi, 0, 0))
    return pl.pallas_call(
        _memkv_body,
        grid=(b,),
        in_specs=[blk, pl.BlockSpec((1, d), const), pl.BlockSpec((d, d), const),
                  pl.BlockSpec((d, d), const), pl.BlockSpec(g_xk.shape, const)],
        out_specs=[blk, blk],
        out_shape=[jax.ShapeDtypeStruct((b, m, d), BF16)] * 2,
        compiler_params=pltpu.CompilerParams(
            dimension_semantics=("arbitrary",), vmem_limit_bytes=VMEM_LIMIT),
        name="mem_kv",
    )(mem, g_mem, w_xk, w_xv, g_xk)


def _xattn_body(x_ref, oa_ref, ob_ref, wo_ref, g_ref, wq_ref, gq_ref, k_ref, v_ref, wxo_ref, h_ref):
    half = oa_ref.shape[2]
    h1 = x_ref[0] + _dot(oa_ref[0], wo_ref[0:half, :]) + _dot(ob_ref[0], wo_ref[half:2 * half, :])
    hn = (h1 * _rms_scale(h1) * g_ref[...]).astype(BF16)
    q = _dot(hn, wq_ref[...])
    hd = gq_ref.shape[1]
    outs = []
    for h in range(q.shape[1] // hd):
        sl = slice(hd * h, hd * (h + 1))
        qh = q[:, sl]
        qh = (qh * _rms_scale(qh) * gq_ref[...] * (hd ** -0.5)).astype(BF16)
        s = _dot_nt(qh, k_ref[0, :, sl])
        e = jnp.exp(s - jnp.max(s, axis=-1, keepdims=True))
        p = (e * (1.0 / jnp.sum(e, axis=-1, keepdims=True))).astype(BF16)
        outs.append(_dot(p, v_ref[0, :, sl]).astype(BF16))
    o = jnp.concatenate(outs, axis=1)
    h_ref[0] = h1 + _dot(o, wxo_ref[...])


def _xattn(x, oa, ob, w_out, g_x, w_xq, g_xq, kmem, vmem, w_xo, tm):
    b, s, d = x.shape
    m = kmem.shape[1]
    const = lambda bi, ti: (0, 0)
    tok = lambda w: pl.BlockSpec((1, tm, w), lambda bi, ti: (bi, ti, 0))
    memblk = pl.BlockSpec((1, m, d), lambda bi, ti: (bi, 0, 0))
    return pl.pallas_call(
        _xattn_body,
        grid=(b, s // tm),
        in_specs=[tok(d), tok(oa.shape[2]), tok(ob.shape[2]),
                  pl.BlockSpec(w_out.shape, const), pl.BlockSpec((1, d), const),
                  pl.BlockSpec(w_xq.shape, const), pl.BlockSpec(g_xq.shape, const),
                  memblk, memblk, pl.BlockSpec(w_xo.shape, const)],
        out_specs=tok(d),
        out_shape=jax.ShapeDtypeStruct((b, s, d), F32),
        compiler_params=pltpu.CompilerParams(
            dimension_semantics=("arbitrary", "arbitrary"), vmem_limit_bytes=VMEM_LIMIT),
        name="outproj_xattn",
    )(x, oa, ob, w_out, g_x, w_xq, g_xq, kmem, vmem, w_xo)


HALO = 8


def _ffn_body(h_ref, g_ref, wa_ref, wg_ref, cw_ref, cb_ref, wo_ref, o_ref, a_ref):
    tm = h_ref.shape[1]

    @pl.when(pl.program_id(1) == 0)
    def _():
        a_ref[0:HALO, :] = jnp.zeros((HALO, a_ref.shape[1]), F32)

    h = h_ref[0]
    hn = (h * _rms_scale(h) * g_ref[...]).astype(BF16)
    a_ref[HALO:HALO + tm, :] = _dot(hn, wa_ref[...])
    gate = _dot(hn, wg_ref[...])
    conv = cb_ref[...]
    for j in range(CONV_W):
        off = HALO - (CONV_W - 1) + j
        conv = conv + a_ref[off:off + tm, :] * cw_ref[j:j + 1, :]
    a_ref[0:HALO, :] = a_ref[tm:tm + HALO, :]
    u = (jax.nn.gelu(conv) * gate).astype(BF16)
    o_ref[0] = h + _dot(u, wo_ref[...])


def _ffn(h, g_ffn, w_a, w_g, conv_w, conv_b, w_o, tm):
    b, s, d = h.shape
    dff = w_a.shape[1]
    const = lambda bi, ti: (0, 0)
    tok = pl.BlockSpec((1, tm, d), lambda bi, ti: (bi, ti, 0))
    return pl.pallas_call(
        _ffn_body,
        grid=(b, s // tm),
        in_specs=[tok, pl.BlockSpec((1, d), const), _const_spec(w_a.shape),
                  _const_spec(w_g.shape), pl.BlockSpec(conv_w.shape, const),
                  pl.BlockSpec((1, dff), const), _const_spec(w_o.shape)],
        out_specs=tok,
        out_shape=jax.ShapeDtypeStruct((b, s, d), F32),
        scratch_shapes=[pltpu.VMEM((tm + HALO, dff), F32)],
        compiler_params=pltpu.CompilerParams(
            dimension_semantics=("arbitrary", "arbitrary"), vmem_limit_bytes=VMEM_LIMIT),
        name="conv_glu",
    )(h, g_ffn, w_a, w_g, conv_w, conv_b, w_o)


def _rearranged_w_in(w_in):
    sizes = (A_HEADS * HEAD_DIM, HEAD_DIM, HEAD_DIM, IDX_HEADS * HEAD_DIM, HEAD_DIM, IDX_HEADS,
             2 * B_HEADS * HEAD_DIM, 2 * B_HEADS * HEAD_DIM, B_HEADS * 2 * HEAD_DIM)
    offs = [0]
    for sz in sizes:
        offs.append(offs[-1] + sz)
    q_a, k_a, v_a, q_i, k_i, w_i, q_b, k_b, v_b = [w_in[:, offs[i]:offs[i + 1]] for i in range(9)]
    pad = jnp.zeros((w_in.shape[0], HEAD_DIM - IDX_HEADS), w_in.dtype)
    w_all = jnp.concatenate([q_a, q_i, k_a, k_a, k_i, k_i, v_a, w_i, pad, q_b, k_b, v_b], axis=1)
    assert w_all.shape[1] == _C_END
    return w_all.astype(BF16)


def kernel(x, mem, positions, g_mix, w_in, g_qa, g_ka, g_qb, g_kb, lam_q1, lam_k1, lam_q2, lam_k2,
           w_out, g_xattn, g_mem, w_xq, w_xk, w_xv, w_xo, g_xq, g_xk, g_ffn, w_ffn_in, conv_w, conv_b,
           w_ffn_out):
    b, s, d = x.shape
    depth = g_mix.shape[0]
    topk = min(TOPK_MAX, s // 4)
    tm = min(512, s)

    inv_freq = 1.0 / (ROPE_THETA ** (jnp.arange(0, HEAD_DIM, 2, dtype=F32) / HEAD_DIM))
    ang = positions.astype(F32)[..., None] * inv_freq
    cos, sin = jnp.cos(ang), jnp.sin(ang)
    cos128 = jnp.concatenate([cos] * 4, axis=-1).reshape(b * s, LANES)
    sin128 = jnp.concatenate([-sin, sin, -sin, sin], axis=-1).reshape(b * s, LANES)
    blk = jnp.arange(MXU_DIM) // HEAD_DIM
    bd = jnp.where(blk[:, None] == blk[None, :], 1.0 / HEAD_DIM, 0.0).astype(BF16)

    h = x
    for l in range(depth):
        lambda_init = 0.8 - 0.6 * math.exp(-0.3 * l)
        gains = jnp.stack([jnp.tile(g, 512 // HEAD_DIM) for g in (g_qa[l], g_ka[l], g_qb[l], g_kb[l])])
        gains = jnp.concatenate([gains, jnp.ones((4, 512), F32)], axis=0)
        qa, qi, ks, wi, qb, kb, vb = _inproj(
            h.reshape(b * s, d), g_mix[l][None, :], _rearranged_w_in(w_in[l]), cos128, sin128, bd, gains, tm)
        r3 = lambda t: t.reshape(b, s, t.shape[-1])
        out_a = _dsa(r3(qa), r3(qi), r3(wi), r3(ks), topk, n_bisect=20)
        out_b = _diff(r3(qb), r3(kb), r3(vb), lam_q1[l][None, :], lam_k1[l][None, :],
                      lam_q2[l][None, :], lam_k2[l][None, :], lambda_init)
        kmem, vmem = _memkv(mem, g_mem[l][None, :], w_xk[l].astype(BF16), w_xv[l].astype(BF16),
                            g_xk[l][None, :])
        h = _xattn(h, out_a, out_b, w_out[l].astype(BF16), g_xattn[l][None, :], w_xq[l].astype(BF16),
                   g_xq[l][None, :], kmem, vmem, w_xo[l].astype(BF16), tm)
        dff = w_ffn_out.shape[1]
        w_ffn = w_ffn_in[l].astype(BF16)
        cw = jnp.concatenate([conv_w[l], jnp.zeros((8 - CONV_W, dff), F32)], axis=0)
        h = _ffn(h, g_ffn[l][None, :], w_ffn[:, :dff], w_ffn[:, dff:], cw, conv_b[l][None, :],
                 w_ffn_out[l].astype(BF16), min(256, s))
    return h
```

```python
import functools
import math

import jax
import jax.numpy as jnp
from jax import lax
from jax.experimental import pallas as pl
from jax.experimental.pallas import tpu as pltpu

F32 = jnp.float32
BF16 = jnp.bfloat16

EPS = 1e-6
ROPE_THETA = 10000.0
HEAD_DIM = 64
A_HEADS = 8
IDX_HEADS = 4
TOPK_MAX = 256
B_HEADS = 4
X_HEADS = 4
CONV_W = 3
LANES = 128
MXU_DIM = 256
DSA_QBLK = 128
DIFF_QBLK = 256
CAUSAL_STEP = MXU_DIM
VMEM_LIMIT = 56 * 1024 * 1024
LOG2E = 1.4426950408889634
SHIFT_MARGIN = 1.02
DENOM_FLOOR = 2.0 ** -40

_C_QA = 0
_C_QI = 512
_C_KS = 768
_C_QB = 1152
_C_KB = 1664
_C_VB = 2176
_C_END = 2688


def _dot(a, b):
    return jnp.dot(a, b, preferred_element_type=F32)


def _dot_nt(a, b):
    return lax.dot_general(a, b, (((1,), (1,)), ((), ())), preferred_element_type=F32)


def _rms_scale(x):
    return lax.rsqrt(jnp.mean(x * x, axis=-1, keepdims=True) + EPS)


def _const_spec(shape):
    zeros = (0,) * len(shape)
    return pl.BlockSpec(shape, lambda *_: zeros, pipeline_mode=pl.Buffered(1))


def _lane_tile(t, width):
    reps = width // t.shape[1]
    return t if reps == 1 else jnp.concatenate([t] * reps, axis=1)


def _col_reduce(x, reduce_fn):
    rows, cols = x.shape
    slab = 8 * max(1, 8 * LANES // cols)
    if rows % slab or rows == slab:
        return reduce_fn(x, axis=0, keepdims=True)
    part = reduce_fn(x.reshape(rows // slab, slab, cols), axis=0)
    return reduce_fn(part, axis=0, keepdims=True)


def _inproj_body(x_ref, gmix_ref, w_ref, cos_ref, sin_ref, bd_ref, gains_ref,
                 qa_ref, qi_ref, ks_ref, wi_ref, qb_ref, kb_ref, vb_ref):
    x = x_ref[...]
    hn = (x * _rms_scale(x) * gmix_ref[...]).astype(BF16)
    cos = cos_ref[...]
    sin = sin_ref[...]
    bd = bd_ref[...]

    def proj(c0, width):
        return _dot(hn, w_ref[:, c0:c0 + width])

    def group_rms_scale(p):
        sq = p * p
        hi = sq.astype(BF16)
        lo = (sq - hi.astype(F32)).astype(BF16)
        outs = []
        for j in range(p.shape[1] // MXU_DIM):
            sl = slice(MXU_DIM * j, MXU_DIM * (j + 1))
            outs.append(_dot(hi[:, sl], bd) + _dot(lo[:, sl], bd))
        ms = outs[0] if len(outs) == 1 else jnp.concatenate(outs, axis=1)
        return lax.rsqrt(ms + EPS)

    def rope(y):
        width = y.shape[1]
        lane = lax.broadcasted_iota(jnp.int32, y.shape, 1)
        first_half = (lane & (HEAD_DIM - 1)) < (HEAD_DIM // 2)
        swapped = jnp.where(first_half, pltpu.roll(y, width - HEAD_DIM // 2, 1),
                            pltpu.roll(y, HEAD_DIM // 2, 1))
        return y * _lane_tile(cos, width) + swapped * _lane_tile(sin, width)

    sm_scale = HEAD_DIM ** -0.5 * LOG2E

    p = proj(_C_QA, 512)
    qa_ref[...] = (rope(p * group_rms_scale(p) * gains_ref[0:1, :]) * sm_scale).astype(BF16)

    p = proj(_C_QI, 256)
    qi_ref[...] = rope(p).astype(BF16)

    p = proj(_C_KS, 384)
    p01 = p[:, 0:256]
    lane = lax.broadcasted_iota(jnp.int32, p01.shape, 1)
    y01 = jnp.where(lane < LANES, p01 * group_rms_scale(p01) * gains_ref[1:2, 0:256], p01)
    y01 = rope(y01)
    p2 = p[:, 256:384]
    lane = lax.broadcasted_iota(jnp.int32, p2.shape, 1)
    p2 = jnp.where(lane < HEAD_DIM, p2, p2 * (IDX_HEADS ** -0.5 * HEAD_DIM ** -0.5))
    ks_ref[:, 0:256] = y01.astype(BF16)
    ks_ref[:, 256:384] = p2.astype(BF16)
    wi_ref[...] = p2

    p = proj(_C_QB, 512)
    qb_ref[...] = (rope(p * group_rms_scale(p) * gains_ref[2:3, :]) * sm_scale).astype(BF16)

    p = proj(_C_KB, 512)
    kb_ref[...] = rope(p * group_rms_scale(p) * gains_ref[3:4, :]).astype(BF16)

    vb_ref[...] = proj(_C_VB, 512).astype(BF16)


def _inproj(x2, gmix, w_all, cos128, sin128, bd, gains, tm):
    n, d = x2.shape
    row = lambda i: (i, 0)
    const = lambda i: (0, 0)
    outs = [(512, BF16), (256, BF16), (384, BF16), (LANES, F32), (512, BF16), (512, BF16), (512, BF16)]
    return pl.pallas_call(
        _inproj_body,
        grid=(n // tm,),
        in_specs=[
            pl.BlockSpec((tm, d), row),
            pl.BlockSpec((1, d), const),
            pl.BlockSpec(w_all.shape, const),
            pl.BlockSpec((tm, LANES), row),
            pl.BlockSpec((tm, LANES), row),
            pl.BlockSpec(bd.shape, const),
            pl.BlockSpec(gains.shape, const),
        ],
        out_specs=[pl.BlockSpec((tm, w), row) for w, _ in outs],
        out_shape=[jax.ShapeDtypeStruct((n, w), dt) for w, dt in outs],
        compiler_params=pltpu.CompilerParams(
            dimension_semantics=("arbitrary",), vmem_limit_bytes=VMEM_LIMIT),
        name="inproj",
    )(x2, gmix, w_all, cos128, sin128, bd, gains)


def _dsa_queries(qa_ref):
    qa = qa_ref[0]
    lane = lax.broadcasted_iota(jnp.int32, (DSA_QBLK, LANES), 1)
    rows = []
    for h in range(A_HEADS):
        slab = qa[:, LANES * (h // 2):LANES * (h // 2 + 1)]
        keep = (lane >= HEAD_DIM) if h % 2 else (lane < HEAD_DIM)
        rows.append(jnp.where(keep, slab, jnp.zeros_like(slab)))
    return jnp.concatenate(rows, axis=0)


def _dsa_write(ot, l, o_ref, ot_ref):
    tq = DSA_QBLK
    ot = ot * (1.0 / l)
    for h in range(A_HEADS):
        ot_ref[HEAD_DIM * h:HEAD_DIM * (h + 1), :] = ot[:, tq * h:tq * (h + 1)]
    o_ref[0] = ot_ref[...].T.astype(BF16)


def _dsa_keys(klen, qblk, qa_ref, qi_ref, wi_ref, ks_ref, o_ref, vt_ref, sc_ref, bias_ref, ot_ref,
              kmax_ref, flag_ref, topk, n_bisect):
    tq = DSA_QBLK
    seq = ks_ref.shape[1]
    tail = klen - CAUSAL_STEP
    neg_inf = -jnp.inf

    lane_q = lax.broadcasted_iota(jnp.int32, (tq, LANES), 1)
    first_head = lane_q < HEAD_DIM

    def head_rows(slab, odd):
        return jnp.where(first_head != odd, slab, jnp.zeros_like(slab))

    qi = qi_ref[0]
    qi_stack = jnp.concatenate(
        [head_rows(qi[:, LANES * (h // 2):LANES * (h // 2 + 1)], bool(h % 2)) for h in range(IDX_HEADS)],
        axis=0)
    lg = _dot_nt(ks_ref[0, 0:klen, 128:256], qi_stack)
    w_t = wi_ref[0].T
    sc = None
    for h in range(IDX_HEADS):
        term = jnp.maximum(lg[:, tq * h:tq * (h + 1)], 0.0) * w_t[HEAD_DIM + h:HEAD_DIM + h + 1, :]
        sc = term if sc is None else sc + term
    kpos_t = tail + lax.broadcasted_iota(jnp.int32, (CAUSAL_STEP, tq), 0)
    qpos_t = qblk * tq + lax.broadcasted_iota(jnp.int32, (CAUSAL_STEP, tq), 1)
    causal_t = kpos_t <= qpos_t
    if tail:
        sc_ref[0:tail, :] = sc[0:tail]
    sc_ref[tail:klen, :] = jnp.where(causal_t, sc[tail:klen], neg_inf)

    kf = float(topk)
    qpos_row = qblk * tq + lax.broadcasted_iota(jnp.int32, (1, tq), 1)
    search = qpos_row >= topk

    def scores():
        return sc_ref[0:klen, :]

    def count(pred):
        return _col_reduce(jnp.where(pred, 1.0, 0.0), jnp.sum)

    hi0 = _col_reduce(scores(), jnp.max)
    lo0 = _col_reduce(jnp.where(causal_t, sc_ref[tail:klen, :], jnp.inf), jnp.min)
    if tail:
        lo0 = jnp.minimum(lo0, _col_reduce(sc_ref[0:tail, :], jnp.min))

    def bisect(_, carry):
        lo, hi = carry
        mid = 0.5 * (lo + hi)
        ge = count(scores() >= mid) >= kf
        return jnp.where(ge, mid, lo), jnp.where(ge, hi, mid)

    lo, _ = lax.fori_loop(0, n_bisect, bisect, (lo0, hi0))

    def too_low(n_gt):
        return jnp.max(jnp.where(jnp.logical_and(search, n_gt >= kf), 1.0, 0.0))

    def climb(carry):
        thr, n_gt, _ = carry
        s = scores()
        nxt = _col_reduce(jnp.where(s > thr, s, jnp.inf), jnp.min)
        thr = jnp.where(jnp.logical_and(search, n_gt >= kf), nxt, thr)
        n_gt = count(scores() > thr)
        return thr, n_gt, too_low(n_gt)

    s = scores()
    thr0 = _col_reduce(jnp.where(s >= lo, s, jnp.inf), jnp.min)
    n_gt0 = count(s > thr0)
    thr, n_gt, _ = lax.while_loop(lambda c: c[2] > 0.0, climb, (thr0, n_gt0, too_low(n_gt0)))

    need = kf - n_gt
    open_row = jnp.where(search, neg_inf, 0.0)
    n_blk = klen // MXU_DIM
    s = scores()
    tie = jnp.where(s == thr, 1.0, 0.0).astype(BF16)
    tie_cat = jnp.concatenate([tie[MXU_DIM * j:MXU_DIM * (j + 1)] for j in range(n_blk)], axis=1)
    tri = jnp.where(lax.broadcasted_iota(jnp.int32, (MXU_DIM, MXU_DIM), 0)
                    >= lax.broadcasted_iota(jnp.int32, (MXU_DIM, MXU_DIM), 1), 1.0, 0.0).astype(BF16)
    prefix = _dot(tri, tie_cat)
    before = jnp.zeros((1, tq), F32)
    for j in range(n_blk):
        rows = slice(MXU_DIM * j, MXU_DIM * (j + 1))
        rank = prefix[:, tq * j:tq * (j + 1)] + before
        before = before + prefix[MXU_DIM - 1:MXU_DIM, tq * j:tq * (j + 1)]
        sj = s[rows]
        admitted = jnp.where(sj == thr, jnp.where(rank <= need, 0.0, neg_inf), neg_inf)
        bias = jnp.maximum(jnp.where(sj > thr, 0.0, admitted), open_row)
        if j == n_blk - 1:
            bias = jnp.where(causal_t, bias, neg_inf)
        bias_ref[rows, :] = bias

    q_all = _dsa_queries(qa_ref)
    qsq = q_all.astype(F32)
    qn2 = _dot_nt(jnp.ones((8, LANES), BF16), (qsq * qsq).astype(BF16))[0:1, :]
    shift = jnp.sqrt(qn2) * (_lane_tile(kmax_ref[...], A_HEADS * tq) * SHIFT_MARGIN)
    bias = bias_ref[0:klen, :]
    st = _dot_nt(ks_ref[0, 0:klen, 0:128], q_all) + jnp.concatenate([bias] * A_HEADS, axis=1) - shift
    e = jnp.exp2(st)
    l = _col_reduce(e, jnp.sum)
    _dsa_write(_dot(vt_ref[0:HEAD_DIM, 0:klen], e.astype(BF16)), l, o_ref, ot_ref)

    @pl.when(jnp.logical_not(jnp.min(l) >= DENOM_FLOOR))
    def _():
        if klen < seq:
            bias_ref[klen:seq, :] = jnp.full((seq - klen, tq), neg_inf, F32)
        flag_ref[0] = 1


def _dsa_body(qa_ref, qi_ref, wi_ref, ks_ref, o_ref, vt_ref, sc_ref, bias_ref, ot_ref, kmax_ref,
              flag_ref, *, topk, n_bisect):
    qblk = pl.program_id(1)
    seq = ks_ref.shape[1]
    flag_ref[0] = 0

    @pl.when(qblk == 0)
    def _():
        vt_ref[...] = ks_ref[0, :, 256:384].astype(F32).T.astype(BF16)
        ka = ks_ref[0, :, 0:LANES].astype(F32)
        lane = lax.broadcasted_iota(jnp.int32, ka.shape, 1)
        kn2 = jnp.sum(jnp.where(lane < HEAD_DIM, ka * ka, 0.0), axis=1, keepdims=True)
        kmax_ref[...] = jnp.broadcast_to(jnp.sqrt(jnp.max(kn2, axis=0, keepdims=True)), kmax_ref.shape)

    blocks_per_step = CAUSAL_STEP // DSA_QBLK
    for c in range(seq // CAUSAL_STEP):
        @pl.when(qblk // blocks_per_step == c)
        def _(c=c):
            _dsa_keys(CAUSAL_STEP * (c + 1), qblk, qa_ref, qi_ref, wi_ref, ks_ref, o_ref, vt_ref,
                      sc_ref, bias_ref, ot_ref, kmax_ref, flag_ref, topk, n_bisect)

    @pl.when(flag_ref[0] != 0)
    def _():
        bias = bias_ref[...]
        st = _dot_nt(ks_ref[0, :, 0:128], _dsa_queries(qa_ref)) + jnp.concatenate([bias] * A_HEADS, axis=1)
        e = jnp.exp2(st - _col_reduce(st, jnp.max))
        _dsa_write(_dot(vt_ref[0:HEAD_DIM, :], e.astype(BF16)), _col_reduce(e, jnp.sum), o_ref, ot_ref)


def _dsa(qa, qi, wi, ks, topk, n_bisect):
    b, s, _ = qa.shape
    tq = DSA_QBLK
    blk = lambda bi, qi_: (bi, qi_, 0)
    return pl.pallas_call(
        functools.partial(_dsa_body, topk=topk, n_bisect=n_bisect),
        grid=(b, s // tq),
        in_specs=[
            pl.BlockSpec((1, tq, 512), blk),
            pl.BlockSpec((1, tq, 256), blk),
            pl.BlockSpec((1, tq, LANES), blk),
            pl.BlockSpec((1, s, 384), lambda bi, qi_: (bi, 0, 0)),
        ],
        out_specs=pl.BlockSpec((1, tq, 512), blk),
        out_shape=jax.ShapeDtypeStruct((b, s, 512), BF16),
        scratch_shapes=[
            pltpu.VMEM((LANES, s), BF16),
            pltpu.VMEM((s, tq), F32),
            pltpu.VMEM((s, tq), F32),
            pltpu.VMEM((A_HEADS * HEAD_DIM, tq), F32),
            pltpu.VMEM((1, LANES), F32),
            pltpu.SMEM((1,), jnp.int32),
        ],
        compiler_params=pltpu.CompilerParams(
            dimension_semantics=("arbitrary", "arbitrary"), vmem_limit_bytes=VMEM_LIMIT),
        name="dsa_attention",
    )(qa, qi, wi, ks)


def _diff_body(q_ref, k_ref, v_ref, lq1_ref, lk1_ref, lq2_ref, lk2_ref, o_ref, vt_ref, kmax_ref,
               flag_ref, *, lambda_init):
    qblk = pl.program_id(2)
    seq = k_ref.shape[1]
    tq = q_ref.shape[1]
    flag_ref[0] = 0

    @pl.when(qblk == 0)
    def _():
        vt_ref[...] = v_ref[0].astype(F32).T.astype(BF16)
        kf = k_ref[0].astype(F32)
        ksq = kf * kf
        lane_k = lax.broadcasted_iota(jnp.int32, ksq.shape, 1)
        for c in range(2):
            part = jnp.where((lane_k >= HEAD_DIM) if c else (lane_k < HEAD_DIM), ksq, 0.0)
            kn2 = jnp.max(jnp.sum(part, axis=1, keepdims=True), axis=0, keepdims=True)
            kmax_ref[c:c + 1, :] = jnp.broadcast_to(jnp.sqrt(kn2), (1, LANES))

    lam = (jnp.exp(jnp.sum(lq1_ref[...] * lk1_ref[...], axis=1, keepdims=True))
           - jnp.exp(jnp.sum(lq2_ref[...] * lk2_ref[...], axis=1, keepdims=True)) + lambda_init)

    q = q_ref[0]
    lane = lax.broadcasted_iota(jnp.int32, q.shape, 1)
    zero = jnp.zeros_like(q)
    q2 = jnp.concatenate([jnp.where(lane < HEAD_DIM, q, zero), jnp.where(lane >= HEAD_DIM, q, zero)],
                         axis=0)
    diag = (lax.broadcasted_iota(jnp.int32, (tq, 2 * tq), 0)
            <= (lax.broadcasted_iota(jnp.int32, (tq, 2 * tq), 1) & (tq - 1)))

    def write(acc, l):
        acc = acc * (1.0 / l)
        ot = acc[:, 0:tq] - lam * acc[:, tq:2 * tq]
        ot = ot * lax.rsqrt(jnp.mean(ot * ot, axis=0, keepdims=True) + EPS) * (1.0 - lambda_init)
        o_ref[0] = ot.T.astype(BF16)

    qsq = q2.astype(F32)
    qn2 = _dot_nt(jnp.ones((8, LANES), BF16), (qsq * qsq).astype(BF16))[0:1, :]
    kmax = jnp.concatenate([_lane_tile(kmax_ref[0:1, :], tq), _lane_tile(kmax_ref[1:2, :], tq)], axis=1)
    shift = jnp.sqrt(qn2) * (kmax * SHIFT_MARGIN)

    def attend(klen):
        tail = klen - tq
        st = _dot_nt(k_ref[0, 0:klen, :], q2) - shift
        e = jnp.exp2(jnp.where(diag, st[tail:klen], -jnp.inf))
        l = _col_reduce(e, jnp.sum)
        acc = _dot(vt_ref[:, tail:klen], e.astype(BF16))
        if tail:
            e = jnp.exp2(st[0:tail])
            l = l + _col_reduce(e, jnp.sum)
            acc = acc + _dot(vt_ref[:, 0:tail], e.astype(BF16))
        write(acc, l)

        @pl.when(jnp.logical_not(jnp.min(l) >= DENOM_FLOOR))
        def _():
            flag_ref[0] = 1

    for c in range(seq // tq):
        @pl.when(qblk == c)
        def _(c=c):
            attend(tq * (c + 1))

    @pl.when(flag_ref[0] != 0)
    def _():
        kpos = lax.broadcasted_iota(jnp.int32, (seq, 2 * tq), 0)
        qpos = qblk * tq + (lax.broadcasted_iota(jnp.int32, (seq, 2 * tq), 1) & (tq - 1))
        st = jnp.where(kpos <= qpos, _dot_nt(k_ref[0], q2), -jnp.inf)
        e = jnp.exp2(st - _col_reduce(st, jnp.max))
        write(_dot(vt_ref[...], e.astype(BF16)), _col_reduce(e, jnp.sum))


def _diff(qb, kb, vb, lq1, lk1, lq2, lk2, lambda_init):
    b, s, _ = qb.shape
    tq = min(DIFF_QBLK, s)
    vec = pl.BlockSpec((1, HEAD_DIM), lambda bi, h, qi_: (0, 0))
    return pl.pallas_call(
        functools.partial(_diff_body, lambda_init=lambda_init),
        grid=(b, B_HEADS, s // tq),
        in_specs=[
            pl.BlockSpec((1, tq, LANES), lambda bi, h, qi_: (bi, qi_, h)),
            pl.BlockSpec((1, s, LANES), lambda bi, h, qi_: (bi, 0, h)),
            pl.BlockSpec((1, s, LANES), lambda bi, h, qi_: (bi, 0, h)),
            vec, vec, vec, vec,
        ],
        out_specs=pl.BlockSpec((1, tq, LANES), lambda bi, h, qi_: (bi, qi_, h)),
        out_shape=jax.ShapeDtypeStruct((b, s, B_HEADS * LANES), BF16),
        scratch_shapes=[pltpu.VMEM((LANES, s), BF16),
                        pltpu.VMEM((8, LANES), F32),
                        pltpu.SMEM((1,), jnp.int32)],
        compiler_params=pltpu.CompilerParams(
            dimension_semantics=("arbitrary", "arbitrary", "arbitrary"), vmem_limit_bytes=VMEM_LIMIT),
        name="diff_attention",
    )(qb, kb, vb, lq1, lk1, lq2, lk2)


def _memkv_body(mem_ref, g_ref, wk_ref, wv_ref, gk_ref, k_ref, v_ref):
    mem = mem_ref[0]
    memn = (mem * _rms_scale(mem) * g_ref[...]).astype(BF16)
    k = _dot(memn, wk_ref[...])
    hd = gk_ref.shape[1]
    for h in range(k.shape[1] // hd):
        kh = k[:, hd * h:hd * (h + 1)]
        k_ref[0, :, hd * h:hd * (h + 1)] = (kh * _rms_scale(kh) * gk_ref[...]).astype(BF16)
    v_ref[0] = _dot(memn, wv_ref[...]).astype(BF16)


def _memkv(mem, g_mem, w_xk, w_xv, g_xk):
    b, m, d = mem.shape
    const = lambda bi: (0, 0)
    blk = pl.BlockSpec((1, m, d), lambda bi: (bi, 0, 0))
    return pl.pallas_call(
        _memkv_body,
        grid=(b,),
        in_specs=[blk, pl.BlockSpec((1, d), const), pl.BlockSpec((d, d), const),
                  pl.BlockSpec((d, d), const), pl.BlockSpec(g_xk.shape, const)],
        out_specs=[blk, blk],
        out_shape=[jax.ShapeDtypeStruct((b, m, d), BF16)] * 2,
        compiler_params=pltpu.CompilerParams(
            dimension_semantics=("arbitrary",), vmem_limit_bytes=VMEM_LIMIT),
        name="mem_kv",
    )(mem, g_mem, w_xk, w_xv, g_xk)


def _xattn_body(x_ref, oa_ref, ob_ref, wo_ref, g_ref, wq_ref, gq_ref, k_ref, v_ref, wxo_ref, h_ref):
    half = oa_ref.shape[2]
    h1 = x_ref[0] + _dot(oa_ref[0], wo_ref[0:half, :]) + _dot(ob_ref[0], wo_ref[half:2 * half, :])
    hn = (h1 * _rms_scale(h1) * g_ref[...]).astype(BF16)
    q = _dot(hn, wq_ref[...])
    hd = gq_ref.shape[1]
    outs = []
    for h in range(q.shape[1] // hd):
        sl = slice(hd * h, hd * (h + 1))
        qh = q[:, sl]
        qh = (qh * _rms_scale(qh) * gq_ref[...] * (hd ** -0.5)).astype(BF16)
        s = _dot_nt(qh, k_ref[0, :, sl])
        e = jnp.exp(s - jnp.max(s, axis=-1, keepdims=True))
        p = (e * (1.0 / jnp.sum(e, axis=-1, keepdims=True))).astype(BF16)
        outs.append(_dot(p, v_ref[0, :, sl]).astype(BF16))
    o = jnp.concatenate(outs, axis=1)
    h_ref[0] = h1 + _dot(o, wxo_ref[...])


def _xattn(x, oa, ob, w_out, g_x, w_xq, g_xq, kmem, vmem, w_xo, tm):
    b, s, d = x.shape
    m = kmem.shape[1]
    const = lambda bi, ti: (0, 0)
    tok = lambda w: pl.BlockSpec((1, tm, w), lambda bi, ti: (bi, ti, 0))
    memblk = pl.BlockSpec((1, m, d), lambda bi, ti: (bi, 0, 0))
    return pl.pallas_call(
        _xattn_body,
        grid=(b, s // tm),
        in_specs=[tok(d), tok(oa.shape[2]), tok(ob.shape[2]),
                  pl.BlockSpec(w_out.shape, const), pl.BlockSpec((1, d), const),
                  pl.BlockSpec(w_xq.shape, const), pl.BlockSpec(g_xq.shape, const),
                  memblk, memblk, pl.BlockSpec(w_xo.shape, const)],
        out_specs=tok(d),
        out_shape=jax.ShapeDtypeStruct((b, s, d), F32),
        compiler_params=pltpu.CompilerParams(
            dimension_semantics=("arbitrary", "arbitrary"), vmem_limit_bytes=VMEM_LIMIT),
        name="outproj_xattn",
    )(x, oa, ob, w_out, g_x, w_xq, g_xq, kmem, vmem, w_xo)


HALO = 8


def _ffn_body(h_ref, g_ref, wa_ref, wg_ref, cw_ref, cb_ref, wo_ref, o_ref, a_ref):
    tm = h_ref.shape[1]

    @pl.when(pl.program_id(1) == 0)
    def _():
        a_ref[0:HALO, :] = jnp.zeros((HALO, a_ref.shape[1]), F32)

    h = h_ref[0]
    hn = (h * _rms_scale(h) * g_ref[...]).astype(BF16)
    a_ref[HALO:HALO + tm, :] = _dot(hn, wa_ref[...])
    gate = _dot(hn, wg_ref[...])
    conv = cb_ref[...]
    for j in range(CONV_W):
        off = HALO - (CONV_W - 1) + j
        conv = conv + a_ref[off:off + tm, :] * cw_ref[j:j + 1, :]
    a_ref[0:HALO, :] = a_ref[tm:tm + HALO, :]
    u = (jax.nn.gelu(conv) * gate).astype(BF16)
    o_ref[0] = h + _dot(u, wo_ref[...])


def _ffn(h, g_ffn, w_a, w_g, conv_w, conv_b, w_o, tm):
    b, s, d = h.shape
    dff = w_a.shape[1]
    const = lambda bi, ti: (0, 0)
    tok = pl.BlockSpec((1, tm, d), lambda bi, ti: (bi, ti, 0))
    return pl.pallas_call(
        _ffn_body,
        grid=(b, s // tm),
        in_specs=[tok, pl.BlockSpec((1, d), const), _const_spec(w_a.shape),
                  _const_spec(w_g.shape), pl.BlockSpec(conv_w.shape, const),
                  pl.BlockSpec((1, dff), const), _const_spec(w_o.shape)],
        out_specs=tok,
        out_shape=jax.ShapeDtypeStruct((b, s, d), F32),
        scratch_shapes=[pltpu.VMEM((tm + HALO, dff), F32)],
        compiler_params=pltpu.CompilerParams(
            dimension_semantics=("arbitrary", "arbitrary"), vmem_limit_bytes=VMEM_LIMIT),
        name="conv_glu",
    )(h, g_ffn, w_a, w_g, conv_w, conv_b, w_o)


def _rearranged_w_in(w_in):
    sizes = (A_HEADS * HEAD_DIM, HEAD_DIM, HEAD_DIM, IDX_HEADS * HEAD_DIM, HEAD_DIM, IDX_HEADS,
             2 * B_HEADS * HEAD_DIM, 2 * B_HEADS * HEAD_DIM, B_HEADS * 2 * HEAD_DIM)
    offs = [0]
    for sz in sizes:
        offs.append(offs[-1] + sz)
    q_a, k_a, v_a, q_i, k_i, w_i, q_b, k_b, v_b = [w_in[:, offs[i]:offs[i + 1]] for i in range(9)]
    pad = jnp.zeros((w_in.shape[0], HEAD_DIM - IDX_HEADS), w_in.dtype)
    w_all = jnp.concatenate([q_a, q_i, k_a, k_a, k_i, k_i, v_a, w_i, pad, q_b, k_b, v_b], axis=1)
    assert w_all.shape[1] == _C_END
    return w_all.astype(BF16)


def kernel(x, mem, positions, g_mix, w_in, g_qa, g_ka, g_qb, g_kb, lam_q1, lam_k1, lam_q2, lam_k2,
           w_out, g_xattn, g_mem, w_xq, w_xk, w_xv, w_xo, g_xq, g_xk, g_ffn, w_ffn_in, conv_w, conv_b,
           w_ffn_out):
    b, s, d = x.shape
    depth = g_mix.shape[0]
    topk = min(TOPK_MAX, s // 4)
    tm = min(512, s)

    inv_freq = 1.0 / (ROPE_THETA ** (jnp.arange(0, HEAD_DIM, 2, dtype=F32) / HEAD_DIM))
    ang = (positions.reshape(b * s // 4, 4, 1).astype(F32) * inv_freq).reshape(b * s // 4, LANES)
    cos = jnp.cos(ang).reshape(b * s, HEAD_DIM // 2)
    sin = jnp.sin(ang).reshape(b * s, HEAD_DIM // 2)
    cos128 = jnp.concatenate([cos] * 4, axis=-1)
    sin128 = jnp.concatenate([-sin, sin, -sin, sin], axis=-1)
    blk = jnp.arange(MXU_DIM) // HEAD_DIM
    bd = jnp.where(blk[:, None] == blk[None, :], 1.0 / HEAD_DIM, 0.0).astype(BF16)

    h = x
    for l in range(depth):
        lambda_init = 0.8 - 0.6 * math.exp(-0.3 * l)
        gains = jnp.stack([jnp.tile(g, 512 // HEAD_DIM) for g in (g_qa[l], g_ka[l], g_qb[l], g_kb[l])])
        gains = jnp.concatenate([gains, jnp.ones((4, 512), F32)], axis=0)
        qa, qi, ks, wi, qb, kb, vb = _inproj(
            h.reshape(b * s, d), g_mix[l][None, :], _rearranged_w_in(w_in[l]), cos128, sin128, bd, gains, tm)
        r3 = lambda t: t.reshape(b, s, t.shape[-1])
        out_a = _dsa(r3(qa), r3(qi), r3(wi), r3(ks), topk, n_bisect=20)
        out_b = _diff(r3(qb), r3(kb), r3(vb), lam_q1[l][None, :], lam_k1[l][None, :],
                      lam_q2[l][None, :], lam_k2[l][None, :], lambda_init)
        kmem, vmem = _memkv(mem, g_mem[l][None, :], w_xk[l].astype(BF16), w_xv[l].astype(BF16),
                            g_xk[l][None, :])
        h = _xattn(h, out_a, out_b, w_out[l].astype(BF16), g_xattn[l][None, :], w_xq[l].astype(BF16),
                   g_xq[l][None, :], kmem, vmem, w_xo[l].astype(BF16), tm)
        dff = w_ffn_out.shape[1]
        w_ffn = w_ffn_in[l].astype(BF16)
        cw = jnp.concatenate([conv_w[l], jnp.zeros((8 - CONV_W, dff), F32)], axis=0)
        h = _ffn(h, g_ffn[l][None, :], w_ffn[:, :dff], w_ffn[:, dff:], cw, conv_b[l][None, :],
                 w_ffn_out[l].astype(BF16), min(256, s))
    return h
```

```python
import functools
import math

import jax
import jax.numpy as jnp
from jax import lax
from jax.experimental import pallas as pl
from jax.experimental.pallas import tpu as pltpu

F32 = jnp.float32
BF16 = jnp.bfloat16

EPS = 1e-6
ROPE_THETA = 10000.0
HEAD_DIM = 64
A_HEADS = 8
IDX_HEADS = 4
TOPK_MAX = 256
B_HEADS = 4
X_HEADS = 4
CONV_W = 3
LANES = 128
MXU_DIM = 256
DSA_QBLK = 128
DIFF_QBLK = 256
CAUSAL_STEP = MXU_DIM
VMEM_LIMIT = 56 * 1024 * 1024
LOG2E = 1.4426950408889634
SHIFT_MARGIN = 1.02
DENOM_FLOOR = 2.0 ** -40

_C_QA = 0
_C_QI = 512
_C_KS = 768
_C_QB = 1152
_C_KB = 1664
_C_VB = 2176
_C_END = 2688


def _dot(a, b):
    return jnp.dot(a, b, preferred_element_type=F32)


def _dot_nt(a, b):
    return lax.dot_general(a, b, (((1,), (1,)), ((), ())), preferred_element_type=F32)


def _rms_scale(x):
    return lax.rsqrt(jnp.mean(x * x, axis=-1, keepdims=True) + EPS)


def _const_spec(shape):
    zeros = (0,) * len(shape)
    return pl.BlockSpec(shape, lambda *_: zeros, pipeline_mode=pl.Buffered(1))


def _lane_tile(t, width):
    reps = width // t.shape[1]
    return t if reps == 1 else jnp.concatenate([t] * reps, axis=1)


def _col_reduce(x, reduce_fn):
    rows, cols = x.shape
    slab = 8 * max(1, 8 * LANES // cols)
    if rows % slab or rows == slab:
        return reduce_fn(x, axis=0, keepdims=True)
    part = reduce_fn(x.reshape(rows // slab, slab, cols), axis=0)
    return reduce_fn(part, axis=0, keepdims=True)


def _inproj_body(x_ref, gmix_ref, w_ref, cos_ref, sin_ref, bd_ref, gains_ref,
                 qa_ref, qi_ref, ks_ref, wi_ref, qb_ref, kb_ref, vb_ref):
    x = x_ref[...]
    hn = (x * _rms_scale(x) * gmix_ref[...]).astype(BF16)
    cos = cos_ref[...]
    sin = sin_ref[...]
    bd = bd_ref[...]

    def proj(c0, width):
        return _dot(hn, w_ref[:, c0:c0 + width])

    def group_rms_scale(p):
        sq = (p * p).astype(BF16)
        outs = []
        for j in range(p.shape[1] // MXU_DIM):
            outs.append(_dot(sq[:, MXU_DIM * j:MXU_DIM * (j + 1)], bd))
        ms = outs[0] if len(outs) == 1 else jnp.concatenate(outs, axis=1)
        return lax.rsqrt(ms + EPS)

    def rope(y):
        width = y.shape[1]
        lane = lax.broadcasted_iota(jnp.int32, y.shape, 1)
        first_half = (lane & (HEAD_DIM - 1)) < (HEAD_DIM // 2)
        swapped = jnp.where(first_half, pltpu.roll(y, width - HEAD_DIM // 2, 1),
                            pltpu.roll(y, HEAD_DIM // 2, 1))
        return y * _lane_tile(cos, width) + swapped * _lane_tile(sin, width)

    sm_scale = HEAD_DIM ** -0.5 * LOG2E

    p = proj(_C_QA, 512)
    qa_ref[...] = (rope(p * group_rms_scale(p) * gains_ref[0:1, :]) * sm_scale).astype(BF16)

    p = proj(_C_QI, 256)
    qi_ref[...] = rope(p).astype(BF16)

    p = proj(_C_KS, 384)
    p01 = p[:, 0:256]
    lane = lax.broadcasted_iota(jnp.int32, p01.shape, 1)
    y01 = jnp.where(lane < LANES, p01 * group_rms_scale(p01) * gains_ref[1:2, 0:256], p01)
    y01 = rope(y01)
    p2 = p[:, 256:384]
    lane = lax.broadcasted_iota(jnp.int32, p2.shape, 1)
    p2 = jnp.where(lane < HEAD_DIM, p2, p2 * (IDX_HEADS ** -0.5 * HEAD_DIM ** -0.5))
    ks_ref[:, 0:256] = y01.astype(BF16)
    ks_ref[:, 256:384] = p2.astype(BF16)
    wi_ref[...] = p2

    def store_heads(ref, y):
        for h in range(B_HEADS):
            ref[0, h] = y[:, LANES * h:LANES * (h + 1)].astype(BF16)

    p = proj(_C_QB, 512)
    store_heads(qb_ref, rope(p * group_rms_scale(p) * gains_ref[2:3, :]) * sm_scale)

    p = proj(_C_KB, 512)
    store_heads(kb_ref, rope(p * group_rms_scale(p) * gains_ref[3:4, :]))

    store_heads(vb_ref, proj(_C_VB, 512))


def _inproj(x2, gmix, w_all, cos128, sin128, bd, gains, tm, seq):
    n, d = x2.shape
    row = lambda i: (i, 0)
    const = lambda i: (0, 0)
    outs = [(512, BF16), (256, BF16), (384, BF16), (LANES, F32)]
    tiles = seq // tm
    head_spec = pl.BlockSpec((1, B_HEADS, tm, LANES), lambda i: (i // tiles, 0, i % tiles, 0))
    head_shape = jax.ShapeDtypeStruct((n // seq, B_HEADS, seq, LANES), BF16)
    return pl.pallas_call(
        _inproj_body,
        grid=(n // tm,),
        in_specs=[
            pl.BlockSpec((tm, d), row),
            pl.BlockSpec((1, d), const),
            pl.BlockSpec(w_all.shape, const),
            pl.BlockSpec((tm, LANES), row),
            pl.BlockSpec((tm, LANES), row),
            pl.BlockSpec(bd.shape, const),
            pl.BlockSpec(gains.shape, const),
        ],
        out_specs=[pl.BlockSpec((tm, w), row) for w, _ in outs] + [head_spec] * 3,
        out_shape=[jax.ShapeDtypeStruct((n, w), dt) for w, dt in outs] + [head_shape] * 3,
        compiler_params=pltpu.CompilerParams(
            dimension_semantics=("arbitrary",), vmem_limit_bytes=VMEM_LIMIT),
        name="inproj",
    )(x2, gmix, w_all, cos128, sin128, bd, gains)


def _dsa_queries(qa_ref):
    qa = qa_ref[0]
    lane = lax.broadcasted_iota(jnp.int32, (DSA_QBLK, LANES), 1)
    rows = []
    for h in range(A_HEADS):
        slab = qa[:, LANES * (h // 2):LANES * (h // 2 + 1)]
        keep = (lane >= HEAD_DIM) if h % 2 else (lane < HEAD_DIM)
        rows.append(jnp.where(keep, slab, jnp.zeros_like(slab)))
    return jnp.concatenate(rows, axis=0)


def _dsa_write(ot, l, o_ref, ot_ref):
    tq = DSA_QBLK
    ot = ot * (1.0 / l)
    for h in range(A_HEADS):
        ot_ref[HEAD_DIM * h:HEAD_DIM * (h + 1), :] = ot[:, tq * h:tq * (h + 1)]
    o_ref[0] = ot_ref[...].T.astype(BF16)


def _dsa_keys(klen, qblk, qa_ref, qi_ref, wi_ref, ks_ref, o_ref, vt_ref, sc_ref, bias_ref, ot_ref,
              kmax_ref, flag_ref, topk, n_bisect):
    tq = DSA_QBLK
    seq = ks_ref.shape[1]
    tail = klen - CAUSAL_STEP
    neg_inf = -jnp.inf

    lane_q = lax.broadcasted_iota(jnp.int32, (tq, LANES), 1)
    first_head = lane_q < HEAD_DIM

    def head_rows(slab, odd):
        return jnp.where(first_head != odd, slab, jnp.zeros_like(slab))

    qi = qi_ref[0]
    qi_stack = jnp.concatenate(
        [head_rows(qi[:, LANES * (h // 2):LANES * (h // 2 + 1)], bool(h % 2)) for h in range(IDX_HEADS)],
        axis=0)
    lg = _dot_nt(ks_ref[0, 0:klen, 128:256], qi_stack)
    w_t = wi_ref[0].T
    sc = None
    for h in range(IDX_HEADS):
        term = jnp.maximum(lg[:, tq * h:tq * (h + 1)], 0.0) * w_t[HEAD_DIM + h:HEAD_DIM + h + 1, :]
        sc = term if sc is None else sc + term
    kpos_t = tail + lax.broadcasted_iota(jnp.int32, (CAUSAL_STEP, tq), 0)
    qpos_t = qblk * tq + lax.broadcasted_iota(jnp.int32, (CAUSAL_STEP, tq), 1)
    causal_t = kpos_t <= qpos_t
    if tail:
        sc_ref[0:tail, :] = sc[0:tail]
    sc_ref[tail:klen, :] = jnp.where(causal_t, sc[tail:klen], neg_inf)

    kf = float(topk)
    qpos_row = qblk * tq + lax.broadcasted_iota(jnp.int32, (1, tq), 1)
    search = qpos_row >= topk

    def scores():
        return sc_ref[0:klen, :]

    def count(pred):
        return _col_reduce(jnp.where(pred, 1.0, 0.0), jnp.sum)

    hi0 = _col_reduce(scores(), jnp.max)
    lo0 = _col_reduce(jnp.where(causal_t, sc_ref[tail:klen, :], jnp.inf), jnp.min)
    if tail:
        lo0 = jnp.minimum(lo0, _col_reduce(sc_ref[0:tail, :], jnp.min))

    def bisect(_, carry):
        lo, hi = carry
        mid = 0.5 * (lo + hi)
        ge = count(scores() >= mid) >= kf
        return jnp.where(ge, mid, lo), jnp.where(ge, hi, mid)

    lo, _ = lax.fori_loop(0, n_bisect, bisect, (lo0, hi0))

    def too_low(n_gt):
        return jnp.max(jnp.where(jnp.logical_and(search, n_gt >= kf), 1.0, 0.0))

    def climb(carry):
        thr, n_gt, _ = carry
        s = scores()
        nxt = _col_reduce(jnp.where(s > thr, s, jnp.inf), jnp.min)
        thr = jnp.where(jnp.logical_and(search, n_gt >= kf), nxt, thr)
        n_gt = count(scores() > thr)
        return thr, n_gt, too_low(n_gt)

    s = scores()
    thr0 = _col_reduce(jnp.where(s >= lo, s, jnp.inf), jnp.min)
    n_gt0 = count(s > thr0)
    thr, n_gt, _ = lax.while_loop(lambda c: c[2] > 0.0, climb, (thr0, n_gt0, too_low(n_gt0)))

    need = kf - n_gt
    open_row = jnp.where(search, neg_inf, 0.0)
    n_blk = klen // MXU_DIM
    s = scores()
    tie = jnp.where(s == thr, 1.0, 0.0).astype(BF16)
    tie_cat = jnp.concatenate([tie[MXU_DIM * j:MXU_DIM * (j + 1)] for j in range(n_blk)], axis=1)
    tri = jnp.where(lax.broadcasted_iota(jnp.int32, (MXU_DIM, MXU_DIM), 0)
                    >= lax.broadcasted_iota(jnp.int32, (MXU_DIM, MXU_DIM), 1), 1.0, 0.0).astype(BF16)
    prefix = _dot(tri, tie_cat)
    before = jnp.zeros((1, tq), F32)
    for j in range(n_blk):
        rows = slice(MXU_DIM * j, MXU_DIM * (j + 1))
        rank = prefix[:, tq * j:tq * (j + 1)] + before
        before = before + prefix[MXU_DIM - 1:MXU_DIM, tq * j:tq * (j + 1)]
        sj = s[rows]
        admitted = jnp.where(sj == thr, jnp.where(rank <= need, 0.0, neg_inf), neg_inf)
        bias = jnp.maximum(jnp.where(sj > thr, 0.0, admitted), open_row)
        if j == n_blk - 1:
            bias = jnp.where(causal_t, bias, neg_inf)
        bias_ref[rows, :] = bias

    q_all = _dsa_queries(qa_ref)
    qsq = q_all.astype(F32)
    qn2 = _dot_nt(jnp.ones((8, LANES), BF16), (qsq * qsq).astype(BF16))[0:1, :]
    shift = jnp.sqrt(qn2) * (_lane_tile(kmax_ref[...], A_HEADS * tq) * SHIFT_MARGIN)
    bias = bias_ref[0:klen, :]
    st = _dot_nt(ks_ref[0, 0:klen, 0:128], q_all) + jnp.concatenate([bias] * A_HEADS, axis=1) - shift
    e = jnp.exp2(st)
    l = _col_reduce(e, jnp.sum)
    _dsa_write(_dot(vt_ref[0:HEAD_DIM, 0:klen], e.astype(BF16)), l, o_ref, ot_ref)

    @pl.when(jnp.logical_not(jnp.min(l) >= DENOM_FLOOR))
    def _():
        if klen < seq:
            bias_ref[klen:seq, :] = jnp.full((seq - klen, tq), neg_inf, F32)
        flag_ref[0] = 1


def _dsa_body(qa_ref, qi_ref, wi_ref, ks_ref, o_ref, vt_ref, sc_ref, bias_ref, ot_ref, kmax_ref,
              flag_ref, *, topk, n_bisect):
    qblk = pl.program_id(1)
    seq = ks_ref.shape[1]
    flag_ref[0] = 0

    @pl.when(qblk == 0)
    def _():
        vt_ref[...] = ks_ref[0, :, 256:384].astype(F32).T.astype(BF16)
        ka = ks_ref[0, :, 0:LANES].astype(F32)
        lane = lax.broadcasted_iota(jnp.int32, ka.shape, 1)
        kn2 = jnp.sum(jnp.where(lane < HEAD_DIM, ka * ka, 0.0), axis=1, keepdims=True)
        kmax_ref[...] = jnp.broadcast_to(jnp.sqrt(jnp.max(kn2, axis=0, keepdims=True)), kmax_ref.shape)

    blocks_per_step = CAUSAL_STEP // DSA_QBLK
    for c in range(seq // CAUSAL_STEP):
        @pl.when(qblk // blocks_per_step == c)
        def _(c=c):
            _dsa_keys(CAUSAL_STEP * (c + 1), qblk, qa_ref, qi_ref, wi_ref, ks_ref, o_ref, vt_ref,
                      sc_ref, bias_ref, ot_ref, kmax_ref, flag_ref, topk, n_bisect)

    @pl.when(flag_ref[0] != 0)
    def _():
        bias = bias_ref[...]
        st = _dot_nt(ks_ref[0, :, 0:128], _dsa_queries(qa_ref)) + jnp.concatenate([bias] * A_HEADS, axis=1)
        e = jnp.exp2(st - _col_reduce(st, jnp.max))
        _dsa_write(_dot(vt_ref[0:HEAD_DIM, :], e.astype(BF16)), _col_reduce(e, jnp.sum), o_ref, ot_ref)


def _dsa(qa, qi, wi, ks, topk, n_bisect):
    b, s, _ = qa.shape
    tq = DSA_QBLK
    blk = lambda bi, qi_: (bi, qi_, 0)
    return pl.pallas_call(
        functools.partial(_dsa_body, topk=topk, n_bisect=n_bisect),
        grid=(b, s // tq),
        in_specs=[
            pl.BlockSpec((1, tq, 512), blk),
            pl.BlockSpec((1, tq, 256), blk),
            pl.BlockSpec((1, tq, LANES), blk),
            pl.BlockSpec((1, s, 384), lambda bi, qi_: (bi, 0, 0)),
        ],
        out_specs=pl.BlockSpec((1, tq, 512), blk),
        out_shape=jax.ShapeDtypeStruct((b, s, 512), BF16),
        scratch_shapes=[
            pltpu.VMEM((LANES, s), BF16),
            pltpu.VMEM((s, tq), F32),
            pltpu.VMEM((s, tq), F32),
            pltpu.VMEM((A_HEADS * HEAD_DIM, tq), F32),
            pltpu.VMEM((1, LANES), F32),
            pltpu.SMEM((1,), jnp.int32),
        ],
        compiler_params=pltpu.CompilerParams(
            dimension_semantics=("arbitrary", "arbitrary"), vmem_limit_bytes=VMEM_LIMIT),
        name="dsa_attention",
    )(qa, qi, wi, ks)


def _diff_body(q_ref, k_ref, v_ref, lq1_ref, lk1_ref, lq2_ref, lk2_ref, o_ref, vt_ref, kmax_ref,
               flag_ref, *, lambda_init):
    qblk = pl.program_id(1)
    seq = k_ref.shape[2]
    tq = q_ref.shape[2]

    @pl.when(qblk == 0)
    def _():
        lane_k = lax.broadcasted_iota(jnp.int32, (seq, LANES), 1)
        for h in range(B_HEADS):
            vt_ref[h] = v_ref[0, h].astype(F32).T.astype(BF16)
            kf = k_ref[0, h].astype(F32)
            ksq = kf * kf
            for c in range(2):
                part = jnp.where((lane_k >= HEAD_DIM) if c else (lane_k < HEAD_DIM), ksq, 0.0)
                kn2 = jnp.max(jnp.sum(part, axis=1, keepdims=True), axis=0, keepdims=True)
                kmax_ref[h, c:c + 1, :] = jnp.broadcast_to(jnp.sqrt(kn2), (1, LANES))

    lam = (jnp.exp(jnp.sum(lq1_ref[...] * lk1_ref[...], axis=1, keepdims=True))
           - jnp.exp(jnp.sum(lq2_ref[...] * lk2_ref[...], axis=1, keepdims=True)) + lambda_init)
    lane = lax.broadcasted_iota(jnp.int32, (tq, LANES), 1)
    diag = (lax.broadcasted_iota(jnp.int32, (tq, 2 * tq), 0)
            <= (lax.broadcasted_iota(jnp.int32, (tq, 2 * tq), 1) & (tq - 1)))

    def queries(h):
        q = q_ref[0, h]
        zero = jnp.zeros_like(q)
        return jnp.concatenate([jnp.where(lane < HEAD_DIM, q, zero), jnp.where(lane >= HEAD_DIM, q, zero)],
                               axis=0)

    def write(h, acc, l):
        acc = acc * (1.0 / l)
        ot = acc[:, 0:tq] - lam * acc[:, tq:2 * tq]
        ot = ot * lax.rsqrt(jnp.mean(ot * ot, axis=0, keepdims=True) + EPS) * (1.0 - lambda_init)
        o_ref[0, h] = ot.T.astype(BF16)

    def attend(klen, h, underflow):
        tail = klen - tq
        q2 = queries(h)
        qsq = q2.astype(F32)
        qn2 = _dot_nt(jnp.ones((8, LANES), BF16), (qsq * qsq).astype(BF16))[0:1, :]
        kmax = jnp.concatenate([_lane_tile(kmax_ref[h, 0:1, :], tq), _lane_tile(kmax_ref[h, 1:2, :], tq)],
                               axis=1)
        shift = jnp.sqrt(qn2) * (kmax * SHIFT_MARGIN)
        st = _dot_nt(k_ref[0, h, 0:klen, :], q2) - shift
        e = jnp.exp2(jnp.where(diag, st[tail:klen], -jnp.inf))
        l = _col_reduce(e, jnp.sum)
        acc = _dot(vt_ref[h, :, tail:klen], e.astype(BF16))
        if tail:
            e = jnp.exp2(st[0:tail])
            l = l + _col_reduce(e, jnp.sum)
            acc = acc + _dot(vt_ref[h, :, 0:tail], e.astype(BF16))
        write(h, acc, l)
        return jnp.maximum(underflow, jnp.where(jnp.min(l) >= DENOM_FLOOR, 0, 1))

    flag_ref[0] = 0
    for c in range(seq // tq):
        @pl.when(qblk == c)
        def _(c=c):
            flag_ref[0] = lax.fori_loop(0, B_HEADS, functools.partial(attend, tq * (c + 1)), 0)

    @pl.when(flag_ref[0] != 0)
    def _():
        kpos = lax.broadcasted_iota(jnp.int32, (seq, 2 * tq), 0)
        qpos = qblk * tq + (lax.broadcasted_iota(jnp.int32, (seq, 2 * tq), 1) & (tq - 1))

        def redo(h, carry):
            st = jnp.where(kpos <= qpos, _dot_nt(k_ref[0, h], queries(h)), -jnp.inf)
            e = jnp.exp2(st - _col_reduce(st, jnp.max))
            write(h, _dot(vt_ref[h], e.astype(BF16)), _col_reduce(e, jnp.sum))
            return carry

        lax.fori_loop(0, B_HEADS, redo, 0)


def _diff(qb, kb, vb, lq1, lk1, lq2, lk2, lambda_init):
    b, _, s, _ = qb.shape
    tq = min(DIFF_QBLK, s)
    vec = pl.BlockSpec((1, HEAD_DIM), lambda bi, qi_: (0, 0))
    keys = pl.BlockSpec((1, B_HEADS, s, LANES), lambda bi, qi_: (bi, 0, 0, 0))
    blk = pl.BlockSpec((1, B_HEADS, tq, LANES), lambda bi, qi_: (bi, 0, qi_, 0))
    return pl.pallas_call(
        functools.partial(_diff_body, lambda_init=lambda_init),
        grid=(b, s // tq),
        in_specs=[blk, keys, keys, vec, vec, vec, vec],
        out_specs=blk,
        out_shape=jax.ShapeDtypeStruct((b, B_HEADS, s, LANES), BF16),
        scratch_shapes=[pltpu.VMEM((B_HEADS, LANES, s), BF16),
                        pltpu.VMEM((B_HEADS, 8, LANES), F32),
                        pltpu.SMEM((1,), jnp.int32)],
        compiler_params=pltpu.CompilerParams(
            dimension_semantics=("arbitrary", "arbitrary"), vmem_limit_bytes=VMEM_LIMIT),
        name="diff_attention",
    )(qb, kb, vb, lq1, lk1, lq2, lk2)


def _memkv_body(mem_ref, g_ref, wk_ref, wv_ref, gk_ref, k_ref, v_ref):
    mem = mem_ref[0]
    memn = (mem * _rms_scale(mem) * g_ref[...]).astype(BF16)
    k = _dot(memn, wk_ref[...])
    hd = gk_ref.shape[1]
    for h in range(k.shape[1] // hd):
        kh = k[:, hd * h:hd * (h + 1)]
        k_ref[0, :, hd * h:hd * (h + 1)] = (kh * _rms_scale(kh) * gk_ref[...]).astype(BF16)
    v_ref[0] = _dot(memn, wv_ref[...]).astype(BF16)


def _memkv(mem, g_mem, w_xk, w_xv, g_xk):
    b, m, d = mem.shape
    const = lambda bi: (0, 0)
    blk = pl.BlockSpec((1, m, d), lambda bi: (bi, 0, 0))
    return pl.pallas_call(
        _memkv_body,
        grid=(b,),
        in_specs=[blk, pl.BlockSpec((1, d), const), pl.BlockSpec((d, d), const),
                  pl.BlockSpec((d, d), const), pl.BlockSpec(g_xk.shape, const)],
        out_specs=[blk, blk],
        out_shape=[jax.ShapeDtypeStruct((b, m, d), BF16)] * 2,
        compiler_params=pltpu.CompilerParams(
            dimension_semantics=("arbitrary",), vmem_limit_bytes=VMEM_LIMIT),
        name="mem_kv",
    )(mem, g_mem, w_xk, w_xv, g_xk)


def _xattn_body(x_ref, oa_ref, ob_ref, wo_ref, g_ref, wq_ref, gq_ref, k_ref, v_ref, wxo_ref, h_ref):
    half = oa_ref.shape[2]
    ob = jnp.concatenate([ob_ref[0, h] for h in range(B_HEADS)], axis=1)
    h1 = x_ref[0] + _dot(oa_ref[0], wo_ref[0:half, :]) + _dot(ob, wo_ref[half:2 * half, :])
    hn = (h1 * _rms_scale(h1) * g_ref[...]).astype(BF16)
    q = _dot(hn, wq_ref[...])
    hd = gq_ref.shape[1]
    outs = []
    for h in range(q.shape[1] // hd):
        sl = slice(hd * h, hd * (h + 1))
        qh = q[:, sl]
        qh = (qh * _rms_scale(qh) * gq_ref[...] * (hd ** -0.5)).astype(BF16)
        s = _dot_nt(qh, k_ref[0, :, sl])
        e = jnp.exp(s - jnp.max(s, axis=-1, keepdims=True))
        p = (e * (1.0 / jnp.sum(e, axis=-1, keepdims=True))).astype(BF16)
        outs.append(_dot(p, v_ref[0, :, sl]).astype(BF16))
    o = jnp.concatenate(outs, axis=1)
    h_ref[0] = h1 + _dot(o, wxo_ref[...])


def _xattn(x, oa, ob, w_out, g_x, w_xq, g_xq, kmem, vmem, w_xo, tm):
    b, s, d = x.shape
    m = kmem.shape[1]
    const = lambda bi, ti: (0, 0)
    tok = lambda w: pl.BlockSpec((1, tm, w), lambda bi, ti: (bi, ti, 0))
    memblk = pl.BlockSpec((1, m, d), lambda bi, ti: (bi, 0, 0))
    return pl.pallas_call(
        _xattn_body,
        grid=(b, s // tm),
        in_specs=[tok(d), tok(oa.shape[2]),
                  pl.BlockSpec((1, B_HEADS, tm, LANES), lambda bi, ti: (bi, 0, ti, 0)),
                  pl.BlockSpec(w_out.shape, const), pl.BlockSpec((1, d), const),
                  pl.BlockSpec(w_xq.shape, const), pl.BlockSpec(g_xq.shape, const),
                  memblk, memblk, pl.BlockSpec(w_xo.shape, const)],
        out_specs=tok(d),
        out_shape=jax.ShapeDtypeStruct((b, s, d), F32),
        compiler_params=pltpu.CompilerParams(
            dimension_semantics=("arbitrary", "arbitrary"), vmem_limit_bytes=VMEM_LIMIT),
        name="outproj_xattn",
    )(x, oa, ob, w_out, g_x, w_xq, g_xq, kmem, vmem, w_xo)


HALO = 8


def _ffn_body(h_ref, g_ref, win_ref, cw_ref, cb_ref, wo_ref, o_ref, a_ref):
    tm = h_ref.shape[1]
    dff = wo_ref.shape[0]

    @pl.when(pl.program_id(1) == 0)
    def _():
        a_ref[0:HALO, :] = jnp.zeros((HALO, a_ref.shape[1]), F32)

    h = h_ref[0]
    hn = (h * _rms_scale(h) * g_ref[...]).astype(BF16)
    a_ref[HALO:HALO + tm, :] = _dot(hn, win_ref[:, 0:dff])
    gate = _dot(hn, win_ref[:, dff:2 * dff])
    conv = cb_ref[...]
    for j in range(CONV_W):
        off = HALO - (CONV_W - 1) + j
        conv = conv + a_ref[off:off + tm, :] * cw_ref[j:j + 1, :]
    a_ref[0:HALO, :] = a_ref[tm:tm + HALO, :]
    u = (jax.nn.gelu(conv) * gate).astype(BF16)
    o_ref[0] = h + _dot(u, wo_ref[...])


def _ffn(h, g_ffn, w_in, conv_w, conv_b, w_o, tm):
    b, s, d = h.shape
    dff = w_o.shape[0]
    const = lambda bi, ti: (0, 0)
    tok = pl.BlockSpec((1, tm, d), lambda bi, ti: (bi, ti, 0))
    return pl.pallas_call(
        _ffn_body,
        grid=(b, s // tm),
        in_specs=[tok, pl.BlockSpec((1, d), const), _const_spec(w_in.shape),
                  pl.BlockSpec(conv_w.shape, const),
                  pl.BlockSpec((1, dff), const), _const_spec(w_o.shape)],
        out_specs=tok,
        out_shape=jax.ShapeDtypeStruct((b, s, d), F32),
        scratch_shapes=[pltpu.VMEM((tm + HALO, dff), F32)],
        compiler_params=pltpu.CompilerParams(
            dimension_semantics=("arbitrary", "arbitrary"), vmem_limit_bytes=VMEM_LIMIT),
        name="conv_glu",
    )(h, g_ffn, w_in, conv_w, conv_b, w_o)


def _rearranged_w_in(w_in):
    sizes = (A_HEADS * HEAD_DIM, HEAD_DIM, HEAD_DIM, IDX_HEADS * HEAD_DIM, HEAD_DIM, IDX_HEADS,
             2 * B_HEADS * HEAD_DIM, 2 * B_HEADS * HEAD_DIM, B_HEADS * 2 * HEAD_DIM)
    offs = [0]
    for sz in sizes:
        offs.append(offs[-1] + sz)
    q_a, k_a, v_a, q_i, k_i, w_i, q_b, k_b, v_b = [w_in[:, offs[i]:offs[i + 1]] for i in range(9)]
    pad = jnp.zeros((w_in.shape[0], HEAD_DIM - IDX_HEADS), w_in.dtype)
    w_all = jnp.concatenate([q_a, q_i, k_a, k_a, k_i, k_i, v_a, w_i, pad, q_b, k_b, v_b], axis=1)
    assert w_all.shape[1] == _C_END
    return w_all.astype(BF16)


def kernel(x, mem, positions, g_mix, w_in, g_qa, g_ka, g_qb, g_kb, lam_q1, lam_k1, lam_q2, lam_k2,
           w_out, g_xattn, g_mem, w_xq, w_xk, w_xv, w_xo, g_xq, g_xk, g_ffn, w_ffn_in, conv_w, conv_b,
           w_ffn_out):
    b, s, d = x.shape
    depth = g_mix.shape[0]
    topk = min(TOPK_MAX, s // 4)
    tm = min(512, s)

    inv_freq = 1.0 / (ROPE_THETA ** (jnp.arange(0, HEAD_DIM, 2, dtype=F32) / HEAD_DIM))
    ang = positions.reshape(b * s, 1).astype(F32) * jnp.tile(inv_freq, 4)[None, :]
    sign = jnp.tile(jnp.repeat(jnp.array([-1.0, 1.0], F32), HEAD_DIM // 2), 2)
    cos128 = jnp.cos(ang)
    sin128 = jnp.sin(ang) * sign[None, :]
    blk = jnp.arange(MXU_DIM) // HEAD_DIM
    bd = jnp.where(blk[:, None] == blk[None, :], 1.0 / HEAD_DIM, 0.0).astype(BF16)

    h = x
    for l in range(depth):
        lambda_init = 0.8 - 0.6 * math.exp(-0.3 * l)
        gains = jnp.stack([jnp.tile(g, 512 // HEAD_DIM) for g in (g_qa[l], g_ka[l], g_qb[l], g_kb[l])])
        gains = jnp.concatenate([gains, jnp.ones((4, 512), F32)], axis=0)
        qa, qi, ks, wi, qb, kb, vb = _inproj(
            h.reshape(b * s, d), g_mix[l][None, :], _rearranged_w_in(w_in[l]), cos128, sin128, bd, gains,
            tm, s)
        r3 = lambda t: t.reshape(b, s, t.shape[-1])
        out_a = _dsa(r3(qa), r3(qi), r3(wi), r3(ks), topk, n_bisect=20)
        out_b = _diff(qb, kb, vb, lam_q1[l][None, :], lam_k1[l][None, :],
                      lam_q2[l][None, :], lam_k2[l][None, :], lambda_init)
        kmem, vmem = _memkv(mem, g_mem[l][None, :], w_xk[l].astype(BF16), w_xv[l].astype(BF16),
                            g_xk[l][None, :])
        h = _xattn(h, out_a, out_b, w_out[l].astype(BF16), g_xattn[l][None, :], w_xq[l].astype(BF16),
                   g_xq[l][None, :], kmem, vmem, w_xo[l].astype(BF16), tm)
        dff = w_ffn_out.shape[1]
        cw = jnp.concatenate([conv_w[l], jnp.zeros((8 - CONV_W, dff), F32)], axis=0)
        h = _ffn(h, g_ffn[l][None, :], w_ffn_in[l].astype(BF16), cw, conv_b[l][None, :],
                 w_ffn_out[l].astype(BF16), tm)
    return h
```

```python
import functools
import math

import jax
import jax.numpy as jnp
from jax import lax
from jax.experimental import pallas as pl
from jax.experimental.pallas import tpu as pltpu

F32 = jnp.float32
BF16 = jnp.bfloat16

EPS = 1e-6
ROPE_THETA = 10000.0
HEAD_DIM = 64
A_HEADS = 8
IDX_HEADS = 4
TOPK_MAX = 256
B_HEADS = 4
X_HEADS = 4
CONV_W = 3
LANES = 128
MXU_DIM = 256
DSA_QBLK = 128
DIFF_QBLK = 256
CAUSAL_STEP = MXU_DIM
VMEM_LIMIT = 56 * 1024 * 1024
LOG2E = 1.4426950408889634
SHIFT_MARGIN = 1.02
DENOM_FLOOR = 2.0 ** -40

_C_QA = 0
_C_QI = 512
_C_KS = 768
_C_QB = 1152
_C_KB = 1664
_C_VB = 2176
_C_END = 2688


def _dot(a, b):
    return jnp.dot(a, b, preferred_element_type=F32)


def _dot_nt(a, b):
    return lax.dot_general(a, b, (((1,), (1,)), ((), ())), preferred_element_type=F32)


def _rms_scale(x):
    return lax.rsqrt(jnp.mean(x * x, axis=-1, keepdims=True) + EPS)


def _const_spec(shape):
    zeros = (0,) * len(shape)
    return pl.BlockSpec(shape, lambda *_: zeros, pipeline_mode=pl.Buffered(1))


def _lane_tile(t, width):
    reps = width // t.shape[1]
    return t if reps == 1 else jnp.concatenate([t] * reps, axis=1)


def _col_reduce(x, reduce_fn):
    rows, cols = x.shape
    slab = 8 * max(1, 8 * LANES // cols)
    if rows % slab or rows == slab:
        return reduce_fn(x, axis=0, keepdims=True)
    part = reduce_fn(x.reshape(rows // slab, slab, cols), axis=0)
    return reduce_fn(part, axis=0, keepdims=True)


def _inproj_body(x_ref, gmix_ref, w_ref, cos_ref, sin_ref, bd_ref, gains_ref,
                 qa_ref, qi_ref, ks_ref, wi_ref, qb_ref, kb_ref, vb_ref):
    x = x_ref[...]
    hn = (x * _rms_scale(x) * gmix_ref[...]).astype(BF16)
    cos = cos_ref[...]
    sin = sin_ref[...]
    bd = bd_ref[...]

    def proj(c0, width):
        return _dot(hn, w_ref[:, c0:c0 + width])

    def group_rms_scale(p):
        sq = (p * p).astype(BF16)
        outs = []
        for j in range(p.shape[1] // MXU_DIM):
            outs.append(_dot(sq[:, MXU_DIM * j:MXU_DIM * (j + 1)], bd))
        ms = outs[0] if len(outs) == 1 else jnp.concatenate(outs, axis=1)
        return lax.rsqrt(ms + EPS)

    def rope(y):
        width = y.shape[1]
        lane = lax.broadcasted_iota(jnp.int32, y.shape, 1)
        first_half = (lane & (HEAD_DIM - 1)) < (HEAD_DIM // 2)
        swapped = jnp.where(first_half, pltpu.roll(y, width - HEAD_DIM // 2, 1),
                            pltpu.roll(y, HEAD_DIM // 2, 1))
        return y * _lane_tile(cos, width) + swapped * _lane_tile(sin, width)

    sm_scale = HEAD_DIM ** -0.5 * LOG2E

    p = proj(_C_QA, 512)
    qa_ref[...] = (rope(p * group_rms_scale(p) * gains_ref[0:1, :]) * sm_scale).astype(BF16)

    p = proj(_C_QI, 256)
    qi_ref[...] = rope(p).astype(BF16)

    p = proj(_C_KS, 384)
    p01 = p[:, 0:256]
    lane = lax.broadcasted_iota(jnp.int32, p01.shape, 1)
    y01 = jnp.where(lane < LANES, p01 * group_rms_scale(p01) * gains_ref[1:2, 0:256], p01)
    y01 = rope(y01)
    p2 = p[:, 256:384]
    lane = lax.broadcasted_iota(jnp.int32, p2.shape, 1)
    p2 = jnp.where(lane < HEAD_DIM, p2, p2 * (IDX_HEADS ** -0.5 * HEAD_DIM ** -0.5))
    ks_ref[:, 0:256] = y01.astype(BF16)
    ks_ref[:, 256:384] = p2.astype(BF16)
    wi_ref[...] = p2

    def store_heads(ref, y):
        for h in range(B_HEADS):
            ref[0, h] = y[:, LANES * h:LANES * (h + 1)].astype(BF16)

    p = proj(_C_QB, 512)
    store_heads(qb_ref, rope(p * group_rms_scale(p) * gains_ref[2:3, :]) * sm_scale)

    p = proj(_C_KB, 512)
    store_heads(kb_ref, rope(p * group_rms_scale(p) * gains_ref[3:4, :]))

    store_heads(vb_ref, proj(_C_VB, 512))


def _inproj(x2, gmix, w_all, cos128, sin128, bd, gains, tm, seq):
    n, d = x2.shape
    row = lambda i: (i, 0)
    const = lambda i: (0, 0)
    outs = [(512, BF16), (256, BF16), (384, BF16), (LANES, F32)]
    tiles = seq // tm
    head_spec = pl.BlockSpec((1, B_HEADS, tm, LANES), lambda i: (i // tiles, 0, i % tiles, 0))
    head_shape = jax.ShapeDtypeStruct((n // seq, B_HEADS, seq, LANES), BF16)
    return pl.pallas_call(
        _inproj_body,
        grid=(n // tm,),
        in_specs=[
            pl.BlockSpec((tm, d), row),
            pl.BlockSpec((1, d), const),
            pl.BlockSpec(w_all.shape, const),
            pl.BlockSpec((tm, LANES), row),
            pl.BlockSpec((tm, LANES), row),
            pl.BlockSpec(bd.shape, const),
            pl.BlockSpec(gains.shape, const),
        ],
        out_specs=[pl.BlockSpec((tm, w), row) for w, _ in outs] + [head_spec] * 3,
        out_shape=[jax.ShapeDtypeStruct((n, w), dt) for w, dt in outs] + [head_shape] * 3,
        compiler_params=pltpu.CompilerParams(
            dimension_semantics=("arbitrary",), vmem_limit_bytes=VMEM_LIMIT),
        name="inproj",
    )(x2, gmix, w_all, cos128, sin128, bd, gains)


def _dsa_queries(qa_ref):
    qa = qa_ref[0]
    lane = lax.broadcasted_iota(jnp.int32, (DSA_QBLK, LANES), 1)
    rows = []
    for h in range(A_HEADS):
        slab = qa[:, LANES * (h // 2):LANES * (h // 2 + 1)]
        keep = (lane >= HEAD_DIM) if h % 2 else (lane < HEAD_DIM)
        rows.append(jnp.where(keep, slab, jnp.zeros_like(slab)))
    return jnp.concatenate(rows, axis=0)


def _dsa_write(ot, l, o_ref, ot_ref):
    tq = DSA_QBLK
    ot = ot * (1.0 / l)
    for h in range(A_HEADS):
        ot_ref[HEAD_DIM * h:HEAD_DIM * (h + 1), :] = ot[:, tq * h:tq * (h + 1)]
    o_ref[0] = ot_ref[...].T.astype(BF16)


def _dsa_keys(klen, qblk, qa_ref, qi_ref, wi_ref, ks_ref, o_ref, vt_ref, sc_ref, bias_ref, ot_ref,
              kmax_ref, flag_ref, topk, n_bisect):
    tq = DSA_QBLK
    seq = ks_ref.shape[1]
    tail = klen - CAUSAL_STEP
    neg_inf = -jnp.inf

    lane_q = lax.broadcasted_iota(jnp.int32, (tq, LANES), 1)
    first_head = lane_q < HEAD_DIM

    def head_rows(slab, odd):
        return jnp.where(first_head != odd, slab, jnp.zeros_like(slab))

    qi = qi_ref[0]
    qi_stack = jnp.concatenate(
        [head_rows(qi[:, LANES * (h // 2):LANES * (h // 2 + 1)], bool(h % 2)) for h in range(IDX_HEADS)],
        axis=0)
    lg = _dot_nt(ks_ref[0, 0:klen, 128:256], qi_stack)
    w_t = wi_ref[0].T
    sc = None
    for h in range(IDX_HEADS):
        term = jnp.maximum(lg[:, tq * h:tq * (h + 1)], 0.0) * w_t[HEAD_DIM + h:HEAD_DIM + h + 1, :]
        sc = term if sc is None else sc + term
    kpos_t = tail + lax.broadcasted_iota(jnp.int32, (CAUSAL_STEP, tq), 0)
    qpos_t = qblk * tq + lax.broadcasted_iota(jnp.int32, (CAUSAL_STEP, tq), 1)
    causal_t = kpos_t <= qpos_t
    if tail:
        sc_ref[0:tail, :] = sc[0:tail]
    sc_ref[tail:klen, :] = jnp.where(causal_t, sc[tail:klen], neg_inf)

    kf = float(topk)
    qpos_row = qblk * tq + lax.broadcasted_iota(jnp.int32, (1, tq), 1)
    search = qpos_row >= topk

    def scores():
        return sc_ref[0:klen, :]

    def count(pred):
        return _col_reduce(jnp.where(pred, 1.0, 0.0), jnp.sum)

    hi0 = _col_reduce(scores(), jnp.max)
    lo0 = _col_reduce(jnp.where(causal_t, sc_ref[tail:klen, :], jnp.inf), jnp.min)
    if tail:
        lo0 = jnp.minimum(lo0, _col_reduce(sc_ref[0:tail, :], jnp.min))

    def bisect(_, carry):
        lo, hi = carry
        mid = 0.5 * (lo + hi)
        ge = count(scores() >= mid) >= kf
        return jnp.where(ge, mid, lo), jnp.where(ge, hi, mid)

    lo, _ = lax.fori_loop(0, n_bisect, bisect, (lo0, hi0))

    def too_low(n_gt):
        return jnp.max(jnp.where(jnp.logical_and(search, n_gt >= kf), 1.0, 0.0))

    def climb(carry):
        thr, n_gt, _ = carry
        s = scores()
        nxt = _col_reduce(jnp.where(s > thr, s, jnp.inf), jnp.min)
        thr = jnp.where(jnp.logical_and(search, n_gt >= kf), nxt, thr)
        n_gt = count(scores() > thr)
        return thr, n_gt, too_low(n_gt)

    s = scores()
    thr0 = _col_reduce(jnp.where(s >= lo, s, jnp.inf), jnp.min)
    n_gt0 = count(s > thr0)
    thr, n_gt, _ = lax.while_loop(lambda c: c[2] > 0.0, climb, (thr0, n_gt0, too_low(n_gt0)))

    need = kf - n_gt
    open_row = jnp.where(search, neg_inf, 0.0)
    n_blk = klen // MXU_DIM
    s = scores()
    tie = jnp.where(s == thr, 1.0, 0.0).astype(BF16)
    tie_cat = jnp.concatenate([tie[MXU_DIM * j:MXU_DIM * (j + 1)] for j in range(n_blk)], axis=1)
    tri = jnp.where(lax.broadcasted_iota(jnp.int32, (MXU_DIM, MXU_DIM), 0)
                    >= lax.broadcasted_iota(jnp.int32, (MXU_DIM, MXU_DIM), 1), 1.0, 0.0).astype(BF16)
    prefix = _dot(tri, tie_cat)
    before = jnp.zeros((1, tq), F32)
    for j in range(n_blk):
        rows = slice(MXU_DIM * j, MXU_DIM * (j + 1))
        rank = prefix[:, tq * j:tq * (j + 1)] + before
        before = before + prefix[MXU_DIM - 1:MXU_DIM, tq * j:tq * (j + 1)]
        sj = s[rows]
        admitted = jnp.where(sj == thr, jnp.where(rank <= need, 0.0, neg_inf), neg_inf)
        bias = jnp.maximum(jnp.where(sj > thr, 0.0, admitted), open_row)
        if j == n_blk - 1:
            bias = jnp.where(causal_t, bias, neg_inf)
        bias_ref[rows, :] = bias

    q_all = _dsa_queries(qa_ref)
    qsq = q_all.astype(F32)
    qn2 = _dot_nt(jnp.ones((8, LANES), BF16), (qsq * qsq).astype(BF16))[0:1, :]
    shift = jnp.sqrt(qn2) * (_lane_tile(kmax_ref[...], A_HEADS * tq) * SHIFT_MARGIN)
    bias = bias_ref[0:klen, :]
    st = _dot_nt(ks_ref[0, 0:klen, 0:128], q_all) + jnp.concatenate([bias] * A_HEADS, axis=1) - shift
    e = jnp.exp2(st)
    l = _col_reduce(e, jnp.sum)
    _dsa_write(_dot(vt_ref[0:HEAD_DIM, 0:klen], e.astype(BF16)), l, o_ref, ot_ref)

    @pl.when(jnp.logical_not(jnp.min(l) >= DENOM_FLOOR))
    def _():
        if klen < seq:
            bias_ref[klen:seq, :] = jnp.full((seq - klen, tq), neg_inf, F32)
        flag_ref[0] = 1


def _dsa_body(qa_ref, qi_ref, wi_ref, ks_ref, o_ref, vt_ref, sc_ref, bias_ref, ot_ref, kmax_ref,
              flag_ref, *, topk, n_bisect):
    qblk = pl.program_id(1)
    seq = ks_ref.shape[1]
    flag_ref[0] = 0

    @pl.when(qblk == 0)
    def _():
        vt_ref[...] = ks_ref[0, :, 256:384].astype(F32).T.astype(BF16)
        ka = ks_ref[0, :, 0:LANES].astype(F32)
        lane = lax.broadcasted_iota(jnp.int32, ka.shape, 1)
        kn2 = jnp.sum(jnp.where(lane < HEAD_DIM, ka * ka, 0.0), axis=1, keepdims=True)
        kmax_ref[...] = jnp.broadcast_to(jnp.sqrt(jnp.max(kn2, axis=0, keepdims=True)), kmax_ref.shape)

    blocks_per_step = CAUSAL_STEP // DSA_QBLK
    for c in range(seq // CAUSAL_STEP):
        @pl.when(qblk // blocks_per_step == c)
        def _(c=c):
            _dsa_keys(CAUSAL_STEP * (c + 1), qblk, qa_ref, qi_ref, wi_ref, ks_ref, o_ref, vt_ref,
                      sc_ref, bias_ref, ot_ref, kmax_ref, flag_ref, topk, n_bisect)

    @pl.when(flag_ref[0] != 0)
    def _():
        bias = bias_ref[...]
        st = _dot_nt(ks_ref[0, :, 0:128], _dsa_queries(qa_ref)) + jnp.concatenate([bias] * A_HEADS, axis=1)
        e = jnp.exp2(st - _col_reduce(st, jnp.max))
        _dsa_write(_dot(vt_ref[0:HEAD_DIM, :], e.astype(BF16)), _col_reduce(e, jnp.sum), o_ref, ot_ref)


def _dsa(qa, qi, wi, ks, topk, n_bisect):
    b, s, _ = qa.shape
    tq = DSA_QBLK
    blk = lambda bi, qi_: (bi, qi_, 0)
    return pl.pallas_call(
        functools.partial(_dsa_body, topk=topk, n_bisect=n_bisect),
        grid=(b, s // tq),
        in_specs=[
            pl.BlockSpec((1, tq, 512), blk),
            pl.BlockSpec((1, tq, 256), blk),
            pl.BlockSpec((1, tq, LANES), blk),
            pl.BlockSpec((1, s, 384), lambda bi, qi_: (bi, 0, 0)),
        ],
        out_specs=pl.BlockSpec((1, tq, 512), blk),
        out_shape=jax.ShapeDtypeStruct((b, s, 512), BF16),
        scratch_shapes=[
            pltpu.VMEM((LANES, s), BF16),
            pltpu.VMEM((s, tq), F32),
            pltpu.VMEM((s, tq), F32),
            pltpu.VMEM((A_HEADS * HEAD_DIM, tq), F32),
            pltpu.VMEM((1, LANES), F32),
            pltpu.SMEM((1,), jnp.int32),
        ],
        compiler_params=pltpu.CompilerParams(
            dimension_semantics=("arbitrary", "arbitrary"), vmem_limit_bytes=VMEM_LIMIT),
        name="dsa_attention",
    )(qa, qi, wi, ks)


def _diff_body(q_ref, k_ref, v_ref, lq1_ref, lk1_ref, lq2_ref, lk2_ref, o_ref, vt_ref, kmax_ref,
               flag_ref, *, lambda_init):
    qblk = pl.program_id(1)
    seq = k_ref.shape[2]
    tq = q_ref.shape[2]

    @pl.when(qblk == 0)
    def _():
        lane_k = lax.broadcasted_iota(jnp.int32, (seq, LANES), 1)
        for h in range(B_HEADS):
            vt_ref[h] = v_ref[0, h].astype(F32).T.astype(BF16)
            kf = k_ref[0, h].astype(F32)
            ksq = kf * kf
            for c in range(2):
                part = jnp.where((lane_k >= HEAD_DIM) if c else (lane_k < HEAD_DIM), ksq, 0.0)
                kn2 = jnp.max(jnp.sum(part, axis=1, keepdims=True), axis=0, keepdims=True)
                kmax_ref[h, c:c + 1, :] = jnp.broadcast_to(jnp.sqrt(kn2), (1, LANES))

    lam = (jnp.exp(jnp.sum(lq1_ref[...] * lk1_ref[...], axis=1, keepdims=True))
           - jnp.exp(jnp.sum(lq2_ref[...] * lk2_ref[...], axis=1, keepdims=True)) + lambda_init)
    lane = lax.broadcasted_iota(jnp.int32, (tq, LANES), 1)
    diag = (lax.broadcasted_iota(jnp.int32, (tq, 2 * tq), 0)
            <= (lax.broadcasted_iota(jnp.int32, (tq, 2 * tq), 1) & (tq - 1)))

    def queries(h):
        q = q_ref[0, h]
        zero = jnp.zeros_like(q)
        return jnp.concatenate([jnp.where(lane < HEAD_DIM, q, zero), jnp.where(lane >= HEAD_DIM, q, zero)],
                               axis=0)

    def write(h, acc, l):
        acc = acc * (1.0 / l)
        ot = acc[:, 0:tq] - lam * acc[:, tq:2 * tq]
        ot = ot * lax.rsqrt(jnp.mean(ot * ot, axis=0, keepdims=True) + EPS) * (1.0 - lambda_init)
        o_ref[0, h] = ot.T.astype(BF16)

    def attend(klen, h, underflow):
        tail = klen - tq
        q2 = queries(h)
        qsq = q2.astype(F32)
        qn2 = _dot_nt(jnp.ones((8, LANES), BF16), (qsq * qsq).astype(BF16))[0:1, :]
        kmax = jnp.concatenate([_lane_tile(kmax_ref[h, 0:1, :], tq), _lane_tile(kmax_ref[h, 1:2, :], tq)],
                               axis=1)
        shift = jnp.sqrt(qn2) * (kmax * SHIFT_MARGIN)
        st = _dot_nt(k_ref[0, h, 0:klen, :], q2) - shift
        e = jnp.exp2(jnp.where(diag, st[tail:klen], -jnp.inf))
        l = _col_reduce(e, jnp.sum)
        acc = _dot(vt_ref[h, :, tail:klen], e.astype(BF16))
        if tail:
            e = jnp.exp2(st[0:tail])
            l = l + _col_reduce(e, jnp.sum)
            acc = acc + _dot(vt_ref[h, :, 0:tail], e.astype(BF16))
        write(h, acc, l)
        return jnp.maximum(underflow, jnp.where(jnp.min(l) >= DENOM_FLOOR, 0, 1))

    flag_ref[0] = 0
    for c in range(seq // tq):
        @pl.when(qblk == c)
        def _(c=c):
            underflow = 0
            for h in range(B_HEADS):
                underflow = attend(tq * (c + 1), h, underflow)
            flag_ref[0] = underflow

    @pl.when(flag_ref[0] != 0)
    def _():
        kpos = lax.broadcasted_iota(jnp.int32, (seq, 2 * tq), 0)
        qpos = qblk * tq + (lax.broadcasted_iota(jnp.int32, (seq, 2 * tq), 1) & (tq - 1))

        def redo(h, carry):
            st = jnp.where(kpos <= qpos, _dot_nt(k_ref[0, h], queries(h)), -jnp.inf)
            e = jnp.exp2(st - _col_reduce(st, jnp.max))
            write(h, _dot(vt_ref[h], e.astype(BF16)), _col_reduce(e, jnp.sum))
            return carry

        lax.fori_loop(0, B_HEADS, redo, 0)


def _diff(qb, kb, vb, lq1, lk1, lq2, lk2, lambda_init):
    b, _, s, _ = qb.shape
    tq = min(DIFF_QBLK, s)
    vec = pl.BlockSpec((1, HEAD_DIM), lambda bi, qi_: (0, 0))
    keys = pl.BlockSpec((1, B_HEADS, s, LANES), lambda bi, qi_: (bi, 0, 0, 0))
    blk = pl.BlockSpec((1, B_HEADS, tq, LANES), lambda bi, qi_: (bi, 0, qi_, 0))
    return pl.pallas_call(
        functools.partial(_diff_body, lambda_init=lambda_init),
        grid=(b, s // tq),
        in_specs=[blk, keys, keys, vec, vec, vec, vec],
        out_specs=blk,
        out_shape=jax.ShapeDtypeStruct((b, B_HEADS, s, LANES), BF16),
        scratch_shapes=[pltpu.VMEM((B_HEADS, LANES, s), BF16),
                        pltpu.VMEM((B_HEADS, 8, LANES), F32),
                        pltpu.SMEM((1,), jnp.int32)],
        compiler_params=pltpu.CompilerParams(
            dimension_semantics=("arbitrary", "arbitrary"), vmem_limit_bytes=VMEM_LIMIT),
        name="diff_attention",
    )(qb, kb, vb, lq1, lk1, lq2, lk2)


def _memkv_body(mem_ref, g_ref, wk_ref, wv_ref, gk_ref, k_ref, v_ref):
    mem = mem_ref[0]
    memn = (mem * _rms_scale(mem) * g_ref[...]).astype(BF16)
    k = _dot(memn, wk_ref[...])
    hd = gk_ref.shape[1]
    for h in range(k.shape[1] // hd):
        kh = k[:, hd * h:hd * (h + 1)]
        k_ref[0, :, hd * h:hd * (h + 1)] = (kh * _rms_scale(kh) * gk_ref[...]).astype(BF16)
    v_ref[0] = _dot(memn, wv_ref[...]).astype(BF16)


def _memkv(mem, g_mem, w_xk, w_xv, g_xk):
    b, m, d = mem.shape
    const = lambda bi: (0, 0)
    blk = pl.BlockSpec((1, m, d), lambda bi: (bi, 0, 0))
    return pl.pallas_call(
        _memkv_body,
        grid=(b,),
        in_specs=[blk, pl.BlockSpec((1, d), const), pl.BlockSpec((d, d), const),
                  pl.BlockSpec((d, d), const), pl.BlockSpec(g_xk.shape, const)],
        out_specs=[blk, blk],
        out_shape=[jax.ShapeDtypeStruct((b, m, d), BF16)] * 2,
        compiler_params=pltpu.CompilerParams(
            dimension_semantics=("arbitrary",), vmem_limit_bytes=VMEM_LIMIT),
        name="mem_kv",
    )(mem, g_mem, w_xk, w_xv, g_xk)


def _xattn_body(x_ref, oa_ref, ob_ref, wo_ref, g_ref, wq_ref, gq_ref, k_ref, v_ref, wxo_ref, h_ref):
    half = oa_ref.shape[2]
    ob = jnp.concatenate([ob_ref[0, h] for h in range(B_HEADS)], axis=1)
    h1 = x_ref[0] + _dot(oa_ref[0], wo_ref[0:half, :]) + _dot(ob, wo_ref[half:2 * half, :])
    hn = (h1 * _rms_scale(h1) * g_ref[...]).astype(BF16)
    q = _dot(hn, wq_ref[...])
    hd = gq_ref.shape[1]
    outs = []
    for h in range(q.shape[1] // hd):
        sl = slice(hd * h, hd * (h + 1))
        qh = q[:, sl]
        qh = (qh * _rms_scale(qh) * gq_ref[...] * (hd ** -0.5)).astype(BF16)
        s = _dot_nt(qh, k_ref[0, :, sl])
        e = jnp.exp(s - jnp.max(s, axis=-1, keepdims=True))
        p = (e * (1.0 / jnp.sum(e, axis=-1, keepdims=True))).astype(BF16)
        outs.append(_dot(p, v_ref[0, :, sl]).astype(BF16))
    o = jnp.concatenate(outs, axis=1)
    h_ref[0] = h1 + _dot(o, wxo_ref[...])


def _xattn(x, oa, ob, w_out, g_x, w_xq, g_xq, kmem, vmem, w_xo, tm):
    b, s, d = x.shape
    m = kmem.shape[1]
    const = lambda bi, ti: (0, 0)
    tok = lambda w: pl.BlockSpec((1, tm, w), lambda bi, ti: (bi, ti, 0))
    memblk = pl.BlockSpec((1, m, d), lambda bi, ti: (bi, 0, 0))
    return pl.pallas_call(
        _xattn_body,
        grid=(b, s // tm),
        in_specs=[tok(d), tok(oa.shape[2]),
                  pl.BlockSpec((1, B_HEADS, tm, LANES), lambda bi, ti: (bi, 0, ti, 0)),
                  pl.BlockSpec(w_out.shape, const), pl.BlockSpec((1, d), const),
                  pl.BlockSpec(w_xq.shape, const), pl.BlockSpec(g_xq.shape, const),
                  memblk, memblk, pl.BlockSpec(w_xo.shape, const)],
        out_specs=tok(d),
        out_shape=jax.ShapeDtypeStruct((b, s, d), F32),
        compiler_params=pltpu.CompilerParams(
            dimension_semantics=("arbitrary", "arbitrary"), vmem_limit_bytes=VMEM_LIMIT),
        name="outproj_xattn",
    )(x, oa, ob, w_out, g_x, w_xq, g_xq, kmem, vmem, w_xo)


HALO = 8


def _ffn_body(h_ref, g_ref, win_ref, cw_ref, cb_ref, wo_ref, o_ref, a_ref):
    tm = h_ref.shape[1]
    dff = wo_ref.shape[0]

    @pl.when(pl.program_id(1) == 0)
    def _():
        a_ref[0:HALO, :] = jnp.zeros((HALO, a_ref.shape[1]), F32)

    h = h_ref[0]
    hn = (h * _rms_scale(h) * g_ref[...]).astype(BF16)
    a_ref[HALO:HALO + tm, :] = _dot(hn, win_ref[:, 0:dff])
    gate = _dot(hn, win_ref[:, dff:2 * dff])
    conv = cb_ref[...]
    for j in range(CONV_W):
        off = HALO - (CONV_W - 1) + j
        conv = conv + a_ref[off:off + tm, :] * cw_ref[j:j + 1, :]
    a_ref[0:HALO, :] = a_ref[tm:tm + HALO, :]
    u = (jax.nn.gelu(conv) * gate).astype(BF16)
    o_ref[0] = h + _dot(u, wo_ref[...])


def _ffn(h, g_ffn, w_in, conv_w, conv_b, w_o, tm):
    b, s, d = h.shape
    dff = w_o.shape[0]
    const = lambda bi, ti: (0, 0)
    tok = pl.BlockSpec((1, tm, d), lambda bi, ti: (bi, ti, 0))
    return pl.pallas_call(
        _ffn_body,
        grid=(b, s // tm),
        in_specs=[tok, pl.BlockSpec((1, d), const), _const_spec(w_in.shape),
                  pl.BlockSpec(conv_w.shape, const),
                  pl.BlockSpec((1, dff), const), _const_spec(w_o.shape)],
        out_specs=tok,
        out_shape=jax.ShapeDtypeStruct((b, s, d), F32),
        scratch_shapes=[pltpu.VMEM((tm + HALO, dff), F32)],
        compiler_params=pltpu.CompilerParams(
            dimension_semantics=("arbitrary", "arbitrary"), vmem_limit_bytes=VMEM_LIMIT),
        name="conv_glu",
    )(h, g_ffn, w_in, conv_w, conv_b, w_o)


def _rearranged_w_in(w_in):
    sizes = (A_HEADS * HEAD_DIM, HEAD_DIM, HEAD_DIM, IDX_HEADS * HEAD_DIM, HEAD_DIM, IDX_HEADS,
             2 * B_HEADS * HEAD_DIM, 2 * B_HEADS * HEAD_DIM, B_HEADS * 2 * HEAD_DIM)
    offs = [0]
    for sz in sizes:
        offs.append(offs[-1] + sz)
    q_a, k_a, v_a, q_i, k_i, w_i, q_b, k_b, v_b = [w_in[:, offs[i]:offs[i + 1]] for i in range(9)]
    pad = jnp.zeros((w_in.shape[0], HEAD_DIM - IDX_HEADS), w_in.dtype)
    w_all = jnp.concatenate([q_a, q_i, k_a, k_a, k_i, k_i, v_a, w_i, pad, q_b, k_b, v_b], axis=1)
    assert w_all.shape[1] == _C_END
    return w_all.astype(BF16)


def kernel(x, mem, positions, g_mix, w_in, g_qa, g_ka, g_qb, g_kb, lam_q1, lam_k1, lam_q2, lam_k2,
           w_out, g_xattn, g_mem, w_xq, w_xk, w_xv, w_xo, g_xq, g_xk, g_ffn, w_ffn_in, conv_w, conv_b,
           w_ffn_out):
    b, s, d = x.shape
    depth = g_mix.shape[0]
    topk = min(TOPK_MAX, s // 4)
    tm = min(512, s)

    inv_freq = 1.0 / (ROPE_THETA ** (jnp.arange(0, HEAD_DIM, 2, dtype=F32) / HEAD_DIM))
    ang = positions.reshape(b * s, 1).astype(F32) * jnp.tile(inv_freq, 4)[None, :]
    sign = jnp.tile(jnp.repeat(jnp.array([-1.0, 1.0], F32), HEAD_DIM // 2), 2)
    cos128 = jnp.cos(ang)
    sin128 = jnp.sin(ang) * sign[None, :]
    blk = jnp.arange(MXU_DIM) // HEAD_DIM
    bd = jnp.where(blk[:, None] == blk[None, :], 1.0 / HEAD_DIM, 0.0).astype(BF16)

    h = x
    for l in range(depth):
        lambda_init = 0.8 - 0.6 * math.exp(-0.3 * l)
        gains = jnp.stack([jnp.tile(g, 512 // HEAD_DIM) for g in (g_qa[l], g_ka[l], g_qb[l], g_kb[l])])
        gains = jnp.concatenate([gains, jnp.ones((4, 512), F32)], axis=0)
        qa, qi, ks, wi, qb, kb, vb = _inproj(
            h.reshape(b * s, d), g_mix[l][None, :], _rearranged_w_in(w_in[l]), cos128, sin128, bd, gains,
            min(1024, s), s)
        r3 = lambda t: t.reshape(b, s, t.shape[-1])
        out_a = _dsa(r3(qa), r3(qi), r3(wi), r3(ks), topk, n_bisect=20)
        out_b = _diff(qb, kb, vb, lam_q1[l][None, :], lam_k1[l][None, :],
                      lam_q2[l][None, :], lam_k2[l][None, :], lambda_init)
        kmem, vmem = _memkv(mem, g_mem[l][None, :], w_xk[l].astype(BF16), w_xv[l].astype(BF16),
                            g_xk[l][None, :])
        h = _xattn(h, out_a, out_b, w_out[l].astype(BF16), g_xattn[l][None, :], w_xq[l].astype(BF16),
                   g_xq[l][None, :], kmem, vmem, w_xo[l].astype(BF16), min(1024, s))
        dff = w_ffn_out.shape[1]
        cw = jnp.concatenate([conv_w[l], jnp.zeros((8 - CONV_W, dff), F32)], axis=0)
        h = _ffn(h, g_ffn[l][None, :], w_ffn_in[l].astype(BF16), cw, conv_b[l][None, :],
                 w_ffn_out[l].astype(BF16), tm)
    return h
```

```python
import functools
import math

import jax
import jax.numpy as jnp
from jax import lax
from jax.experimental import pallas as pl
from jax.experimental.pallas import tpu as pltpu

F32 = jnp.float32
BF16 = jnp.bfloat16

EPS = 1e-6
ROPE_THETA = 10000.0
HEAD_DIM = 64
A_HEADS = 8
IDX_HEADS = 4
TOPK_MAX = 256
B_HEADS = 4
X_HEADS = 4
CONV_W = 3
LANES = 128
MXU_DIM = 256
DSA_QBLK = 128
DSA_PAIR = 2
DIFF_QBLK = 256
CAUSAL_STEP = MXU_DIM
VMEM_LIMIT = 56 * 1024 * 1024
LOG2E = 1.4426950408889634
SHIFT_MARGIN = 1.02
DENOM_FLOOR = 2.0 ** -40

_C_QA = 0
_C_QI = 512
_C_KS = 768
_C_QB = 1152
_C_KB = 1664
_C_VB = 2176
_C_END = 2688


def _dot(a, b):
    return jnp.dot(a, b, preferred_element_type=F32)


def _dot_nt(a, b):
    return lax.dot_general(a, b, (((1,), (1,)), ((), ())), preferred_element_type=F32)


def _rms_scale(x):
    return lax.rsqrt(jnp.mean(x * x, axis=-1, keepdims=True) + EPS)


def _const_spec(shape):
    zeros = (0,) * len(shape)
    return pl.BlockSpec(shape, lambda *_: zeros, pipeline_mode=pl.Buffered(1))


def _lane_tile(t, width):
    reps = width // t.shape[1]
    return t if reps == 1 else jnp.concatenate([t] * reps, axis=1)


def _col_reduce(x, reduce_fn):
    rows, cols = x.shape
    slab = 8 * max(1, 8 * LANES // cols)
    if rows % slab or rows == slab:
        return reduce_fn(x, axis=0, keepdims=True)
    part = reduce_fn(x.reshape(rows // slab, slab, cols), axis=0)
    return reduce_fn(part, axis=0, keepdims=True)


def _inproj_body(x_ref, gmix_ref, w_ref, cos_ref, sin_ref, bd_ref, gains_ref,
                 qa_ref, qi_ref, ks_ref, wi_ref, qb_ref, kb_ref, vb_ref):
    x = x_ref[...]
    hn = (x * _rms_scale(x) * gmix_ref[...]).astype(BF16)
    def spread(t):
        turned = [t] + [pltpu.roll(t, 32 * k, 1) for k in range(1, 4)]
        group = lax.broadcasted_iota(jnp.int32, t.shape, 1) >> 5
        parts = []
        for j in range(4):
            d = (group - j) & 3
            parts.append(jnp.where(d == 0, turned[0], jnp.where(d == 1, turned[1],
                                                               jnp.where(d == 2, turned[2], turned[3]))))
        return jnp.concatenate(parts, axis=0)

    cos = spread(cos_ref[...])
    sin = spread(sin_ref[...]) * gains_ref[4:5, 0:LANES]
    bd = bd_ref[...]

    def proj(c0, width):
        return _dot(hn, w_ref[:, c0:c0 + width])

    def group_rms_scale(p):
        sq = (p * p).astype(BF16)
        outs = []
        for j in range(p.shape[1] // MXU_DIM):
            outs.append(_dot(sq[:, MXU_DIM * j:MXU_DIM * (j + 1)], bd))
        ms = outs[0] if len(outs) == 1 else jnp.concatenate(outs, axis=1)
        return lax.rsqrt(ms + EPS)

    def rope(y):
        width = y.shape[1]
        lane = lax.broadcasted_iota(jnp.int32, y.shape, 1)
        first_half = (lane & (HEAD_DIM - 1)) < (HEAD_DIM // 2)
        swapped = jnp.where(first_half, pltpu.roll(y, width - HEAD_DIM // 2, 1),
                            pltpu.roll(y, HEAD_DIM // 2, 1))
        return y * _lane_tile(cos, width) + swapped * _lane_tile(sin, width)

    sm_scale = HEAD_DIM ** -0.5 * LOG2E

    p = proj(_C_QA, 512)
    qa_ref[...] = (rope(p * group_rms_scale(p) * gains_ref[0:1, :]) * sm_scale).astype(BF16)

    p = proj(_C_QI, 256)
    qi_ref[...] = rope(p).astype(BF16)

    p = proj(_C_KS, 384)
    p01 = p[:, 0:256]
    lane = lax.broadcasted_iota(jnp.int32, p01.shape, 1)
    y01 = jnp.where(lane < LANES, p01 * group_rms_scale(p01) * gains_ref[1:2, 0:256], p01)
    y01 = rope(y01)
    p2 = p[:, 256:384]
    lane = lax.broadcasted_iota(jnp.int32, p2.shape, 1)
    p2 = jnp.where(lane < HEAD_DIM, p2, p2 * (IDX_HEADS ** -0.5 * HEAD_DIM ** -0.5))
    ks_ref[:, 0:256] = y01.astype(BF16)
    ks_ref[:, 256:384] = p2.astype(BF16)
    wi_ref[...] = p2

    def store_heads(ref, y):
        for h in range(B_HEADS):
            ref[0, h] = y[:, LANES * h:LANES * (h + 1)].astype(BF16)

    p = proj(_C_QB, 512)
    store_heads(qb_ref, rope(p * group_rms_scale(p) * gains_ref[2:3, :]) * sm_scale)

    p = proj(_C_KB, 512)
    store_heads(kb_ref, rope(p * group_rms_scale(p) * gains_ref[3:4, :]))

    store_heads(vb_ref, proj(_C_VB, 512))


def _inproj(x2, gmix, w_all, cos128, sin128, bd, gains, tm, seq):
    n, d = x2.shape
    row = lambda i: (i, 0)
    const = lambda i: (0, 0)
    outs = [(512, BF16), (256, BF16), (384, BF16), (LANES, F32)]
    tiles = seq // tm
    head_spec = pl.BlockSpec((1, B_HEADS, tm, LANES), lambda i: (i // tiles, 0, i % tiles, 0))
    head_shape = jax.ShapeDtypeStruct((n // seq, B_HEADS, seq, LANES), BF16)
    return pl.pallas_call(
        _inproj_body,
        grid=(n // tm,),
        in_specs=[
            pl.BlockSpec((tm, d), row),
            pl.BlockSpec((1, d), const),
            pl.BlockSpec(w_all.shape, const),
            pl.BlockSpec((tm // 4, LANES), row),
            pl.BlockSpec((tm // 4, LANES), row),
            pl.BlockSpec(bd.shape, const),
            pl.BlockSpec(gains.shape, const),
        ],
        out_specs=[pl.BlockSpec((tm, w), row) for w, _ in outs] + [head_spec] * 3,
        out_shape=[jax.ShapeDtypeStruct((n, w), dt) for w, dt in outs] + [head_shape] * 3,
        compiler_params=pltpu.CompilerParams(
            dimension_semantics=("arbitrary",), vmem_limit_bytes=VMEM_LIMIT),
        name="inproj",
    )(x2, gmix, w_all, cos128, sin128, bd, gains)


def _group_reduce(x, reduce_fn):
    g, rows, cols = x.shape
    slab = 8 * max(1, 8 * LANES // cols)
    if rows % slab or rows == slab:
        return reduce_fn(x, axis=1, keepdims=True)
    combine = {jnp.sum: jnp.add, jnp.max: jnp.maximum, jnp.min: jnp.minimum}[reduce_fn]
    part = x[:, 0:slab]
    for i in range(1, rows // slab):
        part = combine(part, x[:, i * slab:(i + 1) * slab])
    return reduce_fn(part, axis=1, keepdims=True)


def _dsa_queries(qa_ref, rows):
    qa = qa_ref[0, rows, :]
    lane = lax.broadcasted_iota(jnp.int32, (DSA_QBLK, LANES), 1)
    rows = []
    for h in range(A_HEADS):
        slab = qa[:, LANES * (h // 2):LANES * (h // 2 + 1)]
        keep = (lane >= HEAD_DIM) if h % 2 else (lane < HEAD_DIM)
        rows.append(jnp.where(keep, slab, jnp.zeros_like(slab)))
    return jnp.concatenate(rows, axis=0)


def _dsa_write(ot, l, o_ref, ot_ref, rows):
    tq = DSA_QBLK
    ot = ot * (1.0 / l)
    for h in range(A_HEADS):
        ot_ref[HEAD_DIM * h:HEAD_DIM * (h + 1), :] = ot[:, tq * h:tq * (h + 1)]
    o_ref[0, rows, :] = ot_ref[...].T.astype(BF16)


def _dsa_rows(block):
    return pl.ds(pl.multiple_of(block * DSA_QBLK, DSA_QBLK), DSA_QBLK)


def _dsa_keys(klen, pair, qa_ref, qi_ref, wi_ref, ks_ref, o_ref, vt_ref, sc_ref, bias_ref, ot_ref,
              kmax_ref, flag_ref, topk, n_bisect):
    tq = DSA_QBLK
    seq = ks_ref.shape[1]
    tail = klen - CAUSAL_STEP
    neg_inf = -jnp.inf

    lane_q = lax.broadcasted_iota(jnp.int32, (tq, LANES), 1)
    first_head = lane_q < HEAD_DIM

    def head_rows(slab, odd):
        return jnp.where(first_head != odd, slab, jnp.zeros_like(slab))

    shape_t = (DSA_PAIR, CAUSAL_STEP, tq)
    qpos_t = ((pair * DSA_PAIR + lax.broadcasted_iota(jnp.int32, shape_t, 0)) * tq
              + lax.broadcasted_iota(jnp.int32, shape_t, 2))
    causal_t = tail + lax.broadcasted_iota(jnp.int32, shape_t, 1) <= qpos_t

    def indexer(blk, carry):
        rows = _dsa_rows(blk)
        qi = qi_ref[0, rows, :]
        qi_stack = jnp.concatenate(
            [head_rows(qi[:, LANES * (h // 2):LANES * (h // 2 + 1)], bool(h % 2)) for h in range(IDX_HEADS)],
            axis=0)
        lg = _dot_nt(ks_ref[0, 0:klen, 128:256], qi_stack)
        w_t = wi_ref[0, rows, :].T
        sc = None
        for h in range(IDX_HEADS):
            term = jnp.maximum(lg[:, tq * h:tq * (h + 1)], 0.0) * w_t[HEAD_DIM + h:HEAD_DIM + h + 1, :]
            sc = term if sc is None else sc + term
        sc_ref[blk, 0:klen, :] = sc
        return carry

    lax.fori_loop(0, DSA_PAIR, indexer, 0)
    sc_ref[:, tail:klen, :] = jnp.where(causal_t, sc_ref[:, tail:klen, :], neg_inf)

    kf = float(topk)
    search = qpos_t[:, 0:1, :] >= topk

    def scores():
        return sc_ref[:, 0:klen, :]

    def count(pred):
        return _group_reduce(jnp.where(pred, 1.0, 0.0), jnp.sum)

    hi0 = _group_reduce(scores(), jnp.max)
    lo0 = _group_reduce(jnp.where(causal_t, sc_ref[:, tail:klen, :], jnp.inf), jnp.min)
    if tail:
        lo0 = jnp.minimum(lo0, _group_reduce(sc_ref[:, 0:tail, :], jnp.min))

    def bisect(_, carry):
        lo, hi = carry
        mid = 0.5 * (lo + hi)
        ge = count(scores() >= mid) >= kf
        return jnp.where(ge, mid, lo), jnp.where(ge, hi, mid)

    lo, _ = lax.fori_loop(0, n_bisect, bisect, (lo0, hi0))

    def too_low(n_gt):
        return jnp.max(jnp.where(jnp.logical_and(search, n_gt >= kf), 1.0, 0.0))

    def climb(carry):
        thr, n_gt, _ = carry
        s = scores()
        nxt = _group_reduce(jnp.where(s > thr, s, jnp.inf), jnp.min)
        thr = jnp.where(jnp.logical_and(search, n_gt >= kf), nxt, thr)
        n_gt = count(scores() > thr)
        return thr, n_gt, too_low(n_gt)

    s = scores()
    thr0 = _group_reduce(jnp.where(s >= lo, s, jnp.inf), jnp.min)
    n_gt0 = count(s > thr0)
    thr, n_gt, _ = lax.while_loop(lambda c: c[2] > 0.0, climb, (thr0, n_gt0, too_low(n_gt0)))

    need = kf - n_gt
    open_row = jnp.where(search, neg_inf, 0.0)
    n_blk = klen // MXU_DIM
    s = scores()
    tie = jnp.where(s == thr, 1.0, 0.0).astype(BF16)
    tie_cat = jnp.concatenate([tie[g, MXU_DIM * j:MXU_DIM * (j + 1)]
                               for g in range(DSA_PAIR) for j in range(n_blk)], axis=1)
    tri = jnp.where(lax.broadcasted_iota(jnp.int32, (MXU_DIM, MXU_DIM), 0)
                    >= lax.broadcasted_iota(jnp.int32, (MXU_DIM, MXU_DIM), 1), 1.0, 0.0).astype(BF16)
    prefix = _dot(tri, tie_cat)
    for g in range(DSA_PAIR):
        before = jnp.zeros((1, tq), F32)
        for j in range(n_blk):
            rows = slice(MXU_DIM * j, MXU_DIM * (j + 1))
            col = (g * n_blk + j) * tq
            rank = prefix[:, col:col + tq] + before
            before = before + prefix[MXU_DIM - 1:MXU_DIM, col:col + tq]
            sj = s[g, rows]
            admitted = jnp.where(sj == thr[g], jnp.where(rank <= need[g], 0.0, neg_inf), neg_inf)
            bias = jnp.maximum(jnp.where(sj > thr[g], 0.0, admitted), open_row[g])
            if j == n_blk - 1:
                bias = jnp.where(causal_t[g], bias, neg_inf)
            bias_ref[g, rows, :] = bias

    def attend(blk, underflow):
        rows = _dsa_rows(blk)
        q_all = _dsa_queries(qa_ref, rows)
        qsq = q_all.astype(F32)
        qn2 = _dot_nt(jnp.ones((8, LANES), BF16), (qsq * qsq).astype(BF16))[0:1, :]
        shift = jnp.sqrt(qn2) * (_lane_tile(kmax_ref[...], A_HEADS * tq) * SHIFT_MARGIN)
        bias = bias_ref[blk, 0:klen, :]
        st = _dot_nt(ks_ref[0, 0:klen, 0:128], q_all) + jnp.concatenate([bias] * A_HEADS, axis=1) - shift
        e = jnp.exp2(st)
        l = _col_reduce(e, jnp.sum)
        _dsa_write(_dot(vt_ref[0:HEAD_DIM, 0:klen], e.astype(BF16)), l, o_ref, ot_ref, rows)
        return jnp.maximum(underflow, jnp.where(jnp.min(l) >= DENOM_FLOOR, 0, 1))

    underflow = lax.fori_loop(0, DSA_PAIR, attend, 0)

    @pl.when(underflow != 0)
    def _():
        if klen < seq:
            bias_ref[:, klen:seq, :] = jnp.full((DSA_PAIR, seq - klen, tq), neg_inf, F32)
        flag_ref[0] = 1


def _dsa_body(qa_ref, qi_ref, wi_ref, ks_ref, o_ref, vt_ref, sc_ref, bias_ref, ot_ref, kmax_ref,
              flag_ref, *, topk, n_bisect):
    pair = pl.program_id(1)
    seq = ks_ref.shape[1]
    flag_ref[0] = 0

    @pl.when(pair == 0)
    def _():
        vt_ref[...] = ks_ref[0, :, 256:384].astype(F32).T.astype(BF16)
        ka = ks_ref[0, :, 0:LANES].astype(F32)
        lane = lax.broadcasted_iota(jnp.int32, ka.shape, 1)
        kn2 = jnp.sum(jnp.where(lane < HEAD_DIM, ka * ka, 0.0), axis=1, keepdims=True)
        kmax_ref[...] = jnp.broadcast_to(jnp.sqrt(jnp.max(kn2, axis=0, keepdims=True)), kmax_ref.shape)

    for c in range(seq // CAUSAL_STEP):
        @pl.when(pair == c)
        def _(c=c):
            _dsa_keys(CAUSAL_STEP * (c + 1), pair, qa_ref, qi_ref, wi_ref, ks_ref, o_ref, vt_ref,
                      sc_ref, bias_ref, ot_ref, kmax_ref, flag_ref, topk, n_bisect)

    @pl.when(flag_ref[0] != 0)
    def _():
        def redo(blk, carry):
            rows = _dsa_rows(blk)
            bias = bias_ref[blk]
            st = (_dot_nt(ks_ref[0, :, 0:128], _dsa_queries(qa_ref, rows))
                  + jnp.concatenate([bias] * A_HEADS, axis=1))
            e = jnp.exp2(st - _col_reduce(st, jnp.max))
            _dsa_write(_dot(vt_ref[0:HEAD_DIM, :], e.astype(BF16)), _col_reduce(e, jnp.sum), o_ref, ot_ref,
                       rows)
            return carry

        lax.fori_loop(0, DSA_PAIR, redo, 0)


def _dsa(qa, qi, wi, ks, topk, n_bisect):
    b, s, _ = qa.shape
    tq = DSA_QBLK
    step = DSA_PAIR * tq
    assert step == CAUSAL_STEP and s % step == 0
    blk = lambda bi, qi_: (bi, qi_, 0)
    return pl.pallas_call(
        functools.partial(_dsa_body, topk=topk, n_bisect=n_bisect),
        grid=(b, s // step),
        in_specs=[
            pl.BlockSpec((1, step, 512), blk),
            pl.BlockSpec((1, step, 256), blk),
            pl.BlockSpec((1, step, LANES), blk),
            pl.BlockSpec((1, s, 384), lambda bi, qi_: (bi, 0, 0)),
        ],
        out_specs=pl.BlockSpec((1, step, 512), blk),
        out_shape=jax.ShapeDtypeStruct((b, s, 512), BF16),
        scratch_shapes=[
            pltpu.VMEM((LANES, s), BF16),
            pltpu.VMEM((DSA_PAIR, s, tq), F32),
            pltpu.VMEM((DSA_PAIR, s, tq), F32),
            pltpu.VMEM((A_HEADS * HEAD_DIM, tq), F32),
            pltpu.VMEM((1, LANES), F32),
            pltpu.SMEM((1,), jnp.int32),
        ],
        compiler_params=pltpu.CompilerParams(
            dimension_semantics=("arbitrary", "arbitrary"), vmem_limit_bytes=VMEM_LIMIT),
        name="dsa_attention",
    )(qa, qi, wi, ks)


def _diff_body(q_ref, k_ref, v_ref, lq1_ref, lk1_ref, lq2_ref, lk2_ref, o_ref, vt_ref, kmax_ref,
               flag_ref, *, lambda_init):
    qblk = pl.program_id(1)
    seq = k_ref.shape[2]
    tq = q_ref.shape[2]

    @pl.when(qblk == 0)
    def _():
        lane_k = lax.broadcasted_iota(jnp.int32, (seq, LANES), 1)
        for h in range(B_HEADS):
            vt_ref[h] = v_ref[0, h].astype(F32).T.astype(BF16)
            kf = k_ref[0, h].astype(F32)
            ksq = kf * kf
            for c in range(2):
                part = jnp.where((lane_k >= HEAD_DIM) if c else (lane_k < HEAD_DIM), ksq, 0.0)
                kn2 = jnp.max(jnp.sum(part, axis=1, keepdims=True), axis=0, keepdims=True)
                kmax_ref[h, c:c + 1, :] = jnp.broadcast_to(jnp.sqrt(kn2), (1, LANES))

    lam = (jnp.exp(jnp.sum(lq1_ref[...] * lk1_ref[...], axis=1, keepdims=True))
           - jnp.exp(jnp.sum(lq2_ref[...] * lk2_ref[...], axis=1, keepdims=True)) + lambda_init)
    lane = lax.broadcasted_iota(jnp.int32, (tq, LANES), 1)
    diag = (lax.broadcasted_iota(jnp.int32, (tq, 2 * tq), 0)
            <= (lax.broadcasted_iota(jnp.int32, (tq, 2 * tq), 1) & (tq - 1)))

    def queries(h):
        q = q_ref[0, h]
        zero = jnp.zeros_like(q)
        return jnp.concatenate([jnp.where(lane < HEAD_DIM, q, zero), jnp.where(lane >= HEAD_DIM, q, zero)],
                               axis=0)

    def write(h, acc, l):
        acc = acc * (1.0 / l)
        ot = acc[:, 0:tq] - lam * acc[:, tq:2 * tq]
        ot = ot * lax.rsqrt(jnp.mean(ot * ot, axis=0, keepdims=True) + EPS) * (1.0 - lambda_init)
        o_ref[0, h] = ot.T.astype(BF16)

    def attend(klen, h, underflow):
        tail = klen - tq
        q2 = queries(h)
        qsq = q2.astype(F32)
        qn2 = _dot_nt(jnp.ones((8, LANES), BF16), (qsq * qsq).astype(BF16))[0:1, :]
        kmax = jnp.concatenate([_lane_tile(kmax_ref[h, 0:1, :], tq), _lane_tile(kmax_ref[h, 1:2, :], tq)],
                               axis=1)
        shift = jnp.sqrt(qn2) * (kmax * SHIFT_MARGIN)
        st = _dot_nt(k_ref[0, h, 0:klen, :], q2) - shift
        e = jnp.exp2(jnp.where(diag, st[tail:klen], -jnp.inf))
        l = _col_reduce(e, jnp.sum)
        acc = _dot(vt_ref[h, :, tail:klen], e.astype(BF16))
        if tail:
            e = jnp.exp2(st[0:tail])
            l = l + _col_reduce(e, jnp.sum)
            acc = acc + _dot(vt_ref[h, :, 0:tail], e.astype(BF16))
        write(h, acc, l)
        return jnp.maximum(underflow, jnp.where(jnp.min(l) >= DENOM_FLOOR, 0, 1))

    flag_ref[0] = 0
    for c in range(seq // tq):
        @pl.when(qblk == c)
        def _(c=c):
            underflow = 0
            for h in range(B_HEADS):
                underflow = attend(tq * (c + 1), h, underflow)
            flag_ref[0] = underflow

    @pl.when(flag_ref[0] != 0)
    def _():
        kpos = lax.broadcasted_iota(jnp.int32, (seq, 2 * tq), 0)
        qpos = qblk * tq + (lax.broadcasted_iota(jnp.int32, (seq, 2 * tq), 1) & (tq - 1))

        def redo(h, carry):
            st = jnp.where(kpos <= qpos, _dot_nt(k_ref[0, h], queries(h)), -jnp.inf)
            e = jnp.exp2(st - _col_reduce(st, jnp.max))
            write(h, _dot(vt_ref[h], e.astype(BF16)), _col_reduce(e, jnp.sum))
            return carry

        lax.fori_loop(0, B_HEADS, redo, 0)


def _diff(qb, kb, vb, lq1, lk1, lq2, lk2, lambda_init):
    b, _, s, _ = qb.shape
    tq = min(DIFF_QBLK, s)
    vec = pl.BlockSpec((1, HEAD_DIM), lambda bi, qi_: (0, 0))
    keys = pl.BlockSpec((1, B_HEADS, s, LANES), lambda bi, qi_: (bi, 0, 0, 0))
    blk = pl.BlockSpec((1, B_HEADS, tq, LANES), lambda bi, qi_: (bi, 0, qi_, 0))
    return pl.pallas_call(
        functools.partial(_diff_body, lambda_init=lambda_init),
        grid=(b, s // tq),
        in_specs=[blk, keys, keys, vec, vec, vec, vec],
        out_specs=blk,
        out_shape=jax.ShapeDtypeStruct((b, B_HEADS, s, LANES), BF16),
        scratch_shapes=[pltpu.VMEM((B_HEADS, LANES, s), BF16),
                        pltpu.VMEM((B_HEADS, 8, LANES), F32),
                        pltpu.SMEM((1,), jnp.int32)],
        compiler_params=pltpu.CompilerParams(
            dimension_semantics=("arbitrary", "arbitrary"), vmem_limit_bytes=VMEM_LIMIT),
        name="diff_attention",
    )(qb, kb, vb, lq1, lk1, lq2, lk2)


def _memkv_body(mem_ref, g_ref, wk_ref, wv_ref, gk_ref, k_ref, v_ref):
    mem = mem_ref[0]
    memn = (mem * _rms_scale(mem) * g_ref[...]).astype(BF16)
    k = _dot(memn, wk_ref[...])
    hd = gk_ref.shape[1]
    for h in range(k.shape[1] // hd):
        kh = k[:, hd * h:hd * (h + 1)]
        k_ref[0, :, hd * h:hd * (h + 1)] = (kh * _rms_scale(kh) * gk_ref[...]).astype(BF16)
    v_ref[0] = _dot(memn, wv_ref[...]).astype(BF16)


def _memkv(mem, g_mem, w_xk, w_xv, g_xk):
    b, m, d = mem.shape
    const = lambda bi: (0, 0)
    blk = pl.BlockSpec((1, m, d), lambda bi: (bi, 0, 0))
    return pl.pallas_call(
        _memkv_body,
        grid=(b,),
        in_specs=[blk, pl.BlockSpec((1, d), const), pl.BlockSpec((d, d), const),
                  pl.BlockSpec((d, d), const), pl.BlockSpec(g_xk.shape, const)],
        out_specs=[blk, blk],
        out_shape=[jax.ShapeDtypeStruct((b, m, d), BF16)] * 2,
        compiler_params=pltpu.CompilerParams(
            dimension_semantics=("arbitrary",), vmem_limit_bytes=VMEM_LIMIT),
        name="mem_kv",
    )(mem, g_mem, w_xk, w_xv, g_xk)


def _xattn_body(x_ref, oa_ref, ob_ref, wo_ref, g_ref, wq_ref, gq_ref, k_ref, v_ref, wxo_ref, h_ref):
    half = oa_ref.shape[2]
    ob = jnp.concatenate([ob_ref[0, h] for h in range(B_HEADS)], axis=1)
    h1 = x_ref[0] + _dot(oa_ref[0], wo_ref[0:half, :]) + _dot(ob, wo_ref[half:2 * half, :])
    hn = (h1 * _rms_scale(h1) * g_ref[...]).astype(BF16)
    q = _dot(hn, wq_ref[...])
    hd = gq_ref.shape[1]
    outs = []
    for h in range(q.shape[1] // hd):
        sl = slice(hd * h, hd * (h + 1))
        qh = q[:, sl]
        qh = (qh * _rms_scale(qh) * gq_ref[...] * (hd ** -0.5)).astype(BF16)
        s = _dot_nt(qh, k_ref[0, :, sl])
        e = jnp.exp(s - jnp.max(s, axis=-1, keepdims=True))
        p = (e * (1.0 / jnp.sum(e, axis=-1, keepdims=True))).astype(BF16)
        outs.append(_dot(p, v_ref[0, :, sl]).astype(BF16))
    o = jnp.concatenate(outs, axis=1)
    h_ref[0] = h1 + _dot(o, wxo_ref[...])


def _xattn(x, oa, ob, w_out, g_x, w_xq, g_xq, kmem, vmem, w_xo, tm):
    b, s, d = x.shape
    m = kmem.shape[1]
    const = lambda bi, ti: (0, 0)
    tok = lambda w: pl.BlockSpec((1, tm, w), lambda bi, ti: (bi, ti, 0))
    memblk = pl.BlockSpec((1, m, d), lambda bi, ti: (bi, 0, 0))
    return pl.pallas_call(
        _xattn_body,
        grid=(b, s // tm),
        in_specs=[tok(d), tok(oa.shape[2]),
                  pl.BlockSpec((1, B_HEADS, tm, LANES), lambda bi, ti: (bi, 0, ti, 0)),
                  pl.BlockSpec(w_out.shape, const), pl.BlockSpec((1, d), const),
                  pl.BlockSpec(w_xq.shape, const), pl.BlockSpec(g_xq.shape, const),
                  memblk, memblk, pl.BlockSpec(w_xo.shape, const)],
        out_specs=tok(d),
        out_shape=jax.ShapeDtypeStruct((b, s, d), F32),
        compiler_params=pltpu.CompilerParams(
            dimension_semantics=("arbitrary", "arbitrary"), vmem_limit_bytes=VMEM_LIMIT),
        name="outproj_xattn",
    )(x, oa, ob, w_out, g_x, w_xq, g_xq, kmem, vmem, w_xo)


HALO = 8


def _ffn_body(h_ref, g_ref, win_ref, cw_ref, cb_ref, wo_ref, o_ref, a_ref):
    tm = h_ref.shape[1]
    dff = wo_ref.shape[0]

    @pl.when(pl.program_id(1) == 0)
    def _():
        a_ref[0:HALO, :] = jnp.zeros((HALO, a_ref.shape[1]), F32)

    h = h_ref[0]
    hn = (h * _rms_scale(h) * g_ref[...]).astype(BF16)
    a_ref[HALO:HALO + tm, :] = _dot(hn, win_ref[:, 0:dff])
    gate = _dot(hn, win_ref[:, dff:2 * dff])
    conv = cb_ref[...]
    for j in range(CONV_W):
        off = HALO - (CONV_W - 1) + j
        conv = conv + a_ref[off:off + tm, :] * cw_ref[j:j + 1, :]
    a_ref[0:HALO, :] = a_ref[tm:tm + HALO, :]
    u = (jax.nn.gelu(conv) * gate).astype(BF16)
    o_ref[0] = h + _dot(u, wo_ref[...])


def _ffn(h, g_ffn, w_in, conv_w, conv_b, w_o, tm):
    b, s, d = h.shape
    dff = w_o.shape[0]
    const = lambda bi, ti: (0, 0)
    tok = pl.BlockSpec((1, tm, d), lambda bi, ti: (bi, ti, 0))
    return pl.pallas_call(
        _ffn_body,
        grid=(b, s // tm),
        in_specs=[tok, pl.BlockSpec((1, d), const), _const_spec(w_in.shape),
                  pl.BlockSpec(conv_w.shape, const),
                  pl.BlockSpec((1, dff), const), _const_spec(w_o.shape)],
        out_specs=tok,
        out_shape=jax.ShapeDtypeStruct((b, s, d), F32),
        scratch_shapes=[pltpu.VMEM((tm + HALO, dff), F32)],
        compiler_params=pltpu.CompilerParams(
            dimension_semantics=("arbitrary", "arbitrary"), vmem_limit_bytes=VMEM_LIMIT),
        name="conv_glu",
    )(h, g_ffn, w_in, conv_w, conv_b, w_o)


def _rearranged_w_in(w_in):
    sizes = (A_HEADS * HEAD_DIM, HEAD_DIM, HEAD_DIM, IDX_HEADS * HEAD_DIM, HEAD_DIM, IDX_HEADS,
             2 * B_HEADS * HEAD_DIM, 2 * B_HEADS * HEAD_DIM, B_HEADS * 2 * HEAD_DIM)
    offs = [0]
    for sz in sizes:
        offs.append(offs[-1] + sz)
    q_a, k_a, v_a, q_i, k_i, w_i, q_b, k_b, v_b = [w_in[:, offs[i]:offs[i + 1]] for i in range(9)]
    pad = jnp.zeros((w_in.shape[0], HEAD_DIM - IDX_HEADS), w_in.dtype)
    w_all = jnp.concatenate([q_a, q_i, k_a, k_a, k_i, k_i, v_a, w_i, pad, q_b, k_b, v_b], axis=1)
    assert w_all.shape[1] == _C_END
    return w_all.astype(BF16)


def kernel(x, mem, positions, g_mix, w_in, g_qa, g_ka, g_qb, g_kb, lam_q1, lam_k1, lam_q2, lam_k2,
           w_out, g_xattn, g_mem, w_xq, w_xk, w_xv, w_xo, g_xq, g_xk, g_ffn, w_ffn_in, conv_w, conv_b,
           w_ffn_out):
    b, s, d = x.shape
    depth = g_mix.shape[0]
    topk = min(TOPK_MAX, s // 4)
    tm = min(512, s)
    tm_in = min(1024, s)

    inv_freq = 1.0 / (ROPE_THETA ** (jnp.arange(0, HEAD_DIM, 2, dtype=F32) / HEAD_DIM))
    pos = positions.reshape(b * s // tm_in, 4, tm_in // 4).transpose(0, 2, 1).reshape(b * s // 4, 4)
    ang = (pos.astype(F32)[:, :, None] * inv_freq).reshape(b * s // 4, LANES)
    cos_d, sin_d = jnp.cos(ang), jnp.sin(ang)
    sign = jnp.tile(jnp.repeat(jnp.array([-1.0, 1.0], F32), HEAD_DIM // 2), 512 // HEAD_DIM)
    blk = jnp.arange(MXU_DIM) // HEAD_DIM
    bd = jnp.where(blk[:, None] == blk[None, :], 1.0 / HEAD_DIM, 0.0).astype(BF16)

    h = x
    for l in range(depth):
        lambda_init = 0.8 - 0.6 * math.exp(-0.3 * l)
        gains = jnp.stack([jnp.tile(g, 512 // HEAD_DIM) for g in (g_qa[l], g_ka[l], g_qb[l], g_kb[l])])
        gains = jnp.concatenate([gains, sign[None, :], jnp.ones((3, 512), F32)], axis=0)
        qa, qi, ks, wi, qb, kb, vb = _inproj(
            h.reshape(b * s, d), g_mix[l][None, :], _rearranged_w_in(w_in[l]), cos_d, sin_d, bd, gains,
            tm_in, s)
        r3 = lambda t: t.reshape(b, s, t.shape[-1])
        out_a = _dsa(r3(qa), r3(qi), r3(wi), r3(ks), topk, n_bisect=20)
        out_b = _diff(qb, kb, vb, lam_q1[l][None, :], lam_k1[l][None, :],
                      lam_q2[l][None, :], lam_k2[l][None, :], lambda_init)
        kmem, vmem = _memkv(mem, g_mem[l][None, :], w_xk[l].astype(BF16), w_xv[l].astype(BF16),
                            g_xk[l][None, :])
        h = _xattn(h, out_a, out_b, w_out[l].astype(BF16), g_xattn[l][None, :], w_xq[l].astype(BF16),
                   g_xq[l][None, :], kmem, vmem, w_xo[l].astype(BF16), min(1024, s))
        dff = w_ffn_out.shape[1]
        cw = jnp.concatenate([conv_w[l], jnp.zeros((8 - CONV_W, dff), F32)], axis=0)
        h = _ffn(h, g_ffn[l][None, :], w_ffn_in[l].astype(BF16), cw, conv_b[l][None, :],
                 w_ffn_out[l].astype(BF16), tm)
    return h
```

```python
import functools
import math

import jax
import jax.numpy as jnp
from jax import lax
from jax.experimental import pallas as pl
from jax.experimental.pallas import tpu as pltpu

F32 = jnp.float32
BF16 = jnp.bfloat16

EPS = 1e-6
ROPE_THETA = 10000.0
HEAD_DIM = 64
A_HEADS = 8
IDX_HEADS = 4
TOPK_MAX = 256
B_HEADS = 4
X_HEADS = 4
CONV_W = 3
LANES = 128
MXU_DIM = 256
DSA_QBLK = 128
DSA_PAIR = 2
DIFF_QBLK = 256
CAUSAL_STEP = MXU_DIM
VMEM_LIMIT = 56 * 1024 * 1024
LOG2E = 1.4426950408889634
SHIFT_MARGIN = 1.02
DENOM_FLOOR = 2.0 ** -40

_C_QA = 0
_C_QI = 512
_C_KS = 768
_C_QB = 1152
_C_KB = 1664
_C_END = 2176


def _dot(a, b):
    return jnp.dot(a, b, preferred_element_type=F32)


def _dot_nt(a, b):
    return lax.dot_general(a, b, (((1,), (1,)), ((), ())), preferred_element_type=F32)


def _rms_scale(x):
    return lax.rsqrt(jnp.mean(x * x, axis=-1, keepdims=True) + EPS)


def _const_spec(shape):
    zeros = (0,) * len(shape)
    return pl.BlockSpec(shape, lambda *_: zeros, pipeline_mode=pl.Buffered(1))


def _lane_tile(t, width):
    reps = width // t.shape[1]
    return t if reps == 1 else jnp.concatenate([t] * reps, axis=1)


def _col_reduce(x, reduce_fn):
    rows, cols = x.shape
    slab = 8 * max(1, 8 * LANES // cols)
    if rows % slab or rows == slab:
        return reduce_fn(x, axis=0, keepdims=True)
    part = reduce_fn(x.reshape(rows // slab, slab, cols), axis=0)
    return reduce_fn(part, axis=0, keepdims=True)


def _inproj_body(x_ref, gmix_ref, w_ref, wvt_ref, cos_ref, sin_ref, bd_ref, gains_ref,
                 qa_ref, qi_ref, ks_ref, wi_ref, qb_ref, kb_ref, vbt_ref, vat_ref):
    x = x_ref[...]
    hn = (x * _rms_scale(x) * gmix_ref[...]).astype(BF16)

    def spread(t):
        turned = [t] + [pltpu.roll(t, 32 * k, 1) for k in range(1, 4)]
        group = lax.broadcasted_iota(jnp.int32, t.shape, 1) >> 5
        parts = []
        for j in range(4):
            d = (group - j) & 3
            parts.append(jnp.where(d == 0, turned[0], jnp.where(d == 1, turned[1],
                                                               jnp.where(d == 2, turned[2], turned[3]))))
        return jnp.concatenate(parts, axis=0)

    cos = spread(cos_ref[...])
    sin = spread(sin_ref[...]) * gains_ref[4:5, 0:LANES]
    bd = bd_ref[...]

    def proj(c0, width):
        return _dot(hn, w_ref[:, c0:c0 + width])

    def group_rms_scale(p):
        sq = (p * p).astype(BF16)
        outs = []
        for j in range(p.shape[1] // MXU_DIM):
            outs.append(_dot(sq[:, MXU_DIM * j:MXU_DIM * (j + 1)], bd))
        ms = outs[0] if len(outs) == 1 else jnp.concatenate(outs, axis=1)
        return lax.rsqrt(ms + EPS)

    def rope(y):
        width = y.shape[1]
        lane = lax.broadcasted_iota(jnp.int32, y.shape, 1)
        first_half = (lane & (HEAD_DIM - 1)) < (HEAD_DIM // 2)
        swapped = jnp.where(first_half, pltpu.roll(y, width - HEAD_DIM // 2, 1),
                            pltpu.roll(y, HEAD_DIM // 2, 1))
        return y * _lane_tile(cos, width) + swapped * _lane_tile(sin, width)

    sm_scale = HEAD_DIM ** -0.5 * LOG2E

    p = proj(_C_QA, 512)
    qa_ref[...] = (rope(p * group_rms_scale(p) * gains_ref[0:1, :]) * sm_scale).astype(BF16)

    p = proj(_C_QI, 256)
    qi_ref[...] = rope(p).astype(BF16)

    p = proj(_C_KS, 384)
    p01 = p[:, 0:256]
    lane = lax.broadcasted_iota(jnp.int32, p01.shape, 1)
    y01 = jnp.where(lane < LANES, p01 * group_rms_scale(p01) * gains_ref[1:2, 0:256], p01)
    y01 = rope(y01)
    p2 = p[:, 256:384]
    lane = lax.broadcasted_iota(jnp.int32, p2.shape, 1)
    p2 = jnp.where(lane < HEAD_DIM, p2, p2 * (IDX_HEADS ** -0.5 * HEAD_DIM ** -0.5))
    ks_ref[:, 0:256] = y01.astype(BF16)
    ks_ref[:, 256:384] = p2.astype(BF16)
    wi_ref[...] = p2

    def store_heads(ref, y):
        for h in range(B_HEADS):
            ref[0, h] = y[:, LANES * h:LANES * (h + 1)].astype(BF16)

    p = proj(_C_QB, 512)
    store_heads(qb_ref, rope(p * group_rms_scale(p) * gains_ref[2:3, :]) * sm_scale)

    p = proj(_C_KB, 512)
    store_heads(kb_ref, rope(p * group_rms_scale(p) * gains_ref[3:4, :]))

    vt = _dot_nt(wvt_ref[...], hn)
    for h in range(B_HEADS):
        vbt_ref[0, h] = vt[LANES * h:LANES * (h + 1), :].astype(BF16)
    vat_ref[0] = vt[LANES * B_HEADS:LANES * (B_HEADS + 1), :].astype(BF16)


def _inproj(x2, gmix, w_all, w_vt, cos128, sin128, bd, gains, tm, seq):
    n, d = x2.shape
    row = lambda i: (i, 0)
    const = lambda i: (0, 0)
    outs = [(512, BF16), (256, BF16), (384, BF16), (LANES, F32)]
    tiles = seq // tm
    head_spec = pl.BlockSpec((1, B_HEADS, tm, LANES), lambda i: (i // tiles, 0, i % tiles, 0))
    head_shape = jax.ShapeDtypeStruct((n // seq, B_HEADS, seq, LANES), BF16)
    vbt_spec = pl.BlockSpec((1, B_HEADS, LANES, tm), lambda i: (i // tiles, 0, 0, i % tiles))
    vbt_shape = jax.ShapeDtypeStruct((n // seq, B_HEADS, LANES, seq), BF16)
    vat_spec = pl.BlockSpec((1, LANES, tm), lambda i: (i // tiles, 0, i % tiles))
    vat_shape = jax.ShapeDtypeStruct((n // seq, LANES, seq), BF16)
    return pl.pallas_call(
        _inproj_body,
        grid=(n // tm,),
        in_specs=[
            pl.BlockSpec((tm, d), row),
            pl.BlockSpec((1, d), const),
            pl.BlockSpec(w_all.shape, const),
            pl.BlockSpec(w_vt.shape, const),
            pl.BlockSpec((tm // 4, LANES), row),
            pl.BlockSpec((tm // 4, LANES), row),
            pl.BlockSpec(bd.shape, const),
            pl.BlockSpec(gains.shape, const),
        ],
        out_specs=([pl.BlockSpec((tm, w), row) for w, _ in outs]
                   + [head_spec, head_spec, vbt_spec, vat_spec]),
        out_shape=([jax.ShapeDtypeStruct((n, w), dt) for w, dt in outs]
                   + [head_shape, head_shape, vbt_shape, vat_shape]),
        compiler_params=pltpu.CompilerParams(
            dimension_semantics=("arbitrary",), vmem_limit_bytes=VMEM_LIMIT),
        name="inproj",
    )(x2, gmix, w_all, w_vt, cos128, sin128, bd, gains)


def _group_reduce(x, reduce_fn):
    g, rows, cols = x.shape
    slab = 8 * max(1, 8 * LANES // cols)
    if rows % slab or rows == slab:
        return reduce_fn(x, axis=1, keepdims=True)
    combine = {jnp.sum: jnp.add, jnp.max: jnp.maximum, jnp.min: jnp.minimum}[reduce_fn]
    part = x[:, 0:slab]
    for i in range(1, rows // slab):
        part = combine(part, x[:, i * slab:(i + 1) * slab])
    return reduce_fn(part, axis=1, keepdims=True)


def _dsa_queries(qa_ref, rows):
    qa = qa_ref[0, rows, :]
    lane = lax.broadcasted_iota(jnp.int32, (DSA_QBLK, LANES), 1)
    rows = []
    for h in range(A_HEADS):
        slab = qa[:, LANES * (h // 2):LANES * (h // 2 + 1)]
        keep = (lane >= HEAD_DIM) if h % 2 else (lane < HEAD_DIM)
        rows.append(jnp.where(keep, slab, jnp.zeros_like(slab)))
    return jnp.concatenate(rows, axis=0)


def _dsa_write(ot, l, o_ref, ot_ref, rows):
    tq = DSA_QBLK
    ot = ot * (1.0 / l)
    for h in range(A_HEADS):
        ot_ref[HEAD_DIM * h:HEAD_DIM * (h + 1), :] = ot[:, tq * h:tq * (h + 1)]
    o_ref[0, rows, :] = ot_ref[...].T.astype(BF16)


def _dsa_rows(block):
    if isinstance(block, int):
        return pl.ds(block * DSA_QBLK, DSA_QBLK)
    return pl.ds(pl.multiple_of(block * DSA_QBLK, DSA_QBLK), DSA_QBLK)


def _dsa_keys(klen, pair, qa_ref, qi_ref, wi_ref, ks_ref, o_ref, vt_ref, sc_ref, bias_ref, ot_ref,
              kmax_ref, flag_ref, topk, n_bisect):
    tq = DSA_QBLK
    seq = ks_ref.shape[1]
    tail = klen - CAUSAL_STEP
    neg_inf = -jnp.inf

    lane_q = lax.broadcasted_iota(jnp.int32, (tq, LANES), 1)
    first_head = lane_q < HEAD_DIM

    def head_rows(slab, odd):
        return jnp.where(first_head != odd, slab, jnp.zeros_like(slab))

    shape_t = (DSA_PAIR, CAUSAL_STEP, tq)
    qpos_t = ((pair * DSA_PAIR + lax.broadcasted_iota(jnp.int32, shape_t, 0)) * tq
              + lax.broadcasted_iota(jnp.int32, shape_t, 2))
    causal_t = tail + lax.broadcasted_iota(jnp.int32, shape_t, 1) <= qpos_t

    def indexer(blk, carry):
        rows = _dsa_rows(blk)
        qi = qi_ref[0, rows, :]
        qi_stack = jnp.concatenate(
            [head_rows(qi[:, LANES * (h // 2):LANES * (h // 2 + 1)], bool(h % 2)) for h in range(IDX_HEADS)],
            axis=0)
        lg = _dot_nt(ks_ref[0, 0:klen, 128:256], qi_stack)
        w_t = wi_ref[0, rows, :].T
        sc = None
        for h in range(IDX_HEADS):
            term = jnp.maximum(lg[:, tq * h:tq * (h + 1)], 0.0) * w_t[HEAD_DIM + h:HEAD_DIM + h + 1, :]
            sc = term if sc is None else sc + term
        sc_ref[blk, 0:klen, :] = sc
        return carry

    lax.fori_loop(0, DSA_PAIR, indexer, 0)
    sc_ref[:, tail:klen, :] = jnp.where(causal_t, sc_ref[:, tail:klen, :], neg_inf)

    kf = float(topk)
    search = qpos_t[:, 0:1, :] >= topk

    def scores():
        return sc_ref[:, 0:klen, :]

    def count(pred):
        return _group_reduce(jnp.where(pred, 1.0, 0.0), jnp.sum)

    hi0 = _group_reduce(scores(), jnp.max)
    lo0 = _group_reduce(jnp.where(causal_t, sc_ref[:, tail:klen, :], jnp.inf), jnp.min)
    if tail:
        lo0 = jnp.minimum(lo0, _group_reduce(sc_ref[:, 0:tail, :], jnp.min))

    def bisect(_, carry):
        lo, hi = carry
        mid = 0.5 * (lo + hi)
        ge = count(scores() >= mid) >= kf
        return jnp.where(ge, mid, lo), jnp.where(ge, hi, mid)

    lo, _ = lax.fori_loop(0, n_bisect, bisect, (lo0, hi0))

    def too_low(n_gt):
        return jnp.max(jnp.where(jnp.logical_and(search, n_gt >= kf), 1.0, 0.0))

    def climb(carry):
        thr, n_gt, _ = carry
        s = scores()
        nxt = _group_reduce(jnp.where(s > thr, s, jnp.inf), jnp.min)
        thr = jnp.where(jnp.logical_and(search, n_gt >= kf), nxt, thr)
        n_gt = count(scores() > thr)
        return thr, n_gt, too_low(n_gt)

    s = scores()
    thr0 = _group_reduce(jnp.where(s >= lo, s, jnp.inf), jnp.min)
    n_gt0 = count(s > thr0)
    thr, n_gt, _ = lax.while_loop(lambda c: c[2] > 0.0, climb, (thr0, n_gt0, too_low(n_gt0)))

    need = kf - n_gt
    open_row = jnp.where(search, neg_inf, 0.0)
    n_blk = klen // MXU_DIM
    s = scores()
    tie = jnp.where(s == thr, 1.0, 0.0).astype(BF16)
    tie_cat = jnp.concatenate([tie[g, MXU_DIM * j:MXU_DIM * (j + 1)]
                               for g in range(DSA_PAIR) for j in range(n_blk)], axis=1)
    tri = jnp.where(lax.broadcasted_iota(jnp.int32, (MXU_DIM, MXU_DIM), 0)
                    >= lax.broadcasted_iota(jnp.int32, (MXU_DIM, MXU_DIM), 1), 1.0, 0.0).astype(BF16)
    prefix = _dot(tri, tie_cat)
    for g in range(DSA_PAIR):
        before = jnp.zeros((1, tq), F32)
        for j in range(n_blk):
            rows = slice(MXU_DIM * j, MXU_DIM * (j + 1))
            col = (g * n_blk + j) * tq
            rank = prefix[:, col:col + tq] + before
            before = before + prefix[MXU_DIM - 1:MXU_DIM, col:col + tq]
            sj = s[g, rows]
            admitted = jnp.where(sj == thr[g], jnp.where(rank <= need[g], 0.0, neg_inf), neg_inf)
            bias = jnp.maximum(jnp.where(sj > thr[g], 0.0, admitted), open_row[g])
            if j == n_blk - 1:
                bias = jnp.where(causal_t[g], bias, neg_inf)
            bias_ref[g, rows, :] = bias

    def attend(blk, underflow):
        rows = _dsa_rows(blk)
        q_all = _dsa_queries(qa_ref, rows)
        qsq = q_all.astype(F32)
        qn2 = _dot_nt(jnp.ones((8, LANES), BF16), (qsq * qsq).astype(BF16))[0:1, :]
        shift = jnp.sqrt(qn2) * (_lane_tile(kmax_ref[...], A_HEADS * tq) * SHIFT_MARGIN)
        bias = bias_ref[blk, 0:klen, :]
        st = _dot_nt(ks_ref[0, 0:klen, 0:128], q_all) + jnp.concatenate([bias] * A_HEADS, axis=1) - shift
        e = jnp.exp2(st)
        l = _col_reduce(e, jnp.sum)
        _dsa_write(_dot(vt_ref[0, 0:HEAD_DIM, 0:klen], e.astype(BF16)), l, o_ref, ot_ref, rows)
        return jnp.maximum(underflow, jnp.where(jnp.min(l) >= DENOM_FLOOR, 0, 1))

    underflow = 0
    for blk in range(DSA_PAIR):
        underflow = attend(blk, underflow)

    @pl.when(underflow != 0)
    def _():
        if klen < seq:
            bias_ref[:, klen:seq, :] = jnp.full((DSA_PAIR, seq - klen, tq), neg_inf, F32)
        flag_ref[0] = 1


def _dsa_body(qa_ref, qi_ref, wi_ref, ks_ref, vt_ref, kmax_ref, o_ref, sc_ref, bias_ref, ot_ref,
              flag_ref, *, topk, n_bisect):
    pair = pl.program_id(1)
    seq = ks_ref.shape[1]
    flag_ref[0] = 0

    for c in range(seq // CAUSAL_STEP):
        @pl.when(pair == c)
        def _(c=c):
            _dsa_keys(CAUSAL_STEP * (c + 1), pair, qa_ref, qi_ref, wi_ref, ks_ref, o_ref, vt_ref,
                      sc_ref, bias_ref, ot_ref, kmax_ref, flag_ref, topk, n_bisect)

    @pl.when(flag_ref[0] != 0)
    def _():
        def redo(blk, carry):
            rows = _dsa_rows(blk)
            bias = bias_ref[blk]
            st = (_dot_nt(ks_ref[0, :, 0:128], _dsa_queries(qa_ref, rows))
                  + jnp.concatenate([bias] * A_HEADS, axis=1))
            e = jnp.exp2(st - _col_reduce(st, jnp.max))
            _dsa_write(_dot(vt_ref[0, 0:HEAD_DIM, :], e.astype(BF16)), _col_reduce(e, jnp.sum), o_ref, ot_ref,
                       rows)
            return carry

        lax.fori_loop(0, DSA_PAIR, redo, 0)


def _dsa(qa, qi, wi, ks, vat, kmax, topk, n_bisect):
    b, s, _ = qa.shape
    tq = DSA_QBLK
    step = DSA_PAIR * tq
    assert step == CAUSAL_STEP and s % step == 0
    blk = lambda bi, qi_: (bi, qi_, 0)
    return pl.pallas_call(
        functools.partial(_dsa_body, topk=topk, n_bisect=n_bisect),
        grid=(b, s // step),
        in_specs=[
            pl.BlockSpec((1, step, 512), blk),
            pl.BlockSpec((1, step, 256), blk),
            pl.BlockSpec((1, step, LANES), blk),
            pl.BlockSpec((1, s, 384), lambda bi, qi_: (bi, 0, 0)),
            pl.BlockSpec((1, LANES, s), lambda bi, qi_: (bi, 0, 0)),
            pl.BlockSpec((1, LANES), lambda bi, qi_: (0, 0)),
        ],
        out_specs=pl.BlockSpec((1, step, 512), blk),
        out_shape=jax.ShapeDtypeStruct((b, s, 512), BF16),
        scratch_shapes=[
            pltpu.VMEM((DSA_PAIR, s, tq), F32),
            pltpu.VMEM((DSA_PAIR, s, tq), F32),
            pltpu.VMEM((A_HEADS * HEAD_DIM, tq), F32),
            pltpu.SMEM((1,), jnp.int32),
        ],
        compiler_params=pltpu.CompilerParams(
            dimension_semantics=("arbitrary", "arbitrary"), vmem_limit_bytes=VMEM_LIMIT),
        name="dsa_attention",
    )(qa, qi, wi, ks, vat, kmax)


def _diff_body(q_ref, k_ref, vt_ref, kmax_ref, lq1_ref, lk1_ref, lq2_ref, lk2_ref, o_ref,
               flag_ref, *, lambda_init):
    qblk = pl.program_id(1)
    seq = k_ref.shape[2]
    tq = q_ref.shape[2]

    lam =(jnp.exp(jnp.sum(lq1_ref[...] * lk1_ref[...], axis=1, keepdims=True))
           - jnp.exp(jnp.sum(lq2_ref[...] * lk2_ref[...], axis=1, keepdims=True)) + lambda_init)
    lane = lax.broadcasted_iota(jnp.int32, (tq, LANES), 1)
    diag = (lax.broadcasted_iota(jnp.int32, (tq, 2 * tq), 0)
            <= (lax.broadcasted_iota(jnp.int32, (tq, 2 * tq), 1) & (tq - 1)))

    def queries(h):
        q = q_ref[0, h]
        zero = jnp.zeros_like(q)
        return jnp.concatenate([jnp.where(lane < HEAD_DIM, q, zero), jnp.where(lane >= HEAD_DIM, q, zero)],
                               axis=0)

    def write(h, acc, l):
        acc = acc * (1.0 / l)
        ot = acc[:, 0:tq] - lam * acc[:, tq:2 * tq]
        ot = ot * lax.rsqrt(jnp.mean(ot * ot, axis=0, keepdims=True) + EPS) * (1.0 - lambda_init)
        o_ref[0, h] = ot.T.astype(BF16)

    def attend(klen, h, underflow):
        tail = klen - tq
        q2 = queries(h)
        qsq = q2.astype(F32)
        qn2 = _dot_nt(jnp.ones((8, LANES), BF16), (qsq * qsq).astype(BF16))[0:1, :]
        shift = jnp.sqrt(qn2) * (_lane_tile(kmax_ref[...], 2 * tq) * SHIFT_MARGIN)
        st = _dot_nt(k_ref[0, h, 0:klen, :], q2) - shift
        e = jnp.exp2(jnp.where(diag, st[tail:klen], -jnp.inf))
        l = _col_reduce(e, jnp.sum)
        acc = _dot(vt_ref[0, h, :, tail:klen], e.astype(BF16))
        if tail:
            e = jnp.exp2(st[0:tail])
            l = l + _col_reduce(e, jnp.sum)
            acc = acc + _dot(vt_ref[0, h, :, 0:tail], e.astype(BF16))
        write(h, acc, l)
        return jnp.maximum(underflow, jnp.where(jnp.min(l) >= DENOM_FLOOR, 0, 1))

    flag_ref[0] = 0
    for c in range(seq // tq):
        @pl.when(qblk == c)
        def _(c=c):
            underflow = 0
            for h in range(B_HEADS):
                underflow = attend(tq * (c + 1), h, underflow)
            flag_ref[0] = underflow

    @pl.when(flag_ref[0] != 0)
    def _():
        kpos = lax.broadcasted_iota(jnp.int32, (seq, 2 * tq), 0)
        qpos = qblk * tq + (lax.broadcasted_iota(jnp.int32, (seq, 2 * tq), 1) & (tq - 1))

        def redo(h, carry):
            st = jnp.where(kpos <= qpos, _dot_nt(k_ref[0, h], queries(h)), -jnp.inf)
            e = jnp.exp2(st - _col_reduce(st, jnp.max))
            write(h, _dot(vt_ref[0, h], e.astype(BF16)), _col_reduce(e, jnp.sum))
            return carry

        lax.fori_loop(0, B_HEADS, redo, 0)


def _diff(qb, kb, vbt, kmax, lq1, lk1, lq2, lk2, lambda_init):
    b, _, s, _ = qb.shape
    tq = min(DIFF_QBLK, s)
    vec = pl.BlockSpec((1, HEAD_DIM), lambda bi, qi_: (0, 0))
    keys = pl.BlockSpec((1, B_HEADS, s, LANES), lambda bi, qi_: (bi, 0, 0, 0))
    vals = pl.BlockSpec((1, B_HEADS, LANES, s), lambda bi, qi_: (bi, 0, 0, 0))
    blk = pl.BlockSpec((1, B_HEADS, tq, LANES), lambda bi, qi_: (bi, 0, qi_, 0))
    return pl.pallas_call(
        functools.partial(_diff_body, lambda_init=lambda_init),
        grid=(b, s // tq),
        in_specs=[blk, keys, vals, pl.BlockSpec((1, LANES), lambda bi, qi_: (0, 0)), vec, vec, vec, vec],
        out_specs=blk,
        out_shape=jax.ShapeDtypeStruct((b, B_HEADS, s, LANES), BF16),
        scratch_shapes=[pltpu.SMEM((1,), jnp.int32)],
        compiler_params=pltpu.CompilerParams(
            dimension_semantics=("arbitrary", "arbitrary"), vmem_limit_bytes=VMEM_LIMIT),
        name="diff_attention",
    )(qb, kb, vbt, kmax, lq1, lk1, lq2, lk2)


def _memkv_body(mem_ref, g_ref, wk_ref, wv_ref, gk_ref, k_ref, v_ref):
    mem = mem_ref[0]
    memn = (mem * _rms_scale(mem) * g_ref[...]).astype(BF16)
    k = _dot(memn, wk_ref[...])
    hd = gk_ref.shape[1]
    for h in range(k.shape[1] // hd):
        kh = k[:, hd * h:hd * (h + 1)]
        k_ref[0, :, hd * h:hd * (h + 1)] = (kh * _rms_scale(kh) * gk_ref[...]).astype(BF16)
    v_ref[0] = _dot(memn, wv_ref[...]).astype(BF16)


def _memkv(mem, g_mem, w_xk, w_xv, g_xk):
    b, m, d = mem.shape
    const = lambda bi: (0, 0)
    blk = pl.BlockSpec((1, m, d), lambda bi: (bi, 0, 0))
    return pl.pallas_call(
        _memkv_body,
        grid=(b,),
        in_specs=[blk, pl.BlockSpec((1, d), const), pl.BlockSpec((d, d), const),
                  pl.BlockSpec((d, d), const), pl.BlockSpec(g_xk.shape, const)],
        out_specs=[blk, blk],
        out_shape=[jax.ShapeDtypeStruct((b, m, d), BF16)] * 2,
        compiler_params=pltpu.CompilerParams(
            dimension_semantics=("arbitrary",), vmem_limit_bytes=VMEM_LIMIT),
        name="mem_kv",
    )(mem, g_mem, w_xk, w_xv, g_xk)


def _xattn_body(x_ref, oa_ref, ob_ref, wo_ref, g_ref, wq_ref, gq_ref, k_ref, v_ref, wxo_ref, h_ref):
    half = oa_ref.shape[2]
    ob = jnp.concatenate([ob_ref[0, h] for h in range(B_HEADS)], axis=1)
    h1 = x_ref[0] + _dot(oa_ref[0], wo_ref[0:half, :]) + _dot(ob, wo_ref[half:2 * half, :])
    hn = (h1 * _rms_scale(h1) * g_ref[...]).astype(BF16)
    q = _dot(hn, wq_ref[...])
    hd = gq_ref.shape[1]
    outs = []
    for h in range(q.shape[1] // hd):
        sl = slice(hd * h, hd * (h + 1))
        qh = q[:, sl]
        qh = (qh * _rms_scale(qh) * gq_ref[...] * (hd ** -0.5)).astype(BF16)
        s = _dot_nt(qh, k_ref[0, :, sl])
        e = jnp.exp(s - jnp.max(s, axis=-1, keepdims=True))
        p = (e * (1.0 / jnp.sum(e, axis=-1, keepdims=True))).astype(BF16)
        outs.append(_dot(p, v_ref[0, :, sl]).astype(BF16))
    o = jnp.concatenate(outs, axis=1)
    h_ref[0] = h1 + _dot(o, wxo_ref[...])


def _xattn(x, oa, ob, w_out, g_x, w_xq, g_xq, kmem, vmem, w_xo, tm):
    b, s, d = x.shape
    m = kmem.shape[1]
    const = lambda bi, ti: (0, 0)
    tok = lambda w: pl.BlockSpec((1, tm, w), lambda bi, ti: (bi, ti, 0))
    memblk = pl.BlockSpec((1, m, d), lambda bi, ti: (bi, 0, 0))
    return pl.pallas_call(
        _xattn_body,
        grid=(b, s // tm),
        in_specs=[tok(d), tok(oa.shape[2]),
                  pl.BlockSpec((1, B_HEADS, tm, LANES), lambda bi, ti: (bi, 0, ti, 0)),
                  pl.BlockSpec(w_out.shape, const), pl.BlockSpec((1, d), const),
                  pl.BlockSpec(w_xq.shape, const), pl.BlockSpec(g_xq.shape, const),
                  memblk, memblk, pl.BlockSpec(w_xo.shape, const)],
        out_specs=tok(d),
        out_shape=jax.ShapeDtypeStruct((b, s, d), F32),
        compiler_params=pltpu.CompilerParams(
            dimension_semantics=("arbitrary", "arbitrary"), vmem_limit_bytes=VMEM_LIMIT),
        name="outproj_xattn",
    )(x, oa, ob, w_out, g_x, w_xq, g_xq, kmem, vmem, w_xo)


HALO = 8


def _ffn_body(h_ref, g_ref, win_ref, cw_ref, cb_ref, wo_ref, o_ref, a_ref):
    tm = h_ref.shape[1]
    dff = wo_ref.shape[0]

    @pl.when(pl.program_id(1) == 0)
    def _():
        a_ref[0:HALO, :] = jnp.zeros((HALO, a_ref.shape[1]), F32)

    h = h_ref[0]
    hn = (h * _rms_scale(h) * g_ref[...]).astype(BF16)
    a_ref[HALO:HALO + tm, :] = _dot(hn, win_ref[:, 0:dff])
    gate = _dot(hn, win_ref[:, dff:2 * dff])
    conv = cb_ref[...]
    for j in range(CONV_W):
        off = HALO - (CONV_W - 1) + j
        conv = conv + a_ref[off:off + tm, :] * cw_ref[j:j + 1, :]
    a_ref[0:HALO, :] = a_ref[tm:tm + HALO, :]
    u = (jax.nn.gelu(conv) * gate).astype(BF16)
    o_ref[0] = h + _dot(u, wo_ref[...])


def _ffn(h, g_ffn, w_in, conv_w, conv_b, w_o, tm):
    b, s, d = h.shape
    dff = w_o.shape[0]
    const = lambda bi, ti: (0, 0)
    tok = pl.BlockSpec((1, tm, d), lambda bi, ti: (bi, ti, 0))
    return pl.pallas_call(
        _ffn_body,
        grid=(b, s // tm),
        in_specs=[tok, pl.BlockSpec((1, d), const), _const_spec(w_in.shape),
                  pl.BlockSpec(conv_w.shape, const),
                  pl.BlockSpec((1, dff), const), _const_spec(w_o.shape)],
        out_specs=tok,
        out_shape=jax.ShapeDtypeStruct((b, s, d), F32),
        scratch_shapes=[pltpu.VMEM((tm + HALO, dff), F32)],
        compiler_params=pltpu.CompilerParams(
            dimension_semantics=("arbitrary", "arbitrary"), vmem_limit_bytes=VMEM_LIMIT),
        name="conv_glu",
    )(h, g_ffn, w_in, conv_w, conv_b, w_o)


def _rearranged_w_in(w_in):
    sizes = (A_HEADS * HEAD_DIM, HEAD_DIM, HEAD_DIM, IDX_HEADS * HEAD_DIM, HEAD_DIM, IDX_HEADS,
             2 * B_HEADS * HEAD_DIM, 2 * B_HEADS * HEAD_DIM, B_HEADS * 2 * HEAD_DIM)
    offs = [0]
    for sz in sizes:
        offs.append(offs[-1] + sz)
    q_a, k_a, v_a, q_i, k_i, w_i, q_b, k_b, v_b = [w_in[:, offs[i]:offs[i + 1]] for i in range(9)]
    pad = jnp.zeros((w_in.shape[0], HEAD_DIM - IDX_HEADS), w_in.dtype)
    w_all = jnp.concatenate([q_a, q_i, k_a, k_a, k_i, k_i, v_a, w_i, pad, q_b, k_b], axis=1)
    assert w_all.shape[1] == _C_END
    w_vt = jnp.concatenate([v_b, v_a, jnp.zeros((w_in.shape[0], HEAD_DIM), w_in.dtype)], axis=1).T
    return w_all.astype(BF16), w_vt.astype(BF16)


def kernel(x, mem, positions, g_mix, w_in, g_qa, g_ka, g_qb, g_kb, lam_q1, lam_k1, lam_q2, lam_k2,
           w_out, g_xattn, g_mem, w_xq, w_xk, w_xv, w_xo, g_xq, g_xk, g_ffn, w_ffn_in, conv_w, conv_b,
           w_ffn_out):
    b, s, d = x.shape
    depth = g_mix.shape[0]
    topk = min(TOPK_MAX, s // 4)
    tm = min(512, s)
    tm_in = min(1024, s)

    inv_freq = 1.0 / (ROPE_THETA ** (jnp.arange(0, HEAD_DIM, 2, dtype=F32) / HEAD_DIM))
    pos = positions.reshape(b * s // tm_in, 4, tm_in // 4).transpose(0, 2, 1).reshape(b * s // 4, 4)
    ang = (pos.astype(F32)[:, :, None] * inv_freq).reshape(b * s // 4, LANES)
    cos_d, sin_d = jnp.cos(ang), jnp.sin(ang)
    sign = jnp.tile(jnp.repeat(jnp.array([-1.0, 1.0], F32), HEAD_DIM // 2), 512 // HEAD_DIM)
    blk = jnp.arange(MXU_DIM) // HEAD_DIM
    bd = jnp.where(blk[:, None] == blk[None, :], 1.0 / HEAD_DIM, 0.0).astype(BF16)

    h = x
    for l in range(depth):
        lambda_init = 0.8 - 0.6 * math.exp(-0.3 * l)
        gains = jnp.stack([jnp.tile(g, 512 // HEAD_DIM) for g in (g_qa[l], g_ka[l], g_qb[l], g_kb[l])])
        gains = jnp.concatenate([gains, sign[None, :], jnp.ones((3, 512), F32)], axis=0)
        w_all, w_vt = _rearranged_w_in(w_in[l])
        qa, qi, ks, wi, qb, kb, vbt, vat = _inproj(
            h.reshape(b * s, d), g_mix[l][None, :], w_all, w_vt, cos_d, sin_d, bd, gains, tm_in, s)
        r3 = lambda t: t.reshape(b, s, t.shape[-1])
        key_bound = lambda g: jnp.full((1, LANES), HEAD_DIM ** 0.5, F32) * jnp.max(jnp.abs(g))
        out_a = _dsa(r3(qa), r3(qi), r3(wi), r3(ks), vat, key_bound(g_ka[l]), topk, n_bisect=20)
        out_b = _diff(qb, kb, vbt, key_bound(g_kb[l]), lam_q1[l][None, :], lam_k1[l][None, :],
                      lam_q2[l][None, :], lam_k2[l][None, :], lambda_init)
        kmem, vmem = _memkv(mem, g_mem[l][None, :], w_xk[l].astype(BF16), w_xv[l].astype(BF16),
                            g_xk[l][None, :])
        h = _xattn(h, out_a, out_b, w_out[l].astype(BF16), g_xattn[l][None, :], w_xq[l].astype(BF16),
                   g_xq[l][None, :], kmem, vmem, w_xo[l].astype(BF16), min(1024, s))
        dff = w_ffn_out.shape[1]
        cw = jnp.concatenate([conv_w[l], jnp.zeros((8 - CONV_W, dff), F32)], axis=0)
        h = _ffn(h, g_ffn[l][None, :], w_ffn_in[l].astype(BF16), cw, conv_b[l][None, :],
                 w_ffn_out[l].astype(BF16), tm)
    return h
```

```python
import functools
import math

import jax
import jax.numpy as jnp
from jax import lax
from jax.experimental import pallas as pl
from jax.experimental.pallas import tpu as pltpu

F32 = jnp.float32
BF16 = jnp.bfloat16

EPS = 1e-6
ROPE_THETA = 10000.0
HEAD_DIM = 64
A_HEADS = 8
IDX_HEADS = 4
TOPK_MAX = 256
B_HEADS = 4
X_HEADS = 4
CONV_W = 3
LANES = 128
MXU_DIM = 256
DSA_QBLK = 128
DSA_PAIR = 2
DIFF_QBLK = 256
CAUSAL_STEP = MXU_DIM
VMEM_LIMIT = 56 * 1024 * 1024
LOG2E = 1.4426950408889634
SHIFT_MARGIN = 1.02
DENOM_FLOOR = 2.0 ** -40

_C_QA = 0
_C_QI = 512
_C_KS = 768
_C_QB = 1152
_C_KB = 1664
_C_END = 2176


def _dot(a, b):
    return jnp.dot(a, b, preferred_element_type=F32)


def _dot_nt(a, b):
    return lax.dot_general(a, b, (((1,), (1,)), ((), ())), preferred_element_type=F32)


def _rms_scale(x):
    return lax.rsqrt(jnp.mean(x * x, axis=-1, keepdims=True) + EPS)


def _const_spec(shape):
    zeros = (0,) * len(shape)
    return pl.BlockSpec(shape, lambda *_: zeros, pipeline_mode=pl.Buffered(1))


def _lane_tile(t, width):
    reps = width // t.shape[1]
    return t if reps == 1 else jnp.concatenate([t] * reps, axis=1)


def _col_reduce(x, reduce_fn):
    rows, cols = x.shape
    slab = 8 * max(1, 8 * LANES // cols)
    if rows % slab or rows == slab:
        return reduce_fn(x, axis=0, keepdims=True)
    part = reduce_fn(x.reshape(rows // slab, slab, cols), axis=0)
    return reduce_fn(part, axis=0, keepdims=True)


def _inproj_body(x_ref, gmix_ref, w_ref, wvt_ref, cos_ref, sin_ref, bd_ref, gains_ref,
                 qa_ref, qi_ref, ks_ref, wi_ref, qb_ref, kb_ref, vbt_ref, vat_ref):
    x = x_ref[...]
    hn = (x * _rms_scale(x) * gmix_ref[...]).astype(BF16)

    def spread(t):
        turned = [t] + [pltpu.roll(t, 32 * k, 1) for k in range(1, 4)]
        group = lax.broadcasted_iota(jnp.int32, t.shape, 1) >> 5
        parts = []
        for j in range(4):
            d = (group - j) & 3
            parts.append(jnp.where(d == 0, turned[0], jnp.where(d == 1, turned[1],
                                                               jnp.where(d == 2, turned[2], turned[3]))))
        return jnp.concatenate(parts, axis=0)

    cos = spread(cos_ref[...])
    sin = spread(sin_ref[...]) * gains_ref[4:5, 0:LANES]
    bd = bd_ref[...]

    def proj(c0, width):
        return _dot(hn, w_ref[:, c0:c0 + width])

    def group_rms_scale(p):
        sq = (p * p).astype(BF16)
        outs = []
        for j in range(p.shape[1] // MXU_DIM):
            outs.append(_dot(sq[:, MXU_DIM * j:MXU_DIM * (j + 1)], bd))
        ms = outs[0] if len(outs) == 1 else jnp.concatenate(outs, axis=1)
        return lax.rsqrt(ms + EPS)

    def rope(y):
        width = y.shape[1]
        lane = lax.broadcasted_iota(jnp.int32, y.shape, 1)
        first_half = (lane & (HEAD_DIM - 1)) < (HEAD_DIM // 2)
        swapped = jnp.where(first_half, pltpu.roll(y, width - HEAD_DIM // 2, 1),
                            pltpu.roll(y, HEAD_DIM // 2, 1))
        return y * _lane_tile(cos, width) + swapped * _lane_tile(sin, width)

    sm_scale = HEAD_DIM ** -0.5 * LOG2E

    p = proj(_C_QA, 512)
    qa_ref[...] = (rope(p * group_rms_scale(p) * gains_ref[0:1, :]) * sm_scale).astype(BF16)

    p = proj(_C_QI, 256)
    qi_ref[...] = rope(p).astype(BF16)

    p = proj(_C_KS, 384)
    p01 = p[:, 0:256]
    lane = lax.broadcasted_iota(jnp.int32, p01.shape, 1)
    y01 = jnp.where(lane < LANES, p01 * group_rms_scale(p01) * gains_ref[1:2, 0:256], p01)
    y01 = rope(y01)
    p2 = p[:, 256:384]
    lane = lax.broadcasted_iota(jnp.int32, p2.shape, 1)
    p2 = jnp.where(lane < HEAD_DIM, p2, p2 * (IDX_HEADS ** -0.5 * HEAD_DIM ** -0.5))
    ks_ref[:, 0:256] = y01.astype(BF16)
    ks_ref[:, 256:384] = p2.astype(BF16)
    wi_ref[...] = p2

    def store_heads(ref, y):
        for h in range(B_HEADS):
            ref[0, h] = y[:, LANES * h:LANES * (h + 1)].astype(BF16)

    p = proj(_C_QB, 512)
    store_heads(qb_ref, rope(p * group_rms_scale(p) * gains_ref[2:3, :]) * sm_scale)

    p = proj(_C_KB, 512)
    store_heads(kb_ref, rope(p * group_rms_scale(p) * gains_ref[3:4, :]))

    vt = _dot_nt(wvt_ref[...], hn)
    for h in range(B_HEADS):
        vbt_ref[0, h] = vt[LANES * h:LANES * (h + 1), :].astype(BF16)
    vat_ref[0] = vt[LANES * B_HEADS:LANES * (B_HEADS + 1), :].astype(BF16)


def _inproj(x2, gmix, w_all, w_vt, cos128, sin128, bd, gains, tm, seq):
    n, d = x2.shape
    row = lambda i: (i, 0)
    const = lambda i: (0, 0)
    outs = [(512, BF16), (256, BF16), (384, BF16), (LANES, F32)]
    tiles = seq // tm
    head_spec = pl.BlockSpec((1, B_HEADS, tm, LANES), lambda i: (i // tiles, 0, i % tiles, 0))
    head_shape = jax.ShapeDtypeStruct((n // seq, B_HEADS, seq, LANES), BF16)
    vbt_spec = pl.BlockSpec((1, B_HEADS, LANES, tm), lambda i: (i // tiles, 0, 0, i % tiles))
    vbt_shape = jax.ShapeDtypeStruct((n // seq, B_HEADS, LANES, seq), BF16)
    vat_spec = pl.BlockSpec((1, LANES, tm), lambda i: (i // tiles, 0, i % tiles))
    vat_shape = jax.ShapeDtypeStruct((n // seq, LANES, seq), BF16)
    return pl.pallas_call(
        _inproj_body,
        grid=(n // tm,),
        in_specs=[
            pl.BlockSpec((tm, d), row),
            pl.BlockSpec((1, d), const),
            pl.BlockSpec(w_all.shape, const),
            pl.BlockSpec(w_vt.shape, const),
            pl.BlockSpec((tm // 4, LANES), row),
            pl.BlockSpec((tm // 4, LANES), row),
            pl.BlockSpec(bd.shape, const),
            pl.BlockSpec(gains.shape, const),
        ],
        out_specs=([pl.BlockSpec((tm, w), row) for w, _ in outs]
                   + [head_spec, head_spec, vbt_spec, vat_spec]),
        out_shape=([jax.ShapeDtypeStruct((n, w), dt) for w, dt in outs]
                   + [head_shape, head_shape, vbt_shape, vat_shape]),
        compiler_params=pltpu.CompilerParams(
            dimension_semantics=("arbitrary",), vmem_limit_bytes=VMEM_LIMIT),
        name="inproj",
    )(x2, gmix, w_all, w_vt, cos128, sin128, bd, gains)


def _group_reduce(x, reduce_fn):
    g, rows, cols = x.shape
    slab = 8 * max(1, 8 * LANES // cols)
    if rows % slab or rows == slab:
        return reduce_fn(x, axis=1, keepdims=True)
    combine = {jnp.sum: jnp.add, jnp.max: jnp.maximum, jnp.min: jnp.minimum}[reduce_fn]
    part = x[:, 0:slab]
    for i in range(1, rows // slab):
        part = combine(part, x[:, i * slab:(i + 1) * slab])
    return reduce_fn(part, axis=1, keepdims=True)


def _dsa_queries(qa_ref, rows):
    qa = qa_ref[0, rows, :]
    lane = lax.broadcasted_iota(jnp.int32, (DSA_QBLK, LANES), 1)
    rows = []
    for h in range(A_HEADS):
        slab = qa[:, LANES * (h // 2):LANES * (h // 2 + 1)]
        keep = (lane >= HEAD_DIM) if h % 2 else (lane < HEAD_DIM)
        rows.append(jnp.where(keep, slab, jnp.zeros_like(slab)))
    return jnp.concatenate(rows, axis=0)


def _dsa_write(ot, l, o_ref, ot_ref, rows):
    tq = DSA_QBLK
    ot = ot * (1.0 / l)
    for h in range(A_HEADS):
        ot_ref[HEAD_DIM * h:HEAD_DIM * (h + 1), :] = ot[:, tq * h:tq * (h + 1)]
    o_ref[0, rows, :] = ot_ref[...].T.astype(BF16)


def _dsa_rows(block):
    if isinstance(block, int):
        return pl.ds(block * DSA_QBLK, DSA_QBLK)
    return pl.ds(pl.multiple_of(block * DSA_QBLK, DSA_QBLK), DSA_QBLK)


def _dsa_keys(klen, pair, qa_ref, qi_ref, wi_ref, ks_ref, o_ref, vt_ref, sc_ref, bias_ref, ot_ref,
              kmax_ref, flag_ref, topk, n_bisect):
    tq = DSA_QBLK
    seq = ks_ref.shape[1]
    tail = klen - CAUSAL_STEP
    neg_inf = -jnp.inf

    lane_q = lax.broadcasted_iota(jnp.int32, (tq, LANES), 1)
    first_head = lane_q < HEAD_DIM

    def head_rows(slab, odd):
        return jnp.where(first_head != odd, slab, jnp.zeros_like(slab))

    shape_t = (DSA_PAIR, CAUSAL_STEP, tq)
    qpos_t = ((pair * DSA_PAIR + lax.broadcasted_iota(jnp.int32, shape_t, 0)) * tq
              + lax.broadcasted_iota(jnp.int32, shape_t, 2))
    causal_t = tail + lax.broadcasted_iota(jnp.int32, shape_t, 1) <= qpos_t

    def indexer(blk, carry):
        rows = _dsa_rows(blk)
        qi = qi_ref[0, rows, :]
        qi_stack = jnp.concatenate(
            [head_rows(qi[:, LANES * (h // 2):LANES * (h // 2 + 1)], bool(h % 2)) for h in range(IDX_HEADS)],
            axis=0)
        lg = _dot_nt(ks_ref[0, 0:klen, 128:256], qi_stack)
        w_t = wi_ref[0, rows, :].T
        sc = None
        for h in range(IDX_HEADS):
            term = jnp.maximum(lg[:, tq * h:tq * (h + 1)], 0.0) * w_t[HEAD_DIM + h:HEAD_DIM + h + 1, :]
            sc = term if sc is None else sc + term
        sc_ref[blk, 0:klen, :] = sc
        return carry

    lax.fori_loop(0, DSA_PAIR, indexer, 0)
    sc_ref[:, tail:klen, :] = jnp.where(causal_t, sc_ref[:, tail:klen, :], neg_inf)

    kf = float(topk)
    search = qpos_t[:, 0:1, :] >= topk

    slab = 64

    def fold(per_slab, combine, reduce_fn, start=0, stop=klen):
        acc = per_slab(sc_ref[:, start:start + slab, :], start)
        for r in range(start + slab, stop, slab):
            acc = combine(acc, per_slab(sc_ref[:, r:r + slab, :], r))
        return reduce_fn(acc, axis=1, keepdims=True)

    def count(pred):
        return fold(lambda s, _: jnp.where(pred(s), 1.0, 0.0), jnp.add, jnp.sum)

    hi0 = fold(lambda s, _: s, jnp.maximum, jnp.max)
    lo0 = fold(lambda s, r: jnp.where(causal_t[:, r - tail:r - tail + slab], s, jnp.inf),
               jnp.minimum, jnp.min, start=tail)
    if tail:
        lo0 = jnp.minimum(lo0, fold(lambda s, _: s, jnp.minimum, jnp.min, stop=tail))

    def bisect(_, carry):
        lo, hi = carry
        mid = 0.5 * (lo + hi)
        ge = count(lambda s: s >= mid) >= kf
        return jnp.where(ge, mid, lo), jnp.where(ge, hi, mid)

    lo, _ = lax.fori_loop(0, n_bisect, bisect, (lo0, hi0))

    def too_low(n_gt):
        return jnp.max(jnp.where(jnp.logical_and(search, n_gt >= kf), 1.0, 0.0))

    def climb(carry):
        thr, n_gt, _ = carry
        nxt = fold(lambda s, _: jnp.where(s > thr, s, jnp.inf), jnp.minimum, jnp.min)
        thr = jnp.where(jnp.logical_and(search, n_gt >= kf), nxt, thr)
        n_gt = count(lambda s: s > thr)
        return thr, n_gt, too_low(n_gt)

    thr0 = fold(lambda s, _: jnp.where(s >= lo, s, jnp.inf), jnp.minimum, jnp.min)
    n_gt0 = count(lambda s: s > thr0)
    thr, n_gt, _ = lax.while_loop(lambda c: c[2] > 0.0, climb, (thr0, n_gt0, too_low(n_gt0)))

    need = kf - n_gt
    open_row = jnp.where(search, neg_inf, 0.0)
    n_blk = klen // MXU_DIM
    tie_cat = jnp.concatenate(
        [jnp.where(sc_ref[g, MXU_DIM * j:MXU_DIM * (j + 1), :] == thr[g], 1.0, 0.0).astype(BF16)
         for g in range(DSA_PAIR) for j in range(n_blk)], axis=1)
    tri = jnp.where(lax.broadcasted_iota(jnp.int32, (MXU_DIM, MXU_DIM), 0)
                    >= lax.broadcasted_iota(jnp.int32, (MXU_DIM, MXU_DIM), 1), 1.0, 0.0).astype(BF16)
    prefix = _dot(tri, tie_cat)
    for g in range(DSA_PAIR):
        before = jnp.zeros((1, tq), F32)
        for j in range(n_blk):
            rows = slice(MXU_DIM * j, MXU_DIM * (j + 1))
            col = (g * n_blk + j) * tq
            rank = prefix[:, col:col + tq] + before
            before = before + prefix[MXU_DIM - 1:MXU_DIM, col:col + tq]
            sj = sc_ref[g, rows, :]
            admitted = jnp.where(sj == thr[g], jnp.where(rank <= need[g], 0.0, neg_inf), neg_inf)
            bias = jnp.maximum(jnp.where(sj > thr[g], 0.0, admitted), open_row[g])
            if j == n_blk - 1:
                bias = jnp.where(causal_t[g], bias, neg_inf)
            bias_ref[g, rows, :] = bias

    def attend(blk, underflow):
        rows = _dsa_rows(blk)
        q_all = _dsa_queries(qa_ref, rows)
        qsq = q_all.astype(F32)
        qn2 = _dot_nt(jnp.ones((8, LANES), BF16), (qsq * qsq).astype(BF16))[0:1, :]
        shift = jnp.sqrt(qn2) * (_lane_tile(kmax_ref[...], A_HEADS * tq) * SHIFT_MARGIN)
        kl = klen - tq * (DSA_PAIR - 1 - blk)
        bias = bias_ref[blk, 0:kl, :]
        st = _dot_nt(ks_ref[0, 0:kl, 0:128], q_all) + jnp.concatenate([bias] * A_HEADS, axis=1) - shift
        e = jnp.exp2(st)
        l = _col_reduce(e, jnp.sum)
        _dsa_write(_dot(vt_ref[0, 0:HEAD_DIM, 0:kl], e.astype(BF16)), l, o_ref, ot_ref, rows)
        return jnp.maximum(underflow, jnp.where(jnp.min(l) >= DENOM_FLOOR, 0, 1))

    underflow = 0
    for blk in range(DSA_PAIR):
        underflow = attend(blk, underflow)

    @pl.when(underflow != 0)
    def _():
        if klen < seq:
            bias_ref[:, klen:seq, :] = jnp.full((DSA_PAIR, seq - klen, tq), neg_inf, F32)
        flag_ref[0] = 1


def _dsa_body(qa_ref, qi_ref, wi_ref, ks_ref, vt_ref, kmax_ref, o_ref, sc_ref, bias_ref, ot_ref,
              flag_ref, *, topk, n_bisect):
    pair = pl.program_id(1)
    seq = ks_ref.shape[1]
    flag_ref[0] = 0

    for c in range(seq // CAUSAL_STEP):
        @pl.when(pair == c)
        def _(c=c):
            _dsa_keys(CAUSAL_STEP * (c + 1), pair, qa_ref, qi_ref, wi_ref, ks_ref, o_ref, vt_ref,
                      sc_ref, bias_ref, ot_ref, kmax_ref, flag_ref, topk, n_bisect)

    @pl.when(flag_ref[0] != 0)
    def _():
        def redo(blk, carry):
            rows = _dsa_rows(blk)
            bias = bias_ref[blk]
            st = (_dot_nt(ks_ref[0, :, 0:128], _dsa_queries(qa_ref, rows))
                  + jnp.concatenate([bias] * A_HEADS, axis=1))
            e = jnp.exp2(st - _col_reduce(st, jnp.max))
            _dsa_write(_dot(vt_ref[0, 0:HEAD_DIM, :], e.astype(BF16)), _col_reduce(e, jnp.sum), o_ref, ot_ref,
                       rows)
            return carry

        lax.fori_loop(0, DSA_PAIR, redo, 0)


def _dsa(qa, qi, wi, ks, vat, kmax, topk, n_bisect):
    b, s, _ = qa.shape
    tq = DSA_QBLK
    step = DSA_PAIR * tq
    assert step == CAUSAL_STEP and s % step == 0
    blk = lambda bi, qi_: (bi, qi_, 0)
    return pl.pallas_call(
        functools.partial(_dsa_body, topk=topk, n_bisect=n_bisect),
        grid=(b, s // step),
        in_specs=[
            pl.BlockSpec((1, step, 512), blk),
            pl.BlockSpec((1, step, 256), blk),
            pl.BlockSpec((1, step, LANES), blk),
            pl.BlockSpec((1, s, 384), lambda bi, qi_: (bi, 0, 0)),
            pl.BlockSpec((1, LANES, s), lambda bi, qi_: (bi, 0, 0)),
            pl.BlockSpec((1, LANES), lambda bi, qi_: (0, 0)),
        ],
        out_specs=pl.BlockSpec((1, step, 512), blk),
        out_shape=jax.ShapeDtypeStruct((b, s, 512), BF16),
        scratch_shapes=[
            pltpu.VMEM((DSA_PAIR, s, tq), F32),
            pltpu.VMEM((DSA_PAIR, s, tq), F32),
            pltpu.VMEM((A_HEADS * HEAD_DIM, tq), F32),
            pltpu.SMEM((1,), jnp.int32),
        ],
        compiler_params=pltpu.CompilerParams(
            dimension_semantics=("arbitrary", "arbitrary"), vmem_limit_bytes=VMEM_LIMIT),
        name="dsa_attention",
    )(qa, qi, wi, ks, vat, kmax)


def _diff_body(q_ref, k_ref, vt_ref, kmax_ref, lq1_ref, lk1_ref, lq2_ref, lk2_ref, o_ref,
               flag_ref, *, lambda_init):
    qblk = pl.program_id(1)
    seq = k_ref.shape[2]
    tq = q_ref.shape[2]

    lam =(jnp.exp(jnp.sum(lq1_ref[...] * lk1_ref[...], axis=1, keepdims=True))
           - jnp.exp(jnp.sum(lq2_ref[...] * lk2_ref[...], axis=1, keepdims=True)) + lambda_init)
    lane = lax.broadcasted_iota(jnp.int32, (tq, LANES), 1)
    diag = (lax.broadcasted_iota(jnp.int32, (tq, 2 * tq), 0)
            <= (lax.broadcasted_iota(jnp.int32, (tq, 2 * tq), 1) & (tq - 1)))

    def queries(h):
        q = q_ref[0, h]
        zero = jnp.zeros_like(q)
        return jnp.concatenate([jnp.where(lane < HEAD_DIM, q, zero), jnp.where(lane >= HEAD_DIM, q, zero)],
                               axis=0)

    def write(h, acc, l):
        acc = acc * (1.0 / l)
        ot = acc[:, 0:tq] - lam * acc[:, tq:2 * tq]
        ot = ot * lax.rsqrt(jnp.mean(ot * ot, axis=0, keepdims=True) + EPS) * (1.0 - lambda_init)
        o_ref[0, h] = ot.T.astype(BF16)

    def attend(klen, h, underflow):
        tail = klen - tq
        q2 = queries(h)
        qsq = q2.astype(F32)
        qn2 = _dot_nt(jnp.ones((8, LANES), BF16), (qsq * qsq).astype(BF16))[0:1, :]
        shift = jnp.sqrt(qn2) * (_lane_tile(kmax_ref[...], 2 * tq) * SHIFT_MARGIN)
        st = _dot_nt(k_ref[0, h, 0:klen, :], q2) - shift
        e = jnp.exp2(jnp.where(diag, st[tail:klen], -jnp.inf))
        l = _col_reduce(e, jnp.sum)
        acc = _dot(vt_ref[0, h, :, tail:klen], e.astype(BF16))
        if tail:
            e = jnp.exp2(st[0:tail])
            l = l + _col_reduce(e, jnp.sum)
            acc = acc + _dot(vt_ref[0, h, :, 0:tail], e.astype(BF16))
        write(h, acc, l)
        return jnp.maximum(underflow, jnp.where(jnp.min(l) >= DENOM_FLOOR, 0, 1))

    flag_ref[0] = 0
    for c in range(seq // tq):
        @pl.when(qblk == c)
        def _(c=c):
            underflow = 0
            for h in range(B_HEADS):
                underflow = attend(tq * (c + 1), h, underflow)
            flag_ref[0] = underflow

    @pl.when(flag_ref[0] != 0)
    def _():
        kpos = lax.broadcasted_iota(jnp.int32, (seq, 2 * tq), 0)
        qpos = qblk * tq + (lax.broadcasted_iota(jnp.int32, (seq, 2 * tq), 1) & (tq - 1))

        def redo(h, carry):
            st = jnp.where(kpos <= qpos, _dot_nt(k_ref[0, h], queries(h)), -jnp.inf)
            e = jnp.exp2(st - _col_reduce(st, jnp.max))
            write(h, _dot(vt_ref[0, h], e.astype(BF16)), _col_reduce(e, jnp.sum))
            return carry

        lax.fori_loop(0, B_HEADS, redo, 0)


def _diff(qb, kb, vbt, kmax, lq1, lk1, lq2, lk2, lambda_init):
    b, _, s, _ = qb.shape
    tq = min(DIFF_QBLK, s)
    vec = pl.BlockSpec((1, HEAD_DIM), lambda bi, qi_: (0, 0))
    keys = pl.BlockSpec((1, B_HEADS, s, LANES), lambda bi, qi_: (bi, 0, 0, 0))
    vals = pl.BlockSpec((1, B_HEADS, LANES, s), lambda bi, qi_: (bi, 0, 0, 0))
    blk = pl.BlockSpec((1, B_HEADS, tq, LANES), lambda bi, qi_: (bi, 0, qi_, 0))
    return pl.pallas_call(
        functools.partial(_diff_body, lambda_init=lambda_init),
        grid=(b, s // tq),
        in_specs=[blk, keys, vals, pl.BlockSpec((1, LANES), lambda bi, qi_: (0, 0)), vec, vec, vec, vec],
        out_specs=blk,
        out_shape=jax.ShapeDtypeStruct((b, B_HEADS, s, LANES), BF16),
        scratch_shapes=[pltpu.SMEM((1,), jnp.int32)],
        compiler_params=pltpu.CompilerParams(
            dimension_semantics=("arbitrary", "arbitrary"), vmem_limit_bytes=VMEM_LIMIT),
        name="diff_attention",
    )(qb, kb, vbt, kmax, lq1, lk1, lq2, lk2)


def _memkv_body(mem_ref, g_ref, wk_ref, wv_ref, gk_ref, k_ref, v_ref):
    mem = mem_ref[0]
    memn = (mem * _rms_scale(mem) * g_ref[...]).astype(BF16)
    k = _dot(memn, wk_ref[...])
    hd = gk_ref.shape[1]
    for h in range(k.shape[1] // hd):
        kh = k[:, hd * h:hd * (h + 1)]
        k_ref[0, :, hd * h:hd * (h + 1)] = (kh * _rms_scale(kh) * gk_ref[...]).astype(BF16)
    v_ref[0] = _dot(memn, wv_ref[...]).astype(BF16)


def _memkv(mem, g_mem, w_xk, w_xv, g_xk):
    b, m, d = mem.shape
    const = lambda bi: (0, 0)
    blk = pl.BlockSpec((1, m, d), lambda bi: (bi, 0, 0))
    return pl.pallas_call(
        _memkv_body,
        grid=(b,),
        in_specs=[blk, pl.BlockSpec((1, d), const), pl.BlockSpec((d, d), const),
                  pl.BlockSpec((d, d), const), pl.BlockSpec(g_xk.shape, const)],
        out_specs=[blk, blk],
        out_shape=[jax.ShapeDtypeStruct((b, m, d), BF16)] * 2,
        compiler_params=pltpu.CompilerParams(
            dimension_semantics=("arbitrary",), vmem_limit_bytes=VMEM_LIMIT),
        name="mem_kv",
    )(mem, g_mem, w_xk, w_xv, g_xk)


def _xattn_body(x_ref, oa_ref, ob_ref, wo_ref, g_ref, wq_ref, gq_ref, k_ref, v_ref, wxo_ref, h_ref):
    half = oa_ref.shape[2]
    ob = jnp.concatenate([ob_ref[0, h] for h in range(B_HEADS)], axis=1)
    h1 = x_ref[0] + _dot(oa_ref[0], wo_ref[0:half, :]) + _dot(ob, wo_ref[half:2 * half, :])
    hn = (h1 * _rms_scale(h1) * g_ref[...]).astype(BF16)
    q = _dot(hn, wq_ref[...])
    hd = gq_ref.shape[1]
    outs = []
    for h in range(q.shape[1] // hd):
        sl = slice(hd * h, hd * (h + 1))
        qh = q[:, sl]
        qh = (qh * _rms_scale(qh) * gq_ref[...] * (hd ** -0.5)).astype(BF16)
        s = _dot_nt(qh, k_ref[0, :, sl])
        e = jnp.exp(s - jnp.max(s, axis=-1, keepdims=True))
        p = (e * (1.0 / jnp.sum(e, axis=-1, keepdims=True))).astype(BF16)
        outs.append(_dot(p, v_ref[0, :, sl]).astype(BF16))
    o = jnp.concatenate(outs, axis=1)
    h_ref[0] = h1 + _dot(o, wxo_ref[...])


def _xattn(x, oa, ob, w_out, g_x, w_xq, g_xq, kmem, vmem, w_xo, tm):
    b, s, d = x.shape
    m = kmem.shape[1]
    const = lambda bi, ti: (0, 0)
    tok = lambda w: pl.BlockSpec((1, tm, w), lambda bi, ti: (bi, ti, 0))
    memblk = pl.BlockSpec((1, m, d), lambda bi, ti: (bi, 0, 0))
    return pl.pallas_call(
        _xattn_body,
        grid=(b, s // tm),
        in_specs=[tok(d), tok(oa.shape[2]),
                  pl.BlockSpec((1, B_HEADS, tm, LANES), lambda bi, ti: (bi, 0, ti, 0)),
                  pl.BlockSpec(w_out.shape, const), pl.BlockSpec((1, d), const),
                  pl.BlockSpec(w_xq.shape, const), pl.BlockSpec(g_xq.shape, const),
                  memblk, memblk, pl.BlockSpec(w_xo.shape, const)],
        out_specs=tok(d),
        out_shape=jax.ShapeDtypeStruct((b, s, d), F32),
        compiler_params=pltpu.CompilerParams(
            dimension_semantics=("arbitrary", "arbitrary"), vmem_limit_bytes=VMEM_LIMIT),
        name="outproj_xattn",
    )(x, oa, ob, w_out, g_x, w_xq, g_xq, kmem, vmem, w_xo)


HALO = 8


def _ffn_body(h_ref, g_ref, win_ref, cw_ref, cb_ref, wo_ref, o_ref, a_ref):
    tm = h_ref.shape[1]
    dff = wo_ref.shape[0]

    @pl.when(pl.program_id(1) == 0)
    def _():
        a_ref[0:HALO, :] = jnp.zeros((HALO, a_ref.shape[1]), F32)

    h = h_ref[0]
    hn = (h * _rms_scale(h) * g_ref[...]).astype(BF16)
    a_ref[HALO:HALO + tm, :] = _dot(hn, win_ref[:, 0:dff])
    gate = _dot(hn, win_ref[:, dff:2 * dff])
    conv = cb_ref[...]
    for j in range(CONV_W):
        off = HALO - (CONV_W - 1) + j
        conv = conv + a_ref[off:off + tm, :] * cw_ref[j:j + 1, :]
    a_ref[0:HALO, :] = a_ref[tm:tm + HALO, :]
    u = (jax.nn.gelu(conv) * gate).astype(BF16)
    o_ref[0] = h + _dot(u, wo_ref[...])


def _ffn(h, g_ffn, w_in, conv_w, conv_b, w_o, tm):
    b, s, d = h.shape
    dff = w_o.shape[0]
    const = lambda bi, ti: (0, 0)
    tok = pl.BlockSpec((1, tm, d), lambda bi, ti: (bi, ti, 0))
    return pl.pallas_call(
        _ffn_body,
        grid=(b, s // tm),
        in_specs=[tok, pl.BlockSpec((1, d), const), _const_spec(w_in.shape),
                  pl.BlockSpec(conv_w.shape, const),
                  pl.BlockSpec((1, dff), const), _const_spec(w_o.shape)],
        out_specs=tok,
        out_shape=jax.ShapeDtypeStruct((b, s, d), F32),
        scratch_shapes=[pltpu.VMEM((tm + HALO, dff), F32)],
        compiler_params=pltpu.CompilerParams(
            dimension_semantics=("arbitrary", "arbitrary"), vmem_limit_bytes=VMEM_LIMIT),
        name="conv_glu",
    )(h, g_ffn, w_in, conv_w, conv_b, w_o)


def _rearranged_w_in(w_in):
    sizes = (A_HEADS * HEAD_DIM, HEAD_DIM, HEAD_DIM, IDX_HEADS * HEAD_DIM, HEAD_DIM, IDX_HEADS,
             2 * B_HEADS * HEAD_DIM, 2 * B_HEADS * HEAD_DIM, B_HEADS * 2 * HEAD_DIM)
    offs = [0]
    for sz in sizes:
        offs.append(offs[-1] + sz)
    q_a, k_a, v_a, q_i, k_i, w_i, q_b, k_b, v_b = [w_in[:, offs[i]:offs[i + 1]] for i in range(9)]
    pad = jnp.zeros((w_in.shape[0], HEAD_DIM - IDX_HEADS), w_in.dtype)
    w_all = jnp.concatenate([q_a, q_i, k_a, k_a, k_i, k_i, v_a, w_i, pad, q_b, k_b], axis=1)
    assert w_all.shape[1] == _C_END
    w_vt = jnp.concatenate([v_b, v_a, jnp.zeros((w_in.shape[0], HEAD_DIM), w_in.dtype)], axis=1).T
    return w_all.astype(BF16), w_vt.astype(BF16)


def kernel(x, mem, positions, g_mix, w_in, g_qa, g_ka, g_qb, g_kb, lam_q1, lam_k1, lam_q2, lam_k2,
           w_out, g_xattn, g_mem, w_xq, w_xk, w_xv, w_xo, g_xq, g_xk, g_ffn, w_ffn_in, conv_w, conv_b,
           w_ffn_out):
    b, s, d = x.shape
    depth = g_mix.shape[0]
    topk = min(TOPK_MAX, s // 4)
    tm = min(512, s)
    tm_in = min(1024, s)

    inv_freq = 1.0 / (ROPE_THETA ** (jnp.arange(0, HEAD_DIM, 2, dtype=F32) / HEAD_DIM))
    pos = positions.reshape(b * s // tm_in, 4, tm_in // 4).transpose(0, 2, 1).reshape(b * s // 4, 4)
    ang = (pos.astype(F32)[:, :, None] * inv_freq).reshape(b * s // 4, LANES)
    cos_d, sin_d = jnp.cos(ang), jnp.sin(ang)
    sign = jnp.tile(jnp.repeat(jnp.array([-1.0, 1.0], F32), HEAD_DIM // 2), 512 // HEAD_DIM)
    blk = jnp.arange(MXU_DIM) // HEAD_DIM
    bd = jnp.where(blk[:, None] == blk[None, :], 1.0 / HEAD_DIM, 0.0).astype(BF16)

    h = x
    for l in range(depth):
        lambda_init = 0.8 - 0.6 * math.exp(-0.3 * l)
        gains = jnp.stack([jnp.tile(g, 512 // HEAD_DIM) for g in (g_qa[l], g_ka[l], g_qb[l], g_kb[l])])
        gains = jnp.concatenate([gains, sign[None, :], jnp.ones((3, 512), F32)], axis=0)
        w_all, w_vt = _rearranged_w_in(w_in[l])
        qa, qi, ks, wi, qb, kb, vbt, vat = _inproj(
            h.reshape(b * s, d), g_mix[l][None, :], w_all, w_vt, cos_d, sin_d, bd, gains, tm_in, s)
        r3 = lambda t: t.reshape(b, s, t.shape[-1])
        key_bound = lambda g: jnp.full((1, LANES), HEAD_DIM ** 0.5, F32) * jnp.max(jnp.abs(g))
        out_a = _dsa(r3(qa), r3(qi), r3(wi), r3(ks), vat, key_bound(g_ka[l]), topk, n_bisect=20)
        out_b = _diff(qb, kb, vbt, key_bound(g_kb[l]), lam_q1[l][None, :], lam_k1[l][None, :],
                      lam_q2[l][None, :], lam_k2[l][None, :], lambda_init)
        kmem, vmem = _memkv(mem, g_mem[l][None, :], w_xk[l].astype(BF16), w_xv[l].astype(BF16),
                            g_xk[l][None, :])
        h = _xattn(h, out_a, out_b, w_out[l].astype(BF16), g_xattn[l][None, :], w_xq[l].astype(BF16),
                   g_xq[l][None, :], kmem, vmem, w_xo[l].astype(BF16), min(1024, s))
        dff = w_ffn_out.shape[1]
        cw = jnp.concatenate([conv_w[l], jnp.zeros((8 - CONV_W, dff), F32)], axis=0)
        h = _ffn(h, g_ffn[l][None, :], w_ffn_in[l].astype(BF16), cw, conv_b[l][None, :],
                 w_ffn_out[l].astype(BF16), tm)
    return h
```

```python
import functools
import math

import jax
import jax.numpy as jnp
from jax import lax
from jax.experimental import pallas as pl
from jax.experimental.pallas import tpu as pltpu

F32 = jnp.float32
BF16 = jnp.bfloat16

EPS = 1e-6
ROPE_THETA = 10000.0
HEAD_DIM = 64
A_HEADS = 8
IDX_HEADS = 4
TOPK_MAX = 256
B_HEADS = 4
X_HEADS = 4
CONV_W = 3
LANES = 128
MXU_DIM = 256
DSA_QBLK = 128
DSA_PAIR = 2
DIFF_QBLK = 256
DIFF_STEP_BLOCKS = 2
CAUSAL_STEP = MXU_DIM
VMEM_LIMIT = 56 * 1024 * 1024
LOG2E = 1.4426950408889634
SHIFT_MARGIN = 1.02
DENOM_FLOOR = 2.0 ** -40

_C_QA = 0
_C_QI = 512
_C_KS = 768
_C_QB = 1152
_C_KB = 1664
_C_END = 2176


def _dot(a, b):
    return jnp.dot(a, b, preferred_element_type=F32)


def _dot_nt(a, b):
    return lax.dot_general(a, b, (((1,), (1,)), ((), ())), preferred_element_type=F32)


def _rms_scale(x):
    return lax.rsqrt(jnp.mean(x * x, axis=-1, keepdims=True) + EPS)


def _const_spec(shape):
    zeros = (0,) * len(shape)
    return pl.BlockSpec(shape, lambda *_: zeros, pipeline_mode=pl.Buffered(1))


def _lane_tile(t, width):
    reps = width // t.shape[1]
    return t if reps == 1 else jnp.concatenate([t] * reps, axis=1)


def _col_reduce(x, reduce_fn):
    rows, cols = x.shape
    slab = 8 * max(1, 8 * LANES // cols)
    if rows % slab or rows == slab:
        return reduce_fn(x, axis=0, keepdims=True)
    part = reduce_fn(x.reshape(rows // slab, slab, cols), axis=0)
    return reduce_fn(part, axis=0, keepdims=True)


def _inproj_body(x_ref, gmix_ref, w_ref, wvt_ref, cos_ref, sin_ref, bd_ref, gains_ref,
                 qa_ref, qi_ref, ks_ref, wi_ref, qb_ref, kb_ref, vbt_ref, vat_ref):
    x = x_ref[...]
    hn = (x * _rms_scale(x) * gmix_ref[...]).astype(BF16)

    def spread(t):
        turned = [t] + [pltpu.roll(t, 32 * k, 1) for k in range(1, 4)]
        group = lax.broadcasted_iota(jnp.int32, t.shape, 1) >> 5
        parts = []
        for j in range(4):
            d = (group - j) & 3
            parts.append(jnp.where(d == 0, turned[0], jnp.where(d == 1, turned[1],
                                                               jnp.where(d == 2, turned[2], turned[3]))))
        return jnp.concatenate(parts, axis=0)

    cos = spread(cos_ref[...])
    sin = spread(sin_ref[...]) * gains_ref[4:5, 0:LANES]
    bd = bd_ref[...]

    def proj(c0, width):
        return _dot(hn, w_ref[:, c0:c0 + width])

    def group_rms_scale(p):
        sq = (p * p).astype(BF16)
        outs = []
        for j in range(p.shape[1] // MXU_DIM):
            outs.append(_dot(sq[:, MXU_DIM * j:MXU_DIM * (j + 1)], bd))
        ms = outs[0] if len(outs) == 1 else jnp.concatenate(outs, axis=1)
        return lax.rsqrt(ms + EPS)

    def rope(y):
        width = y.shape[1]
        lane = lax.broadcasted_iota(jnp.int32, y.shape, 1)
        first_half = (lane & (HEAD_DIM - 1)) < (HEAD_DIM // 2)
        swapped = jnp.where(first_half, pltpu.roll(y, width - HEAD_DIM // 2, 1),
                            pltpu.roll(y, HEAD_DIM // 2, 1))
        return y * _lane_tile(cos, width) + swapped * _lane_tile(sin, width)

    sm_scale = HEAD_DIM ** -0.5 * LOG2E

    p = proj(_C_QA, 512)
    qa_ref[...] = (rope(p * group_rms_scale(p) * gains_ref[0:1, :]) * sm_scale).astype(BF16)

    p = proj(_C_QI, 256)
    qi_ref[...] = rope(p).astype(BF16)

    p = proj(_C_KS, 384)
    p01 = p[:, 0:256]
    lane = lax.broadcasted_iota(jnp.int32, p01.shape, 1)
    y01 = jnp.where(lane < LANES, p01 * group_rms_scale(p01) * gains_ref[1:2, 0:256], p01)
    y01 = rope(y01)
    p2 = p[:, 256:384]
    lane = lax.broadcasted_iota(jnp.int32, p2.shape, 1)
    p2 = jnp.where(lane < HEAD_DIM, p2, p2 * (IDX_HEADS ** -0.5 * HEAD_DIM ** -0.5))
    ks_ref[:, 0:256] = y01.astype(BF16)
    ks_ref[:, 256:384] = p2.astype(BF16)
    wi_ref[...] = p2

    def store_heads(ref, y):
        for h in range(B_HEADS):
            ref[0, h] = y[:, LANES * h:LANES * (h + 1)].astype(BF16)

    p = proj(_C_QB, 512)
    store_heads(qb_ref, rope(p * group_rms_scale(p) * gains_ref[2:3, :]) * sm_scale)

    p = proj(_C_KB, 512)
    store_heads(kb_ref, rope(p * group_rms_scale(p) * gains_ref[3:4, :]))

    vt = _dot_nt(wvt_ref[...], hn)
    for h in range(B_HEADS):
        vbt_ref[0, h] = vt[LANES * h:LANES * (h + 1), :].astype(BF16)
    vat_ref[0] = vt[LANES * B_HEADS:LANES * (B_HEADS + 1), :].astype(BF16)


def _inproj(x2, gmix, w_all, w_vt, cos128, sin128, bd, gains, tm, seq):
    n, d = x2.shape
    row = lambda i: (i, 0)
    const = lambda i: (0, 0)
    outs = [(512, BF16), (256, BF16), (384, BF16), (LANES, F32)]
    tiles = seq // tm
    head_spec = pl.BlockSpec((1, B_HEADS, tm, LANES), lambda i: (i // tiles, 0, i % tiles, 0))
    head_shape = jax.ShapeDtypeStruct((n // seq, B_HEADS, seq, LANES), BF16)
    vbt_spec = pl.BlockSpec((1, B_HEADS, LANES, tm), lambda i: (i // tiles, 0, 0, i % tiles))
    vbt_shape = jax.ShapeDtypeStruct((n // seq, B_HEADS, LANES, seq), BF16)
    vat_spec = pl.BlockSpec((1, LANES, tm), lambda i: (i // tiles, 0, i % tiles))
    vat_shape = jax.ShapeDtypeStruct((n // seq, LANES, seq), BF16)
    return pl.pallas_call(
        _inproj_body,
        grid=(n // tm,),
        in_specs=[
            pl.BlockSpec((tm, d), row),
            pl.BlockSpec((1, d), const),
            pl.BlockSpec(w_all.shape, const),
            pl.BlockSpec(w_vt.shape, const),
            pl.BlockSpec((tm // 4, LANES), row),
            pl.BlockSpec((tm // 4, LANES), row),
            pl.BlockSpec(bd.shape, const),
            pl.BlockSpec(gains.shape, const),
        ],
        out_specs=([pl.BlockSpec((tm, w), row) for w, _ in outs]
                   + [head_spec, head_spec, vbt_spec, vat_spec]),
        out_shape=([jax.ShapeDtypeStruct((n, w), dt) for w, dt in outs]
                   + [head_shape, head_shape, vbt_shape, vat_shape]),
        compiler_params=pltpu.CompilerParams(
            dimension_semantics=("arbitrary",), vmem_limit_bytes=VMEM_LIMIT),
        name="inproj",
    )(x2, gmix, w_all, w_vt, cos128, sin128, bd, gains)


def _group_reduce(x, reduce_fn):
    g, rows, cols = x.shape
    slab = 8 * max(1, 8 * LANES // cols)
    if rows % slab or rows == slab:
        return reduce_fn(x, axis=1, keepdims=True)
    combine = {jnp.sum: jnp.add, jnp.max: jnp.maximum, jnp.min: jnp.minimum}[reduce_fn]
    part = x[:, 0:slab]
    for i in range(1, rows // slab):
        part = combine(part, x[:, i * slab:(i + 1) * slab])
    return reduce_fn(part, axis=1, keepdims=True)


def _dsa_queries(qa_ref, rows):
    qa = qa_ref[0, rows, :]
    lane = lax.broadcasted_iota(jnp.int32, (DSA_QBLK, LANES), 1)
    rows = []
    for h in range(A_HEADS):
        slab = qa[:, LANES * (h // 2):LANES * (h // 2 + 1)]
        keep = (lane >= HEAD_DIM) if h % 2 else (lane < HEAD_DIM)
        rows.append(jnp.where(keep, slab, jnp.zeros_like(slab)))
    return jnp.concatenate(rows, axis=0)


def _dsa_write(ot, l, o_ref, ot_ref, rows):
    tq = DSA_QBLK
    ot = ot * (1.0 / l)
    for h in range(A_HEADS):
        ot_ref[HEAD_DIM * h:HEAD_DIM * (h + 1), :] = ot[:, tq * h:tq * (h + 1)]
    o_ref[0, rows, :] = ot_ref[...].T.astype(BF16)


def _dsa_rows(block):
    if isinstance(block, int):
        return pl.ds(block * DSA_QBLK, DSA_QBLK)
    return pl.ds(pl.multiple_of(block * DSA_QBLK, DSA_QBLK), DSA_QBLK)


def _dsa_keys(klen, pair, qa_ref, qi_ref, wi_ref, ks_ref, o_ref, vt_ref, sc_ref, bias_ref, ot_ref,
              kmax_ref, flag_ref, topk, n_bisect):
    tq = DSA_QBLK
    seq = ks_ref.shape[1]
    tail = klen - CAUSAL_STEP
    neg_inf = -jnp.inf

    lane_q = lax.broadcasted_iota(jnp.int32, (tq, LANES), 1)
    first_head = lane_q < HEAD_DIM

    def head_rows(slab, odd):
        return jnp.where(first_head != odd, slab, jnp.zeros_like(slab))

    shape_t = (DSA_PAIR, CAUSAL_STEP, tq)
    qpos_t = ((pair * DSA_PAIR + lax.broadcasted_iota(jnp.int32, shape_t, 0)) * tq
              + lax.broadcasted_iota(jnp.int32, shape_t, 2))
    causal_t = tail + lax.broadcasted_iota(jnp.int32, shape_t, 1) <= qpos_t

    def indexer(blk, carry):
        rows = _dsa_rows(blk)
        qi = qi_ref[0, rows, :]
        qi_stack = jnp.concatenate(
            [head_rows(qi[:, LANES * (h // 2):LANES * (h // 2 + 1)], bool(h % 2)) for h in range(IDX_HEADS)],
            axis=0)
        lg = _dot_nt(ks_ref[0, 0:klen, 128:256], qi_stack)
        w_t = wi_ref[0, rows, :].T
        sc = None
        for h in range(IDX_HEADS):
            term = jnp.maximum(lg[:, tq * h:tq * (h + 1)], 0.0) * w_t[HEAD_DIM + h:HEAD_DIM + h + 1, :]
            sc = term if sc is None else sc + term
        sc_ref[blk, 0:klen, :] = sc
        return carry

    lax.fori_loop(0, DSA_PAIR, indexer, 0)
    sc_ref[:, tail:klen, :] = jnp.where(causal_t, sc_ref[:, tail:klen, :], neg_inf)

    kf = float(topk)
    search = qpos_t[:, 0:1, :] >= topk

    slab = 64

    def fold(per_slab, combine, reduce_fn, start=0, stop=klen):
        acc = per_slab(sc_ref[:, start:start + slab, :], start)
        for r in range(start + slab, stop, slab):
            acc = combine(acc, per_slab(sc_ref[:, r:r + slab, :], r))
        return reduce_fn(acc, axis=1, keepdims=True)

    def count(pred):
        return fold(lambda s, _: jnp.where(pred(s), 1.0, 0.0), jnp.add, jnp.sum)

    hi0 = fold(lambda s, _: s, jnp.maximum, jnp.max)
    lo0 = fold(lambda s, r: jnp.where(causal_t[:, r - tail:r - tail + slab], s, jnp.inf),
               jnp.minimum, jnp.min, start=tail)
    if tail:
        lo0 = jnp.minimum(lo0, fold(lambda s, _: s, jnp.minimum, jnp.min, stop=tail))

    def bisect(_, carry):
        lo, hi = carry
        mid = 0.5 * (lo + hi)
        ge = count(lambda s: s >= mid) >= kf
        return jnp.where(ge, mid, lo), jnp.where(ge, hi, mid)

    lo, _ = lax.fori_loop(0, n_bisect, bisect, (lo0, hi0))

    def too_low(n_gt):
        return jnp.max(jnp.where(jnp.logical_and(search, n_gt >= kf), 1.0, 0.0))

    def climb(carry):
        thr, n_gt, _ = carry
        nxt = fold(lambda s, _: jnp.where(s > thr, s, jnp.inf), jnp.minimum, jnp.min)
        thr = jnp.where(jnp.logical_and(search, n_gt >= kf), nxt, thr)
        n_gt = count(lambda s: s > thr)
        return thr, n_gt, too_low(n_gt)

    thr0 = fold(lambda s, _: jnp.where(s >= lo, s, jnp.inf), jnp.minimum, jnp.min)
    n_gt0 = count(lambda s: s > thr0)
    thr, n_gt, _ = lax.while_loop(lambda c: c[2] > 0.0, climb, (thr0, n_gt0, too_low(n_gt0)))

    need = kf - n_gt
    open_row = jnp.where(search, neg_inf, 0.0)
    n_blk = klen // MXU_DIM
    tie_cat = jnp.concatenate(
        [jnp.where(sc_ref[g, MXU_DIM * j:MXU_DIM * (j + 1), :] == thr[g], 1.0, 0.0).astype(BF16)
         for g in range(DSA_PAIR) for j in range(n_blk)], axis=1)
    tri = jnp.where(lax.broadcasted_iota(jnp.int32, (MXU_DIM, MXU_DIM), 0)
                    >= lax.broadcasted_iota(jnp.int32, (MXU_DIM, MXU_DIM), 1), 1.0, 0.0).astype(BF16)
    prefix = _dot(tri, tie_cat)
    for g in range(DSA_PAIR):
        before = jnp.zeros((1, tq), F32)
        for j in range(n_blk):
            rows = slice(MXU_DIM * j, MXU_DIM * (j + 1))
            col = (g * n_blk + j) * tq
            rank = prefix[:, col:col + tq] + before
            before = before + prefix[MXU_DIM - 1:MXU_DIM, col:col + tq]
            sj = sc_ref[g, rows, :]
            admitted = jnp.where(sj == thr[g], jnp.where(rank <= need[g], 0.0, neg_inf), neg_inf)
            bias = jnp.maximum(jnp.where(sj > thr[g], 0.0, admitted), open_row[g])
            if j == n_blk - 1:
                bias = jnp.where(causal_t[g], bias, neg_inf)
            bias_ref[g, rows, :] = bias

    def attend(blk, underflow):
        rows = _dsa_rows(blk)
        q_all = _dsa_queries(qa_ref, rows)
        qsq = q_all.astype(F32)
        qn2 = _dot_nt(jnp.ones((8, LANES), BF16), (qsq * qsq).astype(BF16))[0:1, :]
        shift = jnp.sqrt(qn2) * (_lane_tile(kmax_ref[...], A_HEADS * tq) * SHIFT_MARGIN)
        kl = klen - tq * (DSA_PAIR - 1 - blk)
        bias = bias_ref[blk, 0:kl, :]
        st = _dot_nt(ks_ref[0, 0:kl, 0:128], q_all) + jnp.concatenate([bias] * A_HEADS, axis=1) - shift
        e = jnp.exp2(st)
        l = _col_reduce(e, jnp.sum)
        _dsa_write(_dot(vt_ref[0, 0:HEAD_DIM, 0:kl], e.astype(BF16)), l, o_ref, ot_ref, rows)
        return jnp.maximum(underflow, jnp.where(jnp.min(l) >= DENOM_FLOOR, 0, 1))

    underflow = 0
    for blk in range(DSA_PAIR):
        underflow = attend(blk, underflow)

    @pl.when(underflow != 0)
    def _():
        if klen < seq:
            bias_ref[:, klen:seq, :] = jnp.full((DSA_PAIR, seq - klen, tq), neg_inf, F32)
        flag_ref[0] = 1


def _dsa_body(qa_ref, qi_ref, wi_ref, ks_ref, vt_ref, kmax_ref, o_ref, sc_ref, bias_ref, ot_ref,
              flag_ref, *, topk, n_bisect):
    pair = pl.program_id(1)
    seq = ks_ref.shape[1]
    flag_ref[0] = 0

    for c in range(seq // CAUSAL_STEP):
        @pl.when(pair == c)
        def _(c=c):
            _dsa_keys(CAUSAL_STEP * (c + 1), pair, qa_ref, qi_ref, wi_ref, ks_ref, o_ref, vt_ref,
                      sc_ref, bias_ref, ot_ref, kmax_ref, flag_ref, topk, n_bisect)

    @pl.when(flag_ref[0] != 0)
    def _():
        def redo(blk, carry):
            rows = _dsa_rows(blk)
            bias = bias_ref[blk]
            st = (_dot_nt(ks_ref[0, :, 0:128], _dsa_queries(qa_ref, rows))
                  + jnp.concatenate([bias] * A_HEADS, axis=1))
            e = jnp.exp2(st - _col_reduce(st, jnp.max))
            _dsa_write(_dot(vt_ref[0, 0:HEAD_DIM, :], e.astype(BF16)), _col_reduce(e, jnp.sum), o_ref, ot_ref,
                       rows)
            return carry

        lax.fori_loop(0, DSA_PAIR, redo, 0)


def _dsa(qa, qi, wi, ks, vat, kmax, topk, n_bisect):
    b, s, _ = qa.shape
    tq = DSA_QBLK
    step = DSA_PAIR * tq
    assert step == CAUSAL_STEP and s % step == 0
    blk = lambda bi, qi_: (bi, qi_, 0)
    return pl.pallas_call(
        functools.partial(_dsa_body, topk=topk, n_bisect=n_bisect),
        grid=(b, s // step),
        in_specs=[
            pl.BlockSpec((1, step, 512), blk),
            pl.BlockSpec((1, step, 256), blk),
            pl.BlockSpec((1, step, LANES), blk),
            pl.BlockSpec((1, s, 384), lambda bi, qi_: (bi, 0, 0)),
            pl.BlockSpec((1, LANES, s), lambda bi, qi_: (bi, 0, 0)),
            pl.BlockSpec((1, LANES), lambda bi, qi_: (0, 0)),
        ],
        out_specs=pl.BlockSpec((1, step, 512), blk),
        out_shape=jax.ShapeDtypeStruct((b, s, 512), BF16),
        scratch_shapes=[
            pltpu.VMEM((DSA_PAIR, s, tq), F32),
            pltpu.VMEM((DSA_PAIR, s, tq), F32),
            pltpu.VMEM((A_HEADS * HEAD_DIM, tq), F32),
            pltpu.SMEM((1,), jnp.int32),
        ],
        compiler_params=pltpu.CompilerParams(
            dimension_semantics=("arbitrary", "arbitrary"), vmem_limit_bytes=VMEM_LIMIT),
        name="dsa_attention",
    )(qa, qi, wi, ks, vat, kmax)


def _diff_body(q_ref, k_ref, vt_ref, kmax_ref, lq1_ref, lk1_ref, lq2_ref, lk2_ref, o_ref,
               flag_ref, *, lambda_init):
    step = pl.program_id(1)
    seq = k_ref.shape[2]
    n_sub = q_ref.shape[2] // DIFF_QBLK
    tq = q_ref.shape[2] // n_sub

    lam =(jnp.exp(jnp.sum(lq1_ref[...] * lk1_ref[...], axis=1, keepdims=True))
           - jnp.exp(jnp.sum(lq2_ref[...] * lk2_ref[...], axis=1, keepdims=True)) + lambda_init)
    lane = lax.broadcasted_iota(jnp.int32, (tq, LANES), 1)
    diag = (lax.broadcasted_iota(jnp.int32, (tq, 2 * tq), 0)
            <= (lax.broadcasted_iota(jnp.int32, (tq, 2 * tq), 1) & (tq - 1)))

    def queries(h, rows):
        q = q_ref[0, h, rows, :]
        zero = jnp.zeros_like(q)
        return jnp.concatenate([jnp.where(lane < HEAD_DIM, q, zero), jnp.where(lane >= HEAD_DIM, q, zero)],
                               axis=0)

    def write(h, rows, acc, l):
        acc = acc * (1.0 / l)
        ot = acc[:, 0:tq] - lam * acc[:, tq:2 * tq]
        ot = ot * lax.rsqrt(jnp.mean(ot * ot, axis=0, keepdims=True) + EPS) * (1.0 - lambda_init)
        o_ref[0, h, rows, :] = ot.T.astype(BF16)

    def attend(klen, h, rows, underflow):
        tail = klen - tq
        q2 = queries(h, rows)
        qsq = q2.astype(F32)
        qn2 = _dot_nt(jnp.ones((8, LANES), BF16), (qsq * qsq).astype(BF16))[0:1, :]
        shift = jnp.sqrt(qn2) * (_lane_tile(kmax_ref[...], 2 * tq) * SHIFT_MARGIN)
        st = _dot_nt(k_ref[0, h, 0:klen, :], q2) - shift
        e = jnp.exp2(jnp.where(diag, st[tail:klen], -jnp.inf))
        l = _col_reduce(e, jnp.sum)
        acc = _dot(vt_ref[0, h, :, tail:klen], e.astype(BF16))
        if tail:
            e = jnp.exp2(st[0:tail])
            l = l + _col_reduce(e, jnp.sum)
            acc = acc + _dot(vt_ref[0, h, :, 0:tail], e.astype(BF16))
        write(h, rows, acc, l)
        return jnp.maximum(underflow, jnp.where(jnp.min(l) >= DENOM_FLOOR, 0, 1))

    flag_ref[0] = 0
    for j in range(seq // (n_sub * tq)):
        @pl.when(step == j)
        def _(j=j):
            underflow = 0
            for i in range(n_sub):
                for h in range(B_HEADS):
                    underflow = attend(tq * (n_sub * j + i + 1), h, slice(tq * i, tq * (i + 1)), underflow)
            flag_ref[0] = underflow

    @pl.when(flag_ref[0] != 0)
    def _():
        kpos = lax.broadcasted_iota(jnp.int32, (seq, 2 * tq), 0)
        lane_q = lax.broadcasted_iota(jnp.int32, (seq, 2 * tq), 1) & (tq - 1)
        for i in range(n_sub):
            rows = slice(tq * i, tq * (i + 1))
            qpos = (step * n_sub + i) * tq + lane_q

            def redo(h, carry, rows=rows, qpos=qpos):
                st = jnp.where(kpos <= qpos, _dot_nt(k_ref[0, h], queries(h, rows)), -jnp.inf)
                e = jnp.exp2(st - _col_reduce(st, jnp.max))
                write(h, rows, _dot(vt_ref[0, h], e.astype(BF16)), _col_reduce(e, jnp.sum))
                return carry

            lax.fori_loop(0, B_HEADS, redo, 0)


def _diff(qb, kb, vbt, kmax, lq1, lk1, lq2, lk2, lambda_init):
    b, _, s, _ = qb.shape
    tq = DIFF_STEP_BLOCKS * DIFF_QBLK
    assert s % tq == 0
    vec = pl.BlockSpec((1, HEAD_DIM), lambda bi, qi_: (0, 0))
    keys = pl.BlockSpec((1, B_HEADS, s, LANES), lambda bi, qi_: (bi, 0, 0, 0))
    vals = pl.BlockSpec((1, B_HEADS, LANES, s), lambda bi, qi_: (bi, 0, 0, 0))
    blk = pl.BlockSpec((1, B_HEADS, tq, LANES), lambda bi, qi_: (bi, 0, qi_, 0))
    return pl.pallas_call(
        functools.partial(_diff_body, lambda_init=lambda_init),
        grid=(b, s // tq),
        in_specs=[blk, keys, vals, pl.BlockSpec((1, LANES), lambda bi, qi_: (0, 0)), vec, vec, vec, vec],
        out_specs=blk,
        out_shape=jax.ShapeDtypeStruct((b, B_HEADS, s, LANES), BF16),
        scratch_shapes=[pltpu.SMEM((1,), jnp.int32)],
        compiler_params=pltpu.CompilerParams(
            dimension_semantics=("arbitrary", "arbitrary"), vmem_limit_bytes=VMEM_LIMIT),
        name="diff_attention",
    )(qb, kb, vbt, kmax, lq1, lk1, lq2, lk2)


def _memkv_body(mem_ref, g_ref, wk_ref, wv_ref, gk_ref, k_ref, v_ref):
    mem = mem_ref[0]
    memn = (mem * _rms_scale(mem) * g_ref[...]).astype(BF16)
    k = _dot(memn, wk_ref[...])
    hd = gk_ref.shape[1]
    for h in range(k.shape[1] // hd):
        kh = k[:, hd * h:hd * (h + 1)]
        k_ref[0, :, hd * h:hd * (h + 1)] = (kh * _rms_scale(kh) * gk_ref[...]).astype(BF16)
    v_ref[0] = _dot(memn, wv_ref[...]).astype(BF16)


def _memkv(mem, g_mem, w_xk, w_xv, g_xk):
    b, m, d = mem.shape
    const = lambda bi: (0, 0)
    blk = pl.BlockSpec((1, m, d), lambda bi: (bi, 0, 0))
    return pl.pallas_call(
        _memkv_body,
        grid=(b,),
        in_specs=[blk, pl.BlockSpec((1, d), const), pl.BlockSpec((d, d), const),
                  pl.BlockSpec((d, d), const), pl.BlockSpec(g_xk.shape, const)],
        out_specs=[blk, blk],
        out_shape=[jax.ShapeDtypeStruct((b, m, d), BF16)] * 2,
        compiler_params=pltpu.CompilerParams(
            dimension_semantics=("arbitrary",), vmem_limit_bytes=VMEM_LIMIT),
        name="mem_kv",
    )(mem, g_mem, w_xk, w_xv, g_xk)


def _xattn_body(x_ref, oa_ref, ob_ref, wo_ref, g_ref, wq_ref, gq_ref, k_ref, v_ref, wxo_ref, h_ref):
    half = oa_ref.shape[2]
    ob = jnp.concatenate([ob_ref[0, h] for h in range(B_HEADS)], axis=1)
    h1 = x_ref[0] + _dot(oa_ref[0], wo_ref[0:half, :]) + _dot(ob, wo_ref[half:2 * half, :])
    hn = (h1 * _rms_scale(h1) * g_ref[...]).astype(BF16)
    q = _dot(hn, wq_ref[...])
    hd = gq_ref.shape[1]
    outs = []
    for h in range(q.shape[1] // hd):
        sl = slice(hd * h, hd * (h + 1))
        qh = q[:, sl]
        qh = (qh * _rms_scale(qh) * gq_ref[...] * (hd ** -0.5)).astype(BF16)
        s = _dot_nt(qh, k_ref[0, :, sl])
        e = jnp.exp(s - jnp.max(s, axis=-1, keepdims=True))
        p = (e * (1.0 / jnp.sum(e, axis=-1, keepdims=True))).astype(BF16)
        outs.append(_dot(p, v_ref[0, :, sl]).astype(BF16))
    o = jnp.concatenate(outs, axis=1)
    h_ref[0] = h1 + _dot(o, wxo_ref[...])


def _xattn(x, oa, ob, w_out, g_x, w_xq, g_xq, kmem, vmem, w_xo, tm):
    b, s, d = x.shape
    m = kmem.shape[1]
    const = lambda bi, ti: (0, 0)
    tok = lambda w: pl.BlockSpec((1, tm, w), lambda bi, ti: (bi, ti, 0))
    memblk = pl.BlockSpec((1, m, d), lambda bi, ti: (bi, 0, 0))
    return pl.pallas_call(
        _xattn_body,
        grid=(b, s // tm),
        in_specs=[tok(d), tok(oa.shape[2]),
                  pl.BlockSpec((1, B_HEADS, tm, LANES), lambda bi, ti: (bi, 0, ti, 0)),
                  pl.BlockSpec(w_out.shape, const), pl.BlockSpec((1, d), const),
                  pl.BlockSpec(w_xq.shape, const), pl.BlockSpec(g_xq.shape, const),
                  memblk, memblk, pl.BlockSpec(w_xo.shape, const)],
        out_specs=tok(d),
        out_shape=jax.ShapeDtypeStruct((b, s, d), F32),
        compiler_params=pltpu.CompilerParams(
            dimension_semantics=("arbitrary", "arbitrary"), vmem_limit_bytes=VMEM_LIMIT),
        name="outproj_xattn",
    )(x, oa, ob, w_out, g_x, w_xq, g_xq, kmem, vmem, w_xo)


HALO = 8


def _ffn_body(h_ref, g_ref, win_ref, cw_ref, cb_ref, wo_ref, o_ref, a_ref):
    tm = h_ref.shape[1]
    dff = wo_ref.shape[0]

    @pl.when(pl.program_id(1) == 0)
    def _():
        a_ref[0:HALO, :] = jnp.zeros((HALO, a_ref.shape[1]), F32)

    h = h_ref[0]
    hn = (h * _rms_scale(h) * g_ref[...]).astype(BF16)
    a_ref[HALO:HALO + tm, :] = _dot(hn, win_ref[:, 0:dff])
    gate = _dot(hn, win_ref[:, dff:2 * dff])
    conv = cb_ref[...]
    for j in range(CONV_W):
        off = HALO - (CONV_W - 1) + j
        conv = conv + a_ref[off:off + tm, :] * cw_ref[j:j + 1, :]
    a_ref[0:HALO, :] = a_ref[tm:tm + HALO, :]
    u = (jax.nn.gelu(conv) * gate).astype(BF16)
    o_ref[0] = h + _dot(u, wo_ref[...])


def _ffn(h, g_ffn, w_in, conv_w, conv_b, w_o, tm):
    b, s, d = h.shape
    dff = w_o.shape[0]
    const = lambda bi, ti: (0, 0)
    tok = pl.BlockSpec((1, tm, d), lambda bi, ti: (bi, ti, 0))
    return pl.pallas_call(
        _ffn_body,
        grid=(b, s // tm),
        in_specs=[tok, pl.BlockSpec((1, d), const), _const_spec(w_in.shape),
                  pl.BlockSpec(conv_w.shape, const),
                  pl.BlockSpec((1, dff), const), _const_spec(w_o.shape)],
        out_specs=tok,
        out_shape=jax.ShapeDtypeStruct((b, s, d), F32),
        scratch_shapes=[pltpu.VMEM((tm + HALO, dff), F32)],
        compiler_params=pltpu.CompilerParams(
            dimension_semantics=("arbitrary", "arbitrary"), vmem_limit_bytes=VMEM_LIMIT),
        name="conv_glu",
    )(h, g_ffn, w_in, conv_w, conv_b, w_o)


def _rearranged_w_in(w_in):
    sizes = (A_HEADS * HEAD_DIM, HEAD_DIM, HEAD_DIM, IDX_HEADS * HEAD_DIM, HEAD_DIM, IDX_HEADS,
             2 * B_HEADS * HEAD_DIM, 2 * B_HEADS * HEAD_DIM, B_HEADS * 2 * HEAD_DIM)
    offs = [0]
    for sz in sizes:
        offs.append(offs[-1] + sz)
    q_a, k_a, v_a, q_i, k_i, w_i, q_b, k_b, v_b = [w_in[:, offs[i]:offs[i + 1]] for i in range(9)]
    pad = jnp.zeros((w_in.shape[0], HEAD_DIM - IDX_HEADS), w_in.dtype)
    w_all = jnp.concatenate([q_a, q_i, k_a, k_a, k_i, k_i, v_a, w_i, pad, q_b, k_b], axis=1)
    assert w_all.shape[1] == _C_END
    w_vt = jnp.concatenate([v_b, v_a, jnp.zeros((w_in.shape[0], HEAD_DIM), w_in.dtype)], axis=1).T
    return w_all.astype(BF16), w_vt.astype(BF16)


def kernel(x, mem, positions, g_mix, w_in, g_qa, g_ka, g_qb, g_kb, lam_q1, lam_k1, lam_q2, lam_k2,
           w_out, g_xattn, g_mem, w_xq, w_xk, w_xv, w_xo, g_xq, g_xk, g_ffn, w_ffn_in, conv_w, conv_b,
           w_ffn_out):
    b, s, d = x.shape
    depth = g_mix.shape[0]
    topk = min(TOPK_MAX, s // 4)
    tm = min(512, s)
    tm_in = min(1024, s)

    inv_freq = 1.0 / (ROPE_THETA ** (jnp.arange(0, HEAD_DIM, 2, dtype=F32) / HEAD_DIM))
    pos = positions.reshape(b * s // tm_in, 4, tm_in // 4).transpose(0, 2, 1).reshape(b * s // 4, 4)
    ang = (pos.astype(F32)[:, :, None] * inv_freq).reshape(b * s // 4, LANES)
    cos_d, sin_d = jnp.cos(ang), jnp.sin(ang)
    sign = jnp.tile(jnp.repeat(jnp.array([-1.0, 1.0], F32), HEAD_DIM // 2), 512 // HEAD_DIM)
    blk = jnp.arange(MXU_DIM) // HEAD_DIM
    bd = jnp.where(blk[:, None] == blk[None, :], 1.0 / HEAD_DIM, 0.0).astype(BF16)

    h = x
    for l in range(depth):
        lambda_init = 0.8 - 0.6 * math.exp(-0.3 * l)
        gains = jnp.stack([jnp.tile(g, 512 // HEAD_DIM) for g in (g_qa[l], g_ka[l], g_qb[l], g_kb[l])])
        gains = jnp.concatenate([gains, sign[None, :], jnp.ones((3, 512), F32)], axis=0)
        w_all, w_vt = _rearranged_w_in(w_in[l])
        qa, qi, ks, wi, qb, kb, vbt, vat = _inproj(
            h.reshape(b * s, d), g_mix[l][None, :], w_all, w_vt, cos_d, sin_d, bd, gains, tm_in, s)
        r3 = lambda t: t.reshape(b, s, t.shape[-1])
        key_bound = lambda g: jnp.full((1, LANES), HEAD_DIM ** 0.5, F32) * jnp.max(jnp.abs(g))
        out_a = _dsa(r3(qa), r3(qi), r3(wi), r3(ks), vat, key_bound(g_ka[l]), topk, n_bisect=16)
        out_b = _diff(qb, kb, vbt, key_bound(g_kb[l]), lam_q1[l][None, :], lam_k1[l][None, :],
                      lam_q2[l][None, :], lam_k2[l][None, :], lambda_init)
        kmem, vmem = _memkv(mem, g_mem[l][None, :], w_xk[l].astype(BF16), w_xv[l].astype(BF16),
                            g_xk[l][None, :])
        h = _xattn(h, out_a, out_b, w_out[l].astype(BF16), g_xattn[l][None, :], w_xq[l].astype(BF16),
                   g_xq[l][None, :], kmem, vmem, w_xo[l].astype(BF16), min(1024, s))
        dff = w_ffn_out.shape[1]
        cw = jnp.concatenate([conv_w[l], jnp.zeros((8 - CONV_W, dff), F32)], axis=0)
        h = _ffn(h, g_ffn[l][None, :], w_ffn_in[l].astype(BF16), cw, conv_b[l][None, :],
                 w_ffn_out[l].astype(BF16), tm)
    return h
```

```python
import functools
import math

import jax
import jax.numpy as jnp
from jax import lax
from jax.experimental import pallas as pl
from jax.experimental.pallas import tpu as pltpu

F32 = jnp.float32
BF16 = jnp.bfloat16

EPS = 1e-6
ROPE_THETA = 10000.0
HEAD_DIM = 64
A_HEADS = 8
IDX_HEADS = 4
TOPK_MAX = 256
B_HEADS = 4
X_HEADS = 4
CONV_W = 3
LANES = 128
MXU_DIM = 256
DSA_QBLK = 128
DSA_PAIR = 2
DSA_STEP_PAIRS = 2
DIFF_QBLK = 256
DIFF_STEP_BLOCKS = 2
CAUSAL_STEP = MXU_DIM
VMEM_LIMIT = 56 * 1024 * 1024
LOG2E = 1.4426950408889634
SHIFT_MARGIN = 1.02
DENOM_FLOOR = 2.0 ** -40

_C_QA = 0
_C_QI = 512
_C_KS = 768
_C_QB = 1152
_C_KB = 1664
_C_END = 2176


def _dot(a, b):
    return jnp.dot(a, b, preferred_element_type=F32)


def _dot_nt(a, b):
    return lax.dot_general(a, b, (((1,), (1,)), ((), ())), preferred_element_type=F32)


def _rms_scale(x):
    return lax.rsqrt(jnp.mean(x * x, axis=-1, keepdims=True) + EPS)


def _const_spec(shape):
    zeros = (0,) * len(shape)
    return pl.BlockSpec(shape, lambda *_: zeros, pipeline_mode=pl.Buffered(1))


def _lane_tile(t, width):
    reps = width // t.shape[1]
    return t if reps == 1 else jnp.concatenate([t] * reps, axis=1)


def _col_reduce(x, reduce_fn):
    rows, cols = x.shape
    slab = 8 * max(1, 8 * LANES // cols)
    if rows % slab or rows == slab:
        return reduce_fn(x, axis=0, keepdims=True)
    part = reduce_fn(x.reshape(rows // slab, slab, cols), axis=0)
    return reduce_fn(part, axis=0, keepdims=True)


def _inproj_body(x_ref, gmix_ref, w_ref, wvt_ref, cos_ref, sin_ref, bd_ref, gains_ref,
                 qa_ref, qi_ref, ks_ref, wi_ref, qb_ref, kb_ref, vbt_ref, vat_ref):
    x = x_ref[...]
    hn = (x * _rms_scale(x) * gmix_ref[...]).astype(BF16)

    def spread(t):
        turned = [t] + [pltpu.roll(t, 32 * k, 1) for k in range(1, 4)]
        group = lax.broadcasted_iota(jnp.int32, t.shape, 1) >> 5
        parts = []
        for j in range(4):
            d = (group - j) & 3
            parts.append(jnp.where(d == 0, turned[0], jnp.where(d == 1, turned[1],
                                                               jnp.where(d == 2, turned[2], turned[3]))))
        return jnp.concatenate(parts, axis=0)

    cos = spread(cos_ref[...])
    sin = spread(sin_ref[...]) * gains_ref[4:5, 0:LANES]
    bd = bd_ref[...]

    def proj(c0, width):
        return _dot(hn, w_ref[:, c0:c0 + width])

    def group_rms_scale(p):
        sq = (p * p).astype(BF16)
        outs = []
        for j in range(p.shape[1] // MXU_DIM):
            outs.append(_dot(sq[:, MXU_DIM * j:MXU_DIM * (j + 1)], bd))
        ms = outs[0] if len(outs) == 1 else jnp.concatenate(outs, axis=1)
        return lax.rsqrt(ms + EPS)

    def rope(y):
        width = y.shape[1]
        lane = lax.broadcasted_iota(jnp.int32, y.shape, 1)
        first_half = (lane & (HEAD_DIM - 1)) < (HEAD_DIM // 2)
        swapped = jnp.where(first_half, pltpu.roll(y, width - HEAD_DIM // 2, 1),
                            pltpu.roll(y, HEAD_DIM // 2, 1))
        return y * _lane_tile(cos, width) + swapped * _lane_tile(sin, width)

    sm_scale = HEAD_DIM ** -0.5 * LOG2E

    p = proj(_C_QA, 512)
    qa_ref[...] = (rope(p * group_rms_scale(p) * gains_ref[0:1, :]) * sm_scale).astype(BF16)

    p = proj(_C_QI, 256)
    qi_ref[...] = rope(p).astype(BF16)

    p = proj(_C_KS, 384)
    p01 = p[:, 0:256]
    lane = lax.broadcasted_iota(jnp.int32, p01.shape, 1)
    y01 = jnp.where(lane < LANES, p01 * group_rms_scale(p01) * gains_ref[1:2, 0:256], p01)
    y01 = rope(y01)
    p2 = p[:, 256:384]
    lane = lax.broadcasted_iota(jnp.int32, p2.shape, 1)
    p2 = jnp.where(lane < HEAD_DIM, p2, p2 * (IDX_HEADS ** -0.5 * HEAD_DIM ** -0.5))
    ks_ref[:, 0:256] = y01.astype(BF16)
    ks_ref[:, 256:384] = p2.astype(BF16)
    wi_ref[...] = p2

    def store_heads(ref, y):
        for h in range(B_HEADS):
            ref[0, h] = y[:, LANES * h:LANES * (h + 1)].astype(BF16)

    p = proj(_C_QB, 512)
    store_heads(qb_ref, rope(p * group_rms_scale(p) * gains_ref[2:3, :]) * sm_scale)

    p = proj(_C_KB, 512)
    store_heads(kb_ref, rope(p * group_rms_scale(p) * gains_ref[3:4, :]))

    vt = _dot_nt(wvt_ref[...], hn)
    for h in range(B_HEADS):
        vbt_ref[0, h] = vt[LANES * h:LANES * (h + 1), :].astype(BF16)
    vat_ref[0] = vt[LANES * B_HEADS:LANES * (B_HEADS + 1), :].astype(BF16)


def _inproj(x2, gmix, w_all, w_vt, cos128, sin128, bd, gains, tm, seq):
    n, d = x2.shape
    row = lambda i: (i, 0)
    const = lambda i: (0, 0)
    outs = [(512, BF16), (256, BF16), (384, BF16), (LANES, F32)]
    tiles = seq // tm
    head_spec = pl.BlockSpec((1, B_HEADS, tm, LANES), lambda i: (i // tiles, 0, i % tiles, 0))
    head_shape = jax.ShapeDtypeStruct((n // seq, B_HEADS, seq, LANES), BF16)
    vbt_spec = pl.BlockSpec((1, B_HEADS, LANES, tm), lambda i: (i // tiles, 0, 0, i % tiles))
    vbt_shape = jax.ShapeDtypeStruct((n // seq, B_HEADS, LANES, seq), BF16)
    vat_spec = pl.BlockSpec((1, LANES, tm), lambda i: (i // tiles, 0, i % tiles))
    vat_shape = jax.ShapeDtypeStruct((n // seq, LANES, seq), BF16)
    return pl.pallas_call(
        _inproj_body,
        grid=(n // tm,),
        in_specs=[
            pl.BlockSpec((tm, d), row),
            pl.BlockSpec((1, d), const),
            pl.BlockSpec(w_all.shape, const),
            pl.BlockSpec(w_vt.shape, const),
            pl.BlockSpec((tm // 4, LANES), row),
            pl.BlockSpec((tm // 4, LANES), row),
            pl.BlockSpec(bd.shape, const),
            pl.BlockSpec(gains.shape, const),
        ],
        out_specs=([pl.BlockSpec((tm, w), row) for w, _ in outs]
                   + [head_spec, head_spec, vbt_spec, vat_spec]),
        out_shape=([jax.ShapeDtypeStruct((n, w), dt) for w, dt in outs]
                   + [head_shape, head_shape, vbt_shape, vat_shape]),
        compiler_params=pltpu.CompilerParams(
            dimension_semantics=("arbitrary",), vmem_limit_bytes=VMEM_LIMIT),
        name="inproj",
    )(x2, gmix, w_all, w_vt, cos128, sin128, bd, gains)


def _group_reduce(x, reduce_fn):
    g, rows, cols = x.shape
    slab = 8 * max(1, 8 * LANES // cols)
    if rows % slab or rows == slab:
        return reduce_fn(x, axis=1, keepdims=True)
    combine = {jnp.sum: jnp.add, jnp.max: jnp.maximum, jnp.min: jnp.minimum}[reduce_fn]
    part = x[:, 0:slab]
    for i in range(1, rows // slab):
        part = combine(part, x[:, i * slab:(i + 1) * slab])
    return reduce_fn(part, axis=1, keepdims=True)


def _dsa_queries(qa_ref, rows):
    qa = qa_ref[0, rows, :]
    lane = lax.broadcasted_iota(jnp.int32, (DSA_QBLK, LANES), 1)
    rows = []
    for h in range(A_HEADS):
        slab = qa[:, LANES * (h // 2):LANES * (h // 2 + 1)]
        keep = (lane >= HEAD_DIM) if h % 2 else (lane < HEAD_DIM)
        rows.append(jnp.where(keep, slab, jnp.zeros_like(slab)))
    return jnp.concatenate(rows, axis=0)


def _dsa_write(ot, l, o_ref, ot_ref, rows):
    tq = DSA_QBLK
    ot = ot * (1.0 / l)
    for h in range(A_HEADS):
        ot_ref[HEAD_DIM * h:HEAD_DIM * (h + 1), :] = ot[:, tq * h:tq * (h + 1)]
    o_ref[0, rows, :] = ot_ref[...].T.astype(BF16)


def _dsa_rows(block):
    if isinstance(block, int):
        return pl.ds(block * DSA_QBLK, DSA_QBLK)
    return pl.ds(pl.multiple_of(block * DSA_QBLK, DSA_QBLK), DSA_QBLK)


def _dsa_keys(klen, pair, sub, qa_ref, qi_ref, wi_ref, ks_ref, o_ref, vt_ref, sc_ref, bias_ref, ot_ref,
              kmax_ref, flag_ref, topk, n_bisect):
    tq = DSA_QBLK
    seq = ks_ref.shape[1]
    tail = klen - CAUSAL_STEP
    neg_inf = -jnp.inf

    lane_q = lax.broadcasted_iota(jnp.int32, (tq, LANES), 1)
    first_head = lane_q < HEAD_DIM

    def head_rows(slab, odd):
        return jnp.where(first_head != odd, slab, jnp.zeros_like(slab))

    shape_t = (DSA_PAIR, CAUSAL_STEP, tq)
    qpos_t = ((pair * DSA_PAIR + lax.broadcasted_iota(jnp.int32, shape_t, 0)) * tq
              + lax.broadcasted_iota(jnp.int32, shape_t, 2))
    causal_t = tail + lax.broadcasted_iota(jnp.int32, shape_t, 1) <= qpos_t

    def indexer(blk, carry):
        rows = _dsa_rows(sub * DSA_PAIR + blk)
        qi = qi_ref[0, rows, :]
        qi_stack = jnp.concatenate(
            [head_rows(qi[:, LANES * (h // 2):LANES * (h // 2 + 1)], bool(h % 2)) for h in range(IDX_HEADS)],
            axis=0)
        lg = _dot_nt(ks_ref[0, 0:klen, 128:256], qi_stack)
        w_t = wi_ref[0, rows, :].T
        sc = None
        for h in range(IDX_HEADS):
            term = jnp.maximum(lg[:, tq * h:tq * (h + 1)], 0.0) * w_t[HEAD_DIM + h:HEAD_DIM + h + 1, :]
            sc = term if sc is None else sc + term
        sc_ref[blk, 0:klen, :] = sc
        return carry

    lax.fori_loop(0, DSA_PAIR, indexer, 0)
    sc_ref[:, tail:klen, :] = jnp.where(causal_t, sc_ref[:, tail:klen, :], neg_inf)

    kf = float(topk)
    search = qpos_t[:, 0:1, :] >= topk

    slab = 64

    def fold(per_slab, combine, reduce_fn, start=0, stop=klen):
        acc = per_slab(sc_ref[:, start:start + slab, :], start)
        for r in range(start + slab, stop, slab):
            acc = combine(acc, per_slab(sc_ref[:, r:r + slab, :], r))
        return reduce_fn(acc, axis=1, keepdims=True)

    def count(pred):
        return fold(lambda s, _: jnp.where(pred(s), 1.0, 0.0), jnp.add, jnp.sum)

    hi0 = fold(lambda s, _: s, jnp.maximum, jnp.max)
    lo0 = fold(lambda s, r: jnp.where(causal_t[:, r - tail:r - tail + slab], s, jnp.inf),
               jnp.minimum, jnp.min, start=tail)
    if tail:
        lo0 = jnp.minimum(lo0, fold(lambda s, _: s, jnp.minimum, jnp.min, stop=tail))

    def bisect(_, carry):
        lo, hi = carry
        mid = 0.5 * (lo + hi)
        ge = count(lambda s: s >= mid) >= kf
        return jnp.where(ge, mid, lo), jnp.where(ge, hi, mid)

    lo, _ = lax.fori_loop(0, n_bisect, bisect, (lo0, hi0))

    def too_low(n_gt):
        return jnp.max(jnp.where(jnp.logical_and(search, n_gt >= kf), 1.0, 0.0))

    def climb(carry):
        thr, n_gt, _ = carry
        nxt = fold(lambda s, _: jnp.where(s > thr, s, jnp.inf), jnp.minimum, jnp.min)
        thr = jnp.where(jnp.logical_and(search, n_gt >= kf), nxt, thr)
        n_gt = count(lambda s: s > thr)
        return thr, n_gt, too_low(n_gt)

    thr0 = fold(lambda s, _: jnp.where(s >= lo, s, jnp.inf), jnp.minimum, jnp.min)
    n_gt0 = count(lambda s: s > thr0)
    thr, n_gt, _ = lax.while_loop(lambda c: c[2] > 0.0, climb, (thr0, n_gt0, too_low(n_gt0)))

    need = kf - n_gt
    open_row = jnp.where(search, neg_inf, 0.0)
    n_blk = klen // MXU_DIM
    tie_cat = jnp.concatenate(
        [jnp.where(sc_ref[g, MXU_DIM * j:MXU_DIM * (j + 1), :] == thr[g], 1.0, 0.0).astype(BF16)
         for g in range(DSA_PAIR) for j in range(n_blk)], axis=1)
    tri = jnp.where(lax.broadcasted_iota(jnp.int32, (MXU_DIM, MXU_DIM), 0)
                    >= lax.broadcasted_iota(jnp.int32, (MXU_DIM, MXU_DIM), 1), 1.0, 0.0).astype(BF16)
    prefix = _dot(tri, tie_cat)
    for g in range(DSA_PAIR):
        before = jnp.zeros((1, tq), F32)
        for j in range(n_blk):
            rows = slice(MXU_DIM * j, MXU_DIM * (j + 1))
            col = (g * n_blk + j) * tq
            rank = prefix[:, col:col + tq] + before
            before = before + prefix[MXU_DIM - 1:MXU_DIM, col:col + tq]
            sj = sc_ref[g, rows, :]
            admitted = jnp.where(sj == thr[g], jnp.where(rank <= need[g], 0.0, neg_inf), neg_inf)
            bias = jnp.maximum(jnp.where(sj > thr[g], 0.0, admitted), open_row[g])
            if j == n_blk - 1:
                bias = jnp.where(causal_t[g], bias, neg_inf)
            bias_ref[g, rows, :] = bias

    def attend(blk, underflow):
        rows = _dsa_rows(sub * DSA_PAIR + blk)
        q_all = _dsa_queries(qa_ref, rows)
        qsq = q_all.astype(F32)
        qn2 = _dot_nt(jnp.ones((8, LANES), BF16), (qsq * qsq).astype(BF16))[0:1, :]
        shift = jnp.sqrt(qn2) * (_lane_tile(kmax_ref[...], A_HEADS * tq) * SHIFT_MARGIN)
        kl = klen - tq * (DSA_PAIR - 1 - blk)
        bias = bias_ref[blk, 0:kl, :]
        st = _dot_nt(ks_ref[0, 0:kl, 0:128], q_all) + jnp.concatenate([bias] * A_HEADS, axis=1) - shift
        e = jnp.exp2(st)
        l = _col_reduce(e, jnp.sum)
        _dsa_write(_dot(vt_ref[0, 0:HEAD_DIM, 0:kl], e.astype(BF16)), l, o_ref, ot_ref, rows)
        return jnp.maximum(underflow, jnp.where(jnp.min(l) >= DENOM_FLOOR, 0, 1))

    underflow = 0
    for blk in range(DSA_PAIR):
        underflow = attend(blk, underflow)

    @pl.when(underflow != 0)
    def _():
        if klen < seq:
            bias_ref[:, klen:seq, :] = jnp.full((DSA_PAIR, seq - klen, tq), neg_inf, F32)
        flag_ref[sub] = 1


def _dsa_body(qa_ref, qi_ref, wi_ref, ks_ref, vt_ref, kmax_ref, o_ref, sc_ref, bias_ref, ot_ref,
              flag_ref, *, topk, n_bisect):
    step = pl.program_id(1)
    seq = ks_ref.shape[1]
    for sub in range(DSA_STEP_PAIRS):
        flag_ref[sub] = 0

    for j in range(seq // (CAUSAL_STEP * DSA_STEP_PAIRS)):
        @pl.when(step == j)
        def _(j=j):
            for sub in range(DSA_STEP_PAIRS):
                c = DSA_STEP_PAIRS * j + sub
                _dsa_keys(CAUSAL_STEP * (c + 1), c, sub, qa_ref, qi_ref, wi_ref, ks_ref, o_ref, vt_ref,
                          sc_ref, bias_ref.at[sub], ot_ref, kmax_ref, flag_ref, topk, n_bisect)

    for sub in range(DSA_STEP_PAIRS):
        @pl.when(flag_ref[sub] != 0)
        def _(sub=sub):
            def redo(blk, carry):
                rows = _dsa_rows(sub * DSA_PAIR + blk)
                bias = bias_ref[sub, blk]
                st = (_dot_nt(ks_ref[0, :, 0:128], _dsa_queries(qa_ref, rows))
                      + jnp.concatenate([bias] * A_HEADS, axis=1))
                e = jnp.exp2(st - _col_reduce(st, jnp.max))
                _dsa_write(_dot(vt_ref[0, 0:HEAD_DIM, :], e.astype(BF16)), _col_reduce(e, jnp.sum), o_ref,
                           ot_ref, rows)
                return carry

            lax.fori_loop(0, DSA_PAIR, redo, 0)


def _dsa(qa, qi, wi, ks, vat, kmax, topk, n_bisect):
    b, s, _ = qa.shape
    tq = DSA_QBLK
    assert DSA_PAIR * tq == CAUSAL_STEP
    step = DSA_STEP_PAIRS * CAUSAL_STEP
    assert s % step == 0
    blk = lambda bi, qi_: (bi, qi_, 0)
    return pl.pallas_call(
        functools.partial(_dsa_body, topk=topk, n_bisect=n_bisect),
        grid=(b, s // step),
        in_specs=[
            pl.BlockSpec((1, step, 512), blk),
            pl.BlockSpec((1, step, 256), blk),
            pl.BlockSpec((1, step, LANES), blk),
            pl.BlockSpec((1, s, 384), lambda bi, qi_: (bi, 0, 0)),
            pl.BlockSpec((1, LANES, s), lambda bi, qi_: (bi, 0, 0)),
            pl.BlockSpec((1, LANES), lambda bi, qi_: (0, 0)),
        ],
        out_specs=pl.BlockSpec((1, step, 512), blk),
        out_shape=jax.ShapeDtypeStruct((b, s, 512), BF16),
        scratch_shapes=[
            pltpu.VMEM((DSA_PAIR, s, tq), F32),
            pltpu.VMEM((DSA_STEP_PAIRS, DSA_PAIR, s, tq), F32),
            pltpu.VMEM((A_HEADS * HEAD_DIM, tq), F32),
            pltpu.SMEM((DSA_STEP_PAIRS,), jnp.int32),
        ],
        compiler_params=pltpu.CompilerParams(
            dimension_semantics=("arbitrary", "arbitrary"), vmem_limit_bytes=VMEM_LIMIT),
        name="dsa_attention",
    )(qa, qi, wi, ks, vat, kmax)


def _diff_body(q_ref, k_ref, vt_ref, kmax_ref, lq1_ref, lk1_ref, lq2_ref, lk2_ref, o_ref,
               flag_ref, *, lambda_init):
    step = pl.program_id(1)
    seq = k_ref.shape[2]
    n_sub = q_ref.shape[2] // DIFF_QBLK
    tq = q_ref.shape[2] // n_sub

    lam =(jnp.exp(jnp.sum(lq1_ref[...] * lk1_ref[...], axis=1, keepdims=True))
           - jnp.exp(jnp.sum(lq2_ref[...] * lk2_ref[...], axis=1, keepdims=True)) + lambda_init)
    lane = lax.broadcasted_iota(jnp.int32, (tq, LANES), 1)
    diag = (lax.broadcasted_iota(jnp.int32, (tq, 2 * tq), 0)
            <= (lax.broadcasted_iota(jnp.int32, (tq, 2 * tq), 1) & (tq - 1)))

    def queries(h, rows):
        q = q_ref[0, h, rows, :]
        zero = jnp.zeros_like(q)
        return jnp.concatenate([jnp.where(lane < HEAD_DIM, q, zero), jnp.where(lane >= HEAD_DIM, q, zero)],
                               axis=0)

    def write(h, rows, acc, l):
        acc = acc * (1.0 / l)
        ot = acc[:, 0:tq] - lam * acc[:, tq:2 * tq]
        ot = ot * lax.rsqrt(jnp.mean(ot * ot, axis=0, keepdims=True) + EPS) * (1.0 - lambda_init)
        o_ref[0, h, rows, :] = ot.T.astype(BF16)

    def attend(klen, h, rows, underflow):
        tail = klen - tq
        q2 = queries(h, rows)
        qsq = q2.astype(F32)
        qn2 = _dot_nt(jnp.ones((8, LANES), BF16), (qsq * qsq).astype(BF16))[0:1, :]
        shift = jnp.sqrt(qn2) * (_lane_tile(kmax_ref[...], 2 * tq) * SHIFT_MARGIN)
        st = _dot_nt(k_ref[0, h, 0:klen, :], q2) - shift
        e = jnp.exp2(jnp.where(diag, st[tail:klen], -jnp.inf))
        l = _col_reduce(e, jnp.sum)
        acc = _dot(vt_ref[0, h, :, tail:klen], e.astype(BF16))
        if tail:
            e = jnp.exp2(st[0:tail])
            l = l + _col_reduce(e, jnp.sum)
            acc = acc + _dot(vt_ref[0, h, :, 0:tail], e.astype(BF16))
        write(h, rows, acc, l)
        return jnp.maximum(underflow, jnp.where(jnp.min(l) >= DENOM_FLOOR, 0, 1))

    flag_ref[0] = 0
    for j in range(seq // (n_sub * tq)):
        @pl.when(step == j)
        def _(j=j):
            underflow = 0
            for i in range(n_sub):
                for h in range(B_HEADS):
                    underflow = attend(tq * (n_sub * j + i + 1), h, slice(tq * i, tq * (i + 1)), underflow)
            flag_ref[0] = underflow

    @pl.when(flag_ref[0] != 0)
    def _():
        kpos = lax.broadcasted_iota(jnp.int32, (seq, 2 * tq), 0)
        lane_q = lax.broadcasted_iota(jnp.int32, (seq, 2 * tq), 1) & (tq - 1)
        for i in range(n_sub):
            rows = slice(tq * i, tq * (i + 1))
            qpos = (step * n_sub + i) * tq + lane_q

            def redo(h, carry, rows=rows, qpos=qpos):
                st = jnp.where(kpos <= qpos, _dot_nt(k_ref[0, h], queries(h, rows)), -jnp.inf)
                e = jnp.exp2(st - _col_reduce(st, jnp.max))
                write(h, rows, _dot(vt_ref[0, h], e.astype(BF16)), _col_reduce(e, jnp.sum))
                return carry

            lax.fori_loop(0, B_HEADS, redo, 0)


def _diff(qb, kb, vbt, kmax, lq1, lk1, lq2, lk2, lambda_init):
    b, _, s, _ = qb.shape
    tq = DIFF_STEP_BLOCKS * DIFF_QBLK
    assert s % tq == 0
    vec = pl.BlockSpec((1, HEAD_DIM), lambda bi, qi_: (0, 0))
    keys = pl.BlockSpec((1, B_HEADS, s, LANES), lambda bi, qi_: (bi, 0, 0, 0))
    vals = pl.BlockSpec((1, B_HEADS, LANES, s), lambda bi, qi_: (bi, 0, 0, 0))
    blk = pl.BlockSpec((1, B_HEADS, tq, LANES), lambda bi, qi_: (bi, 0, qi_, 0))
    return pl.pallas_call(
        functools.partial(_diff_body, lambda_init=lambda_init),
        grid=(b, s // tq),
        in_specs=[blk, keys, vals, pl.BlockSpec((1, LANES), lambda bi, qi_: (0, 0)), vec, vec, vec, vec],
        out_specs=blk,
        out_shape=jax.ShapeDtypeStruct((b, B_HEADS, s, LANES), BF16),
        scratch_shapes=[pltpu.SMEM((1,), jnp.int32)],
        compiler_params=pltpu.CompilerParams(
            dimension_semantics=("arbitrary", "arbitrary"), vmem_limit_bytes=VMEM_LIMIT),
        name="diff_attention",
    )(qb, kb, vbt, kmax, lq1, lk1, lq2, lk2)


def _memkv_body(mem_ref, g_ref, wk_ref, wv_ref, gk_ref, k_ref, v_ref):
    mem = mem_ref[0]
    memn = (mem * _rms_scale(mem) * g_ref[...]).astype(BF16)
    k = _dot(memn, wk_ref[...])
    hd = gk_ref.shape[1]
    for h in range(k.shape[1] // hd):
        kh = k[:, hd * h:hd * (h + 1)]
        k_ref[0, :, hd * h:hd * (h + 1)] = (kh * _rms_scale(kh) * gk_ref[...]).astype(BF16)
    v_ref[0] = _dot(memn, wv_ref[...]).astype(BF16)


def _memkv(mem, g_mem, w_xk, w_xv, g_xk):
    b, m, d = mem.shape
    const = lambda bi: (0, 0)
    blk = pl.BlockSpec((1, m, d), lambda bi: (bi, 0, 0))
    return pl.pallas_call(
        _memkv_body,
        grid=(b,),
        in_specs=[blk, pl.BlockSpec((1, d), const), pl.BlockSpec((d, d), const),
                  pl.BlockSpec((d, d), const), pl.BlockSpec(g_xk.shape, const)],
        out_specs=[blk, blk],
        out_shape=[jax.ShapeDtypeStruct((b, m, d), BF16)] * 2,
        compiler_params=pltpu.CompilerParams(
            dimension_semantics=("arbitrary",), vmem_limit_bytes=VMEM_LIMIT),
        name="mem_kv",
    )(mem, g_mem, w_xk, w_xv, g_xk)


def _xattn_body(x_ref, oa_ref, ob_ref, wo_ref, g_ref, wq_ref, gq_ref, k_ref, v_ref, wxo_ref, h_ref):
    half = oa_ref.shape[2]
    ob = jnp.concatenate([ob_ref[0, h] for h in range(B_HEADS)], axis=1)
    h1 = x_ref[0] + _dot(oa_ref[0], wo_ref[0:half, :]) + _dot(ob, wo_ref[half:2 * half, :])
    hn = (h1 * _rms_scale(h1) * g_ref[...]).astype(BF16)
    q = _dot(hn, wq_ref[...])
    hd = gq_ref.shape[1]
    outs = []
    for h in range(q.shape[1] // hd):
        sl = slice(hd * h, hd * (h + 1))
        qh = q[:, sl]
        qh = (qh * _rms_scale(qh) * gq_ref[...] * (hd ** -0.5)).astype(BF16)
        s = _dot_nt(qh, k_ref[0, :, sl])
        e = jnp.exp(s - jnp.max(s, axis=-1, keepdims=True))
        p = (e * (1.0 / jnp.sum(e, axis=-1, keepdims=True))).astype(BF16)
        outs.append(_dot(p, v_ref[0, :, sl]).astype(BF16))
    o = jnp.concatenate(outs, axis=1)
    h_ref[0] = h1 + _dot(o, wxo_ref[...])


def _xattn(x, oa, ob, w_out, g_x, w_xq, g_xq, kmem, vmem, w_xo, tm):
    b, s, d = x.shape
    m = kmem.shape[1]
    const = lambda bi, ti: (0, 0)
    tok = lambda w: pl.BlockSpec((1, tm, w), lambda bi, ti: (bi, ti, 0))
    memblk = pl.BlockSpec((1, m, d), lambda bi, ti: (bi, 0, 0))
    return pl.pallas_call(
        _xattn_body,
        grid=(b, s // tm),
        in_specs=[tok(d), tok(oa.shape[2]),
                  pl.BlockSpec((1, B_HEADS, tm, LANES), lambda bi, ti: (bi, 0, ti, 0)),
                  pl.BlockSpec(w_out.shape, const), pl.BlockSpec((1, d), const),
                  pl.BlockSpec(w_xq.shape, const), pl.BlockSpec(g_xq.shape, const),
                  memblk, memblk, pl.BlockSpec(w_xo.shape, const)],
        out_specs=tok(d),
        out_shape=jax.ShapeDtypeStruct((b, s, d), F32),
        compiler_params=pltpu.CompilerParams(
            dimension_semantics=("arbitrary", "arbitrary"), vmem_limit_bytes=VMEM_LIMIT),
        name="outproj_xattn",
    )(x, oa, ob, w_out, g_x, w_xq, g_xq, kmem, vmem, w_xo)


HALO = 8


def _ffn_body(h_ref, g_ref, win_ref, cw_ref, cb_ref, wo_ref, o_ref, a_ref):
    tm = h_ref.shape[1]
    dff = wo_ref.shape[0]

    @pl.when(pl.program_id(1) == 0)
    def _():
        a_ref[0:HALO, :] = jnp.zeros((HALO, a_ref.shape[1]), F32)

    h = h_ref[0]
    hn = (h * _rms_scale(h) * g_ref[...]).astype(BF16)
    a_ref[HALO:HALO + tm, :] = _dot(hn, win_ref[:, 0:dff])
    gate = _dot(hn, win_ref[:, dff:2 * dff])
    conv = cb_ref[...]
    for j in range(CONV_W):
        off = HALO - (CONV_W - 1) + j
        conv = conv + a_ref[off:off + tm, :] * cw_ref[j:j + 1, :]
    a_ref[0:HALO, :] = a_ref[tm:tm + HALO, :]
    u = (jax.nn.gelu(conv) * gate).astype(BF16)
    o_ref[0] = h + _dot(u, wo_ref[...])


def _ffn(h, g_ffn, w_in, conv_w, conv_b, w_o, tm):
    b, s, d = h.shape
    dff = w_o.shape[0]
    const = lambda bi, ti: (0, 0)
    tok = pl.BlockSpec((1, tm, d), lambda bi, ti: (bi, ti, 0))
    return pl.pallas_call(
        _ffn_body,
        grid=(b, s // tm),
        in_specs=[tok, pl.BlockSpec((1, d), const), _const_spec(w_in.shape),
                  pl.BlockSpec(conv_w.shape, const),
                  pl.BlockSpec((1, dff), const), _const_spec(w_o.shape)],
        out_specs=tok,
        out_shape=jax.ShapeDtypeStruct((b, s, d), F32),
        scratch_shapes=[pltpu.VMEM((tm + HALO, dff), F32)],
        compiler_params=pltpu.CompilerParams(
            dimension_semantics=("arbitrary", "arbitrary"), vmem_limit_bytes=VMEM_LIMIT),
        name="conv_glu",
    )(h, g_ffn, w_in, conv_w, conv_b, w_o)


def _rearranged_w_in(w_in):
    sizes = (A_HEADS * HEAD_DIM, HEAD_DIM, HEAD_DIM, IDX_HEADS * HEAD_DIM, HEAD_DIM, IDX_HEADS,
             2 * B_HEADS * HEAD_DIM, 2 * B_HEADS * HEAD_DIM, B_HEADS * 2 * HEAD_DIM)
    offs = [0]
    for sz in sizes:
        offs.append(offs[-1] + sz)
    q_a, k_a, v_a, q_i, k_i, w_i, q_b, k_b, v_b = [w_in[:, offs[i]:offs[i + 1]] for i in range(9)]
    pad = jnp.zeros((w_in.shape[0], HEAD_DIM - IDX_HEADS), w_in.dtype)
    w_all = jnp.concatenate([q_a, q_i, k_a, k_a, k_i, k_i, v_a, w_i, pad, q_b, k_b], axis=1)
    assert w_all.shape[1] == _C_END
    w_vt = jnp.concatenate([v_b, v_a, jnp.zeros((w_in.shape[0], HEAD_DIM), w_in.dtype)], axis=1).T
    return w_all.astype(BF16), w_vt.astype(BF16)


def kernel(x, mem, positions, g_mix, w_in, g_qa, g_ka, g_qb, g_kb, lam_q1, lam_k1, lam_q2, lam_k2,
           w_out, g_xattn, g_mem, w_xq, w_xk, w_xv, w_xo, g_xq, g_xk, g_ffn, w_ffn_in, conv_w, conv_b,
           w_ffn_out):
    b, s, d = x.shape
    depth = g_mix.shape[0]
    topk = min(TOPK_MAX, s // 4)
    tm = min(512, s)
    tm_in = min(1024, s)

    inv_freq = 1.0 / (ROPE_THETA ** (jnp.arange(0, HEAD_DIM, 2, dtype=F32) / HEAD_DIM))
    pos = positions.reshape(b * s // tm_in, 4, tm_in // 4).transpose(0, 2, 1).reshape(b * s // 4, 4)
    ang = (pos.astype(F32)[:, :, None] * inv_freq).reshape(b * s // 4, LANES)
    cos_d, sin_d = jnp.cos(ang), jnp.sin(ang)
    sign = jnp.tile(jnp.repeat(jnp.array([-1.0, 1.0], F32), HEAD_DIM // 2), 512 // HEAD_DIM)
    blk = jnp.arange(MXU_DIM) // HEAD_DIM
    bd = jnp.where(blk[:, None] == blk[None, :], 1.0 / HEAD_DIM, 0.0).astype(BF16)

    h = x
    for l in range(depth):
        lambda_init = 0.8 - 0.6 * math.exp(-0.3 * l)
        gains = jnp.stack([jnp.tile(g, 512 // HEAD_DIM) for g in (g_qa[l], g_ka[l], g_qb[l], g_kb[l])])
        gains = jnp.concatenate([gains, sign[None, :], jnp.ones((3, 512), F32)], axis=0)
        w_all, w_vt = _rearranged_w_in(w_in[l])
        qa, qi, ks, wi, qb, kb, vbt, vat = _inproj(
            h.reshape(b * s, d), g_mix[l][None, :], w_all, w_vt, cos_d, sin_d, bd, gains, tm_in, s)
        r3 = lambda t: t.reshape(b, s, t.shape[-1])
        key_bound = lambda g: jnp.full((1, LANES), HEAD_DIM ** 0.5, F32) * jnp.max(jnp.abs(g))
        out_a = _dsa(r3(qa), r3(qi), r3(wi), r3(ks), vat, key_bound(g_ka[l]), topk, n_bisect=16)
        out_b = _diff(qb, kb, vbt, key_bound(g_kb[l]), lam_q1[l][None, :], lam_k1[l][None, :],
                      lam_q2[l][None, :], lam_k2[l][None, :], lambda_init)
        kmem, vmem = _memkv(mem, g_mem[l][None, :], w_xk[l].astype(BF16), w_xv[l].astype(BF16),
                            g_xk[l][None, :])
        h = _xattn(h, out_a, out_b, w_out[l].astype(BF16), g_xattn[l][None, :], w_xq[l].astype(BF16),
                   g_xq[l][None, :], kmem, vmem, w_xo[l].astype(BF16), min(1024, s))
        dff = w_ffn_out.shape[1]
        cw = jnp.concatenate([conv_w[l], jnp.zeros((8 - CONV_W, dff), F32)], axis=0)
        h = _ffn(h, g_ffn[l][None, :], w_ffn_in[l].astype(BF16), cw, conv_b[l][None, :],
                 w_ffn_out[l].astype(BF16), tm)
    return h
```

```python
import functools
import math

import jax
import jax.numpy as jnp
from jax import lax
from jax.experimental import pallas as pl
from jax.experimental.pallas import tpu as pltpu

F32 = jnp.float32
BF16 = jnp.bfloat16

EPS = 1e-6
ROPE_THETA = 10000.0
HEAD_DIM = 64
A_HEADS = 8
IDX_HEADS = 4
TOPK_MAX = 256
B_HEADS = 4
X_HEADS = 4
CONV_W = 3
LANES = 128
MXU_DIM = 256
DSA_QBLK = 128
DSA_PAIR = 2
DSA_STEP_PAIRS = 4
DIFF_QBLK = 256
DIFF_STEP_BLOCKS = 4
CAUSAL_STEP = MXU_DIM
VMEM_LIMIT = 56 * 1024 * 1024
LOG2E = 1.4426950408889634
SHIFT_MARGIN = 1.02
DENOM_FLOOR = 2.0 ** -40

_C_QA = 0
_C_QI = 512
_C_KS = 768
_C_QB = 1152
_C_KB = 1664
_C_END = 2176


def _dot(a, b):
    return jnp.dot(a, b, preferred_element_type=F32)


def _dot_nt(a, b):
    return lax.dot_general(a, b, (((1,), (1,)), ((), ())), preferred_element_type=F32)


def _rms_scale(x):
    return lax.rsqrt(jnp.mean(x * x, axis=-1, keepdims=True) + EPS)


def _const_spec(shape):
    zeros = (0,) * len(shape)
    return pl.BlockSpec(shape, lambda *_: zeros, pipeline_mode=pl.Buffered(1))


def _lane_tile(t, width):
    reps = width // t.shape[1]
    return t if reps == 1 else jnp.concatenate([t] * reps, axis=1)


def _col_reduce(x, reduce_fn):
    rows, cols = x.shape
    slab = 8 * max(1, 8 * LANES // cols)
    if rows % slab or rows == slab:
        return reduce_fn(x, axis=0, keepdims=True)
    part = reduce_fn(x.reshape(rows // slab, slab, cols), axis=0)
    return reduce_fn(part, axis=0, keepdims=True)


def _inproj_body(x_ref, gmix_ref, w_ref, wvt_ref, cos_ref, sin_ref, bd_ref, gains_ref,
                 qa_ref, qi_ref, ks_ref, wi_ref, qb_ref, kb_ref, vbt_ref, vat_ref):
    x = x_ref[...]
    hn = (x * _rms_scale(x) * gmix_ref[...]).astype(BF16)

    def spread(t):
        turned = [t] + [pltpu.roll(t, 32 * k, 1) for k in range(1, 4)]
        group = lax.broadcasted_iota(jnp.int32, t.shape, 1) >> 5
        parts = []
        for j in range(4):
            d = (group - j) & 3
            parts.append(jnp.where(d == 0, turned[0], jnp.where(d == 1, turned[1],
                                                               jnp.where(d == 2, turned[2], turned[3]))))
        return jnp.concatenate(parts, axis=0)

    cos = spread(cos_ref[...])
    sin = spread(sin_ref[...]) * gains_ref[4:5, 0:LANES]
    bd = bd_ref[...]

    def proj(c0, width):
        return _dot(hn, w_ref[:, c0:c0 + width])

    def group_rms_scale(p):
        sq = (p * p).astype(BF16)
        outs = []
        for j in range(p.shape[1] // MXU_DIM):
            outs.append(_dot(sq[:, MXU_DIM * j:MXU_DIM * (j + 1)], bd))
        ms = outs[0] if len(outs) == 1 else jnp.concatenate(outs, axis=1)
        return lax.rsqrt(ms + EPS)

    def rope(y):
        width = y.shape[1]
        lane = lax.broadcasted_iota(jnp.int32, y.shape, 1)
        first_half = (lane & (HEAD_DIM - 1)) < (HEAD_DIM // 2)
        swapped = jnp.where(first_half, pltpu.roll(y, width - HEAD_DIM // 2, 1),
                            pltpu.roll(y, HEAD_DIM // 2, 1))
        return y * _lane_tile(cos, width) + swapped * _lane_tile(sin, width)

    sm_scale = HEAD_DIM ** -0.5 * LOG2E

    p = proj(_C_QA, 512)
    qa_ref[...] = (rope(p * group_rms_scale(p) * gains_ref[0:1, :]) * sm_scale).astype(BF16)

    p = proj(_C_QI, 256)
    qi_ref[...] = rope(p).astype(BF16)

    p = proj(_C_KS, 384)
    p01 = p[:, 0:256]
    lane = lax.broadcasted_iota(jnp.int32, p01.shape, 1)
    y01 = jnp.where(lane < LANES, p01 * group_rms_scale(p01) * gains_ref[1:2, 0:256], p01)
    y01 = rope(y01)
    p2 = p[:, 256:384]
    lane = lax.broadcasted_iota(jnp.int32, p2.shape, 1)
    p2 = jnp.where(lane < HEAD_DIM, p2, p2 * (IDX_HEADS ** -0.5 * HEAD_DIM ** -0.5))
    ks_ref[:, 0:256] = y01.astype(BF16)
    ks_ref[:, 256:384] = p2.astype(BF16)
    wi_ref[...] = p2

    def store_heads(ref, y):
        for h in range(B_HEADS):
            ref[0, h] = y[:, LANES * h:LANES * (h + 1)].astype(BF16)

    p = proj(_C_QB, 512)
    store_heads(qb_ref, rope(p * group_rms_scale(p) * gains_ref[2:3, :]) * sm_scale)

    p = proj(_C_KB, 512)
    store_heads(kb_ref, rope(p * group_rms_scale(p) * gains_ref[3:4, :]))

    vt = _dot_nt(wvt_ref[...], hn)
    for h in range(B_HEADS):
        vbt_ref[0, h] = vt[LANES * h:LANES * (h + 1), :].astype(BF16)
    vat_ref[0] = vt[LANES * B_HEADS:LANES * (B_HEADS + 1), :].astype(BF16)


def _inproj(x2, gmix, w_all, w_vt, cos128, sin128, bd, gains, tm, seq):
    n, d = x2.shape
    row = lambda i: (i, 0)
    const = lambda i: (0, 0)
    outs = [(512, BF16), (256, BF16), (384, BF16), (LANES, F32)]
    tiles = seq // tm
    head_spec = pl.BlockSpec((1, B_HEADS, tm, LANES), lambda i: (i // tiles, 0, i % tiles, 0))
    head_shape = jax.ShapeDtypeStruct((n // seq, B_HEADS, seq, LANES), BF16)
    vbt_spec = pl.BlockSpec((1, B_HEADS, LANES, tm), lambda i: (i // tiles, 0, 0, i % tiles))
    vbt_shape = jax.ShapeDtypeStruct((n // seq, B_HEADS, LANES, seq), BF16)
    vat_spec = pl.BlockSpec((1, LANES, tm), lambda i: (i // tiles, 0, i % tiles))
    vat_shape = jax.ShapeDtypeStruct((n // seq, LANES, seq), BF16)
    return pl.pallas_call(
        _inproj_body,
        grid=(n // tm,),
        in_specs=[
            pl.BlockSpec((tm, d), row),
            pl.BlockSpec((1, d), const),
            pl.BlockSpec(w_all.shape, const),
            pl.BlockSpec(w_vt.shape, const),
            pl.BlockSpec((tm // 4, LANES), row),
            pl.BlockSpec((tm // 4, LANES), row),
            pl.BlockSpec(bd.shape, const),
            pl.BlockSpec(gains.shape, const),
        ],
        out_specs=([pl.BlockSpec((tm, w), row) for w, _ in outs]
                   + [head_spec, head_spec, vbt_spec, vat_spec]),
        out_shape=([jax.ShapeDtypeStruct((n, w), dt) for w, dt in outs]
                   + [head_shape, head_shape, vbt_shape, vat_shape]),
        compiler_params=pltpu.CompilerParams(
            dimension_semantics=("arbitrary",), vmem_limit_bytes=VMEM_LIMIT),
        name="inproj",
    )(x2, gmix, w_all, w_vt, cos128, sin128, bd, gains)


def _group_reduce(x, reduce_fn):
    g, rows, cols = x.shape
    slab = 8 * max(1, 8 * LANES // cols)
    if rows % slab or rows == slab:
        return reduce_fn(x, axis=1, keepdims=True)
    combine = {jnp.sum: jnp.add, jnp.max: jnp.maximum, jnp.min: jnp.minimum}[reduce_fn]
    part = x[:, 0:slab]
    for i in range(1, rows // slab):
        part = combine(part, x[:, i * slab:(i + 1) * slab])
    return reduce_fn(part, axis=1, keepdims=True)


def _dsa_queries(qa_ref, rows):
    qa = qa_ref[0, rows, :]
    lane = lax.broadcasted_iota(jnp.int32, (DSA_QBLK, LANES), 1)
    rows = []
    for h in range(A_HEADS):
        slab = qa[:, LANES * (h // 2):LANES * (h // 2 + 1)]
        keep = (lane >= HEAD_DIM) if h % 2 else (lane < HEAD_DIM)
        rows.append(jnp.where(keep, slab, jnp.zeros_like(slab)))
    return jnp.concatenate(rows, axis=0)


def _dsa_write(ot, l, o_ref, ot_ref, rows):
    tq = DSA_QBLK
    ot = ot * (1.0 / l)
    for h in range(A_HEADS):
        ot_ref[HEAD_DIM * h:HEAD_DIM * (h + 1), :] = ot[:, tq * h:tq * (h + 1)]
    o_ref[0, rows, :] = ot_ref[...].T.astype(BF16)


def _dsa_rows(block):
    if isinstance(block, int):
        return pl.ds(block * DSA_QBLK, DSA_QBLK)
    return pl.ds(pl.multiple_of(block * DSA_QBLK, DSA_QBLK), DSA_QBLK)


def _dsa_keys(klen, pair, sub, qa_ref, qi_ref, wi_ref, ks_ref, o_ref, vt_ref, sc_ref, bias_ref, ot_ref,
              kmax_ref, flag_ref, topk, n_bisect):
    tq = DSA_QBLK
    seq = ks_ref.shape[1]
    tail = klen - CAUSAL_STEP
    neg_inf = -jnp.inf

    lane_q = lax.broadcasted_iota(jnp.int32, (tq, LANES), 1)
    first_head = lane_q < HEAD_DIM

    def head_rows(slab, odd):
        return jnp.where(first_head != odd, slab, jnp.zeros_like(slab))

    shape_t = (DSA_PAIR, CAUSAL_STEP, tq)
    qpos_t = ((pair * DSA_PAIR + lax.broadcasted_iota(jnp.int32, shape_t, 0)) * tq
              + lax.broadcasted_iota(jnp.int32, shape_t, 2))
    causal_t = tail + lax.broadcasted_iota(jnp.int32, shape_t, 1) <= qpos_t

    def indexer(blk, carry):
        rows = _dsa_rows(sub * DSA_PAIR + blk)
        qi = qi_ref[0, rows, :]
        qi_stack = jnp.concatenate(
            [head_rows(qi[:, LANES * (h // 2):LANES * (h // 2 + 1)], bool(h % 2)) for h in range(IDX_HEADS)],
            axis=0)
        lg = _dot_nt(ks_ref[0, 0:klen, 128:256], qi_stack)
        w_t = wi_ref[0, rows, :].T
        sc = None
        for h in range(IDX_HEADS):
            term = jnp.maximum(lg[:, tq * h:tq * (h + 1)], 0.0) * w_t[HEAD_DIM + h:HEAD_DIM + h + 1, :]
            sc = term if sc is None else sc + term
        sc_ref[blk, 0:klen, :] = sc
        return carry

    for blk in range(DSA_PAIR):
        indexer(blk, 0)
    sc_ref[:, tail:klen, :] = jnp.where(causal_t, sc_ref[:, tail:klen, :], neg_inf)

    kf = float(topk)
    search = qpos_t[:, 0:1, :] >= topk

    slab = 64

    def fold(per_slab, combine, reduce_fn, start=0, stop=klen):
        acc = per_slab(sc_ref[:, start:start + slab, :], start)
        for r in range(start + slab, stop, slab):
            acc = combine(acc, per_slab(sc_ref[:, r:r + slab, :], r))
        return reduce_fn(acc, axis=1, keepdims=True)

    def count(pred):
        return fold(lambda s, _: jnp.where(pred(s), 1.0, 0.0), jnp.add, jnp.sum)

    hi0 = fold(lambda s, _: s, jnp.maximum, jnp.max)
    lo0 = fold(lambda s, r: jnp.where(causal_t[:, r - tail:r - tail + slab], s, jnp.inf),
               jnp.minimum, jnp.min, start=tail)
    if tail:
        lo0 = jnp.minimum(lo0, fold(lambda s, _: s, jnp.minimum, jnp.min, stop=tail))

    def bisect(_, carry):
        lo, hi = carry
        mid = 0.5 * (lo + hi)
        ge = count(lambda s: s >= mid) >= kf
        return jnp.where(ge, mid, lo), jnp.where(ge, hi, mid)

    lo, _ = lax.fori_loop(0, n_bisect, bisect, (lo0, hi0))

    def too_low(n_gt):
        return jnp.max(jnp.where(jnp.logical_and(search, n_gt >= kf), 1.0, 0.0))

    def climb(carry):
        thr, n_gt, _ = carry
        nxt = fold(lambda s, _: jnp.where(s > thr, s, jnp.inf), jnp.minimum, jnp.min)
        thr = jnp.where(jnp.logical_and(search, n_gt >= kf), nxt, thr)
        n_gt = count(lambda s: s > thr)
        return thr, n_gt, too_low(n_gt)

    thr0 = fold(lambda s, _: jnp.where(s >= lo, s, jnp.inf), jnp.minimum, jnp.min)
    n_gt0 = count(lambda s: s > thr0)
    thr, n_gt, _ = lax.while_loop(lambda c: c[2] > 0.0, climb, (thr0, n_gt0, too_low(n_gt0)))

    need = kf - n_gt
    open_row = jnp.where(search, neg_inf, 0.0)
    n_blk = klen // MXU_DIM
    tie_cat = jnp.concatenate(
        [jnp.where(sc_ref[g, MXU_DIM * j:MXU_DIM * (j + 1), :] == thr[g], 1.0, 0.0).astype(BF16)
         for g in range(DSA_PAIR) for j in range(n_blk)], axis=1)
    tri = jnp.where(lax.broadcasted_iota(jnp.int32, (MXU_DIM, MXU_DIM), 0)
                    >= lax.broadcasted_iota(jnp.int32, (MXU_DIM, MXU_DIM), 1), 1.0, 0.0).astype(BF16)
    prefix = _dot(tri, tie_cat)
    for g in range(DSA_PAIR):
        before = jnp.zeros((1, tq), F32)
        for j in range(n_blk):
            rows = slice(MXU_DIM * j, MXU_DIM * (j + 1))
            col = (g * n_blk + j) * tq
            rank = prefix[:, col:col + tq] + before
            before = before + prefix[MXU_DIM - 1:MXU_DIM, col:col + tq]
            sj = sc_ref[g, rows, :]
            admitted = jnp.where(sj == thr[g], jnp.where(rank <= need[g], 0.0, neg_inf), neg_inf)
            bias = jnp.maximum(jnp.where(sj > thr[g], 0.0, admitted), open_row[g])
            if j == n_blk - 1:
                bias = jnp.where(causal_t[g], bias, neg_inf)
            bias_ref[g, rows, :] = bias

    def attend(blk, underflow):
        rows = _dsa_rows(sub * DSA_PAIR + blk)
        q_all = _dsa_queries(qa_ref, rows)
        qsq = q_all.astype(F32)
        qn2 = _dot_nt(jnp.ones((8, LANES), BF16), (qsq * qsq).astype(BF16))[0:1, :]
        shift = jnp.sqrt(qn2) * (_lane_tile(kmax_ref[...], A_HEADS * tq) * SHIFT_MARGIN)
        kl = klen - tq * (DSA_PAIR - 1 - blk)
        bias = bias_ref[blk, 0:kl, :]
        st = _dot_nt(ks_ref[0, 0:kl, 0:128], q_all) + jnp.concatenate([bias] * A_HEADS, axis=1) - shift
        e = jnp.exp2(st)
        l = _col_reduce(e, jnp.sum)
        _dsa_write(_dot(vt_ref[0, 0:HEAD_DIM, 0:kl], e.astype(BF16)), l, o_ref, ot_ref, rows)
        return jnp.maximum(underflow, jnp.where(jnp.min(l) >= DENOM_FLOOR, 0, 1))

    underflow = 0
    for blk in range(DSA_PAIR):
        underflow = attend(blk, underflow)

    @pl.when(underflow != 0)
    def _():
        if klen < seq:
            bias_ref[:, klen:seq, :] = jnp.full((DSA_PAIR, seq - klen, tq), neg_inf, F32)
        flag_ref[sub] = 1


def _dsa_body(qa_ref, qi_ref, wi_ref, ks_ref, vt_ref, kmax_ref, o_ref, sc_ref, bias_ref, ot_ref,
              flag_ref, *, topk, n_bisect):
    step = pl.program_id(1)
    seq = ks_ref.shape[1]
    step_pairs = flag_ref.shape[0]
    for sub in range(step_pairs):
        flag_ref[sub] = 0

    for j in range(seq // (CAUSAL_STEP * step_pairs)):
        @pl.when(step == j)
        def _(j=j):
            for sub in range(step_pairs):
                c = step_pairs * j + sub
                _dsa_keys(CAUSAL_STEP * (c + 1), c, sub, qa_ref, qi_ref, wi_ref, ks_ref, o_ref, vt_ref,
                          sc_ref, bias_ref.at[sub], ot_ref, kmax_ref, flag_ref, topk, n_bisect)

    for sub in range(step_pairs):
        @pl.when(flag_ref[sub] != 0)
        def _(sub=sub):
            def redo(blk, carry):
                rows = _dsa_rows(sub * DSA_PAIR + blk)
                bias = bias_ref[sub, blk]
                st = (_dot_nt(ks_ref[0, :, 0:128], _dsa_queries(qa_ref, rows))
                      + jnp.concatenate([bias] * A_HEADS, axis=1))
                e = jnp.exp2(st - _col_reduce(st, jnp.max))
                _dsa_write(_dot(vt_ref[0, 0:HEAD_DIM, :], e.astype(BF16)), _col_reduce(e, jnp.sum), o_ref,
                           ot_ref, rows)
                return carry

            lax.fori_loop(0, DSA_PAIR, redo, 0)


def _dsa(qa, qi, wi, ks, vat, kmax, topk, n_bisect):
    b, s, _ = qa.shape
    tq = DSA_QBLK
    assert DSA_PAIR * tq == CAUSAL_STEP
    step_pairs = min(DSA_STEP_PAIRS, s // CAUSAL_STEP)
    step = step_pairs * CAUSAL_STEP
    assert s % step == 0
    blk = lambda bi, qi_: (bi, qi_, 0)
    return pl.pallas_call(
        functools.partial(_dsa_body, topk=topk, n_bisect=n_bisect),
        grid=(b, s // step),
        in_specs=[
            pl.BlockSpec((1, step, 512), blk),
            pl.BlockSpec((1, step, 256), blk),
            pl.BlockSpec((1, step, LANES), blk),
            pl.BlockSpec((1, s, 384), lambda bi, qi_: (bi, 0, 0)),
            pl.BlockSpec((1, LANES, s), lambda bi, qi_: (bi, 0, 0)),
            pl.BlockSpec((1, LANES), lambda bi, qi_: (0, 0)),
        ],
        out_specs=pl.BlockSpec((1, step, 512), blk),
        out_shape=jax.ShapeDtypeStruct((b, s, 512), BF16),
        scratch_shapes=[
            pltpu.VMEM((DSA_PAIR, s, tq), F32),
            pltpu.VMEM((step_pairs, DSA_PAIR, s, tq), F32),
            pltpu.VMEM((A_HEADS * HEAD_DIM, tq), F32),
            pltpu.SMEM((step_pairs,), jnp.int32),
        ],
        compiler_params=pltpu.CompilerParams(
            dimension_semantics=("arbitrary", "arbitrary"), vmem_limit_bytes=VMEM_LIMIT),
        name="dsa_attention",
    )(qa, qi, wi, ks, vat, kmax)


def _diff_body(q_ref, k_ref, vt_ref, kmax_ref, lq1_ref, lk1_ref, lq2_ref, lk2_ref, o_ref,
               flag_ref, *, lambda_init):
    step = pl.program_id(1)
    seq = k_ref.shape[2]
    n_sub = q_ref.shape[2] // DIFF_QBLK
    tq = q_ref.shape[2] // n_sub

    lam =(jnp.exp(jnp.sum(lq1_ref[...] * lk1_ref[...], axis=1, keepdims=True))
           - jnp.exp(jnp.sum(lq2_ref[...] * lk2_ref[...], axis=1, keepdims=True)) + lambda_init)
    lane = lax.broadcasted_iota(jnp.int32, (tq, LANES), 1)
    diag = (lax.broadcasted_iota(jnp.int32, (tq, 2 * tq), 0)
            <= (lax.broadcasted_iota(jnp.int32, (tq, 2 * tq), 1) & (tq - 1)))

    def queries(h, rows):
        q = q_ref[0, h, rows, :]
        zero = jnp.zeros_like(q)
        return jnp.concatenate([jnp.where(lane < HEAD_DIM, q, zero), jnp.where(lane >= HEAD_DIM, q, zero)],
                               axis=0)

    def write(h, rows, acc, l):
        acc = acc * (1.0 / l)
        ot = acc[:, 0:tq] - lam * acc[:, tq:2 * tq]
        ot = ot * lax.rsqrt(jnp.mean(ot * ot, axis=0, keepdims=True) + EPS) * (1.0 - lambda_init)
        o_ref[0, h, rows, :] = ot.T.astype(BF16)

    def attend(klen, h, rows, underflow):
        tail = klen - tq
        q2 = queries(h, rows)
        qsq = q2.astype(F32)
        qn2 = _dot_nt(jnp.ones((8, LANES), BF16), (qsq * qsq).astype(BF16))[0:1, :]
        shift = jnp.sqrt(qn2) * (_lane_tile(kmax_ref[...], 2 * tq) * SHIFT_MARGIN)
        st = _dot_nt(k_ref[0, h, 0:klen, :], q2) - shift
        e = jnp.exp2(jnp.where(diag, st[tail:klen], -jnp.inf))
        l = _col_reduce(e, jnp.sum)
        acc = _dot(vt_ref[0, h, :, tail:klen], e.astype(BF16))
        if tail:
            e = jnp.exp2(st[0:tail])
            l = l + _col_reduce(e, jnp.sum)
            acc = acc + _dot(vt_ref[0, h, :, 0:tail], e.astype(BF16))
        write(h, rows, acc, l)
        return jnp.maximum(underflow, jnp.where(jnp.min(l) >= DENOM_FLOOR, 0, 1))

    flag_ref[0] = 0
    for j in range(seq // (n_sub * tq)):
        @pl.when(step == j)
        def _(j=j):
            underflow = 0
            for i in range(n_sub):
                for h in range(B_HEADS):
                    underflow = attend(tq * (n_sub * j + i + 1), h, slice(tq * i, tq * (i + 1)), underflow)
            flag_ref[0] = underflow

    @pl.when(flag_ref[0] != 0)
    def _():
        kpos = lax.broadcasted_iota(jnp.int32, (seq, 2 * tq), 0)
        lane_q = lax.broadcasted_iota(jnp.int32, (seq, 2 * tq), 1) & (tq - 1)
        for i in range(n_sub):
            rows = slice(tq * i, tq * (i + 1))
            qpos = (step * n_sub + i) * tq + lane_q

            def redo(h, carry, rows=rows, qpos=qpos):
                st = jnp.where(kpos <= qpos, _dot_nt(k_ref[0, h], queries(h, rows)), -jnp.inf)
                e = jnp.exp2(st - _col_reduce(st, jnp.max))
                write(h, rows, _dot(vt_ref[0, h], e.astype(BF16)), _col_reduce(e, jnp.sum))
                return carry

            lax.fori_loop(0, B_HEADS, redo, 0)


def _diff(qb, kb, vbt, kmax, lq1, lk1, lq2, lk2, lambda_init):
    b, _, s, _ = qb.shape
    tq = min(DIFF_STEP_BLOCKS, s // DIFF_QBLK) * DIFF_QBLK
    assert s % tq == 0
    vec = pl.BlockSpec((1, HEAD_DIM), lambda bi, qi_: (0, 0))
    keys = pl.BlockSpec((1, B_HEADS, s, LANES), lambda bi, qi_: (bi, 0, 0, 0))
    vals = pl.BlockSpec((1, B_HEADS, LANES, s), lambda bi, qi_: (bi, 0, 0, 0))
    blk = pl.BlockSpec((1, B_HEADS, tq, LANES), lambda bi, qi_: (bi, 0, qi_, 0))
    return pl.pallas_call(
        functools.partial(_diff_body, lambda_init=lambda_init),
        grid=(b, s // tq),
        in_specs=[blk, keys, vals, pl.BlockSpec((1, LANES), lambda bi, qi_: (0, 0)), vec, vec, vec, vec],
        out_specs=blk,
        out_shape=jax.ShapeDtypeStruct((b, B_HEADS, s, LANES), BF16),
        scratch_shapes=[pltpu.SMEM((1,), jnp.int32)],
        compiler_params=pltpu.CompilerParams(
            dimension_semantics=("arbitrary", "arbitrary"), vmem_limit_bytes=VMEM_LIMIT),
        name="diff_attention",
    )(qb, kb, vbt, kmax, lq1, lk1, lq2, lk2)


def _memkv_body(mem_ref, g_ref, wk_ref, wv_ref, gk_ref, k_ref, v_ref):
    mem = mem_ref[0]
    memn = (mem * _rms_scale(mem) * g_ref[...]).astype(BF16)
    k = _dot(memn, wk_ref[...])
    hd = gk_ref.shape[1]
    for h in range(k.shape[1] // hd):
        kh = k[:, hd * h:hd * (h + 1)]
        k_ref[0, :, hd * h:hd * (h + 1)] = (kh * _rms_scale(kh) * gk_ref[...]).astype(BF16)
    v_ref[0] = _dot(memn, wv_ref[...]).astype(BF16)


def _memkv(mem, g_mem, w_xk, w_xv, g_xk):
    b, m, d = mem.shape
    const = lambda bi: (0, 0)
    blk = pl.BlockSpec((1, m, d), lambda bi: (bi, 0, 0))
    return pl.pallas_call(
        _memkv_body,
        grid=(b,),
        in_specs=[blk, pl.BlockSpec((1, d), const), pl.BlockSpec((d, d), const),
                  pl.BlockSpec((d, d), const), pl.BlockSpec(g_xk.shape, const)],
        out_specs=[blk, blk],
        out_shape=[jax.ShapeDtypeStruct((b, m, d), BF16)] * 2,
        compiler_params=pltpu.CompilerParams(
            dimension_semantics=("arbitrary",), vmem_limit_bytes=VMEM_LIMIT),
        name="mem_kv",
    )(mem, g_mem, w_xk, w_xv, g_xk)


def _xattn_body(x_ref, oa_ref, ob_ref, wo_ref, g_ref, wq_ref, gq_ref, k_ref, v_ref, wxo_ref, h_ref):
    half = oa_ref.shape[2]
    ob = jnp.concatenate([ob_ref[0, h] for h in range(B_HEADS)], axis=1)
    h1 = x_ref[0] + _dot(oa_ref[0], wo_ref[0:half, :]) + _dot(ob, wo_ref[half:2 * half, :])
    hn = (h1 * _rms_scale(h1) * g_ref[...]).astype(BF16)
    q = _dot(hn, wq_ref[...])
    hd = gq_ref.shape[1]
    outs = []
    for h in range(q.shape[1] // hd):
        sl = slice(hd * h, hd * (h + 1))
        qh = q[:, sl]
        qh = (qh * _rms_scale(qh) * gq_ref[...] * (hd ** -0.5)).astype(BF16)
        s = _dot_nt(qh, k_ref[0, :, sl])
        e = jnp.exp(s - jnp.max(s, axis=-1, keepdims=True))
        p = (e * (1.0 / jnp.sum(e, axis=-1, keepdims=True))).astype(BF16)
        outs.append(_dot(p, v_ref[0, :, sl]).astype(BF16))
    o = jnp.concatenate(outs, axis=1)
    h_ref[0] = h1 + _dot(o, wxo_ref[...])


def _xattn(x, oa, ob, w_out, g_x, w_xq, g_xq, kmem, vmem, w_xo, tm):
    b, s, d = x.shape
    m = kmem.shape[1]
    const = lambda bi, ti: (0, 0)
    tok = lambda w: pl.BlockSpec((1, tm, w), lambda bi, ti: (bi, ti, 0))
    memblk = pl.BlockSpec((1, m, d), lambda bi, ti: (bi, 0, 0))
    return pl.pallas_call(
        _xattn_body,
        grid=(b, s // tm),
        in_specs=[tok(d), tok(oa.shape[2]),
                  pl.BlockSpec((1, B_HEADS, tm, LANES), lambda bi, ti: (bi, 0, ti, 0)),
                  pl.BlockSpec(w_out.shape, const), pl.BlockSpec((1, d), const),
                  pl.BlockSpec(w_xq.shape, const), pl.BlockSpec(g_xq.shape, const),
                  memblk, memblk, pl.BlockSpec(w_xo.shape, const)],
        out_specs=tok(d),
        out_shape=jax.ShapeDtypeStruct((b, s, d), F32),
        compiler_params=pltpu.CompilerParams(
            dimension_semantics=("arbitrary", "arbitrary"), vmem_limit_bytes=VMEM_LIMIT),
        name="outproj_xattn",
    )(x, oa, ob, w_out, g_x, w_xq, g_xq, kmem, vmem, w_xo)


HALO = 8


def _ffn_body(h_ref, g_ref, win_ref, cw_ref, cb_ref, wo_ref, o_ref, a_ref):
    tm = h_ref.shape[1]
    dff = wo_ref.shape[0]

    @pl.when(pl.program_id(1) == 0)
    def _():
        a_ref[0:HALO, :] = jnp.zeros((HALO, a_ref.shape[1]), F32)

    h = h_ref[0]
    hn = (h * _rms_scale(h) * g_ref[...]).astype(BF16)
    a_ref[HALO:HALO + tm, :] = _dot(hn, win_ref[:, 0:dff])
    gate = _dot(hn, win_ref[:, dff:2 * dff])
    conv = cb_ref[...]
    for j in range(CONV_W):
        off = HALO - (CONV_W - 1) + j
        conv = conv + a_ref[off:off + tm, :] * cw_ref[j:j + 1, :]
    a_ref[0:HALO, :] = a_ref[tm:tm + HALO, :]
    u = (jax.nn.gelu(conv) * gate).astype(BF16)
    o_ref[0] = h + _dot(u, wo_ref[...])


def _ffn(h, g_ffn, w_in, conv_w, conv_b, w_o, tm):
    b, s, d = h.shape
    dff = w_o.shape[0]
    const = lambda bi, ti: (0, 0)
    tok = pl.BlockSpec((1, tm, d), lambda bi, ti: (bi, ti, 0))
    return pl.pallas_call(
        _ffn_body,
        grid=(b, s // tm),
        in_specs=[tok, pl.BlockSpec((1, d), const), _const_spec(w_in.shape),
                  pl.BlockSpec(conv_w.shape, const),
                  pl.BlockSpec((1, dff), const), _const_spec(w_o.shape)],
        out_specs=tok,
        out_shape=jax.ShapeDtypeStruct((b, s, d), F32),
        scratch_shapes=[pltpu.VMEM((tm + HALO, dff), F32)],
        compiler_params=pltpu.CompilerParams(
            dimension_semantics=("arbitrary", "arbitrary"), vmem_limit_bytes=VMEM_LIMIT),
        name="conv_glu",
    )(h, g_ffn, w_in, conv_w, conv_b, w_o)


def _rearranged_w_in(w_in):
    sizes = (A_HEADS * HEAD_DIM, HEAD_DIM, HEAD_DIM, IDX_HEADS * HEAD_DIM, HEAD_DIM, IDX_HEADS,
             2 * B_HEADS * HEAD_DIM, 2 * B_HEADS * HEAD_DIM, B_HEADS * 2 * HEAD_DIM)
    offs = [0]
    for sz in sizes:
        offs.append(offs[-1] + sz)
    q_a, k_a, v_a, q_i, k_i, w_i, q_b, k_b, v_b = [w_in[:, offs[i]:offs[i + 1]] for i in range(9)]
    pad = jnp.zeros((w_in.shape[0], HEAD_DIM - IDX_HEADS), w_in.dtype)
    w_all = jnp.concatenate([q_a, q_i, k_a, k_a, k_i, k_i, v_a, w_i, pad, q_b, k_b], axis=1)
    assert w_all.shape[1] == _C_END
    w_vt = jnp.concatenate([v_b, v_a, jnp.zeros((w_in.shape[0], HEAD_DIM), w_in.dtype)], axis=1).T
    return w_all.astype(BF16), w_vt.astype(BF16)


def kernel(x, mem, positions, g_mix, w_in, g_qa, g_ka, g_qb, g_kb, lam_q1, lam_k1, lam_q2, lam_k2,
           w_out, g_xattn, g_mem, w_xq, w_xk, w_xv, w_xo, g_xq, g_xk, g_ffn, w_ffn_in, conv_w, conv_b,
           w_ffn_out):
    b, s, d = x.shape
    depth = g_mix.shape[0]
    topk = min(TOPK_MAX, s // 4)
    tm = min(512, s)
    tm_in = min(1024, s)

    inv_freq = 1.0 / (ROPE_THETA ** (jnp.arange(0, HEAD_DIM, 2, dtype=F32) / HEAD_DIM))
    pos = positions.reshape(b * s // tm_in, 4, tm_in // 4).transpose(0, 2, 1).reshape(b * s // 4, 4)
    ang = (pos.astype(F32)[:, :, None] * inv_freq).reshape(b * s // 4, LANES)
    cos_d, sin_d = jnp.cos(ang), jnp.sin(ang)
    sign = jnp.tile(jnp.repeat(jnp.array([-1.0, 1.0], F32), HEAD_DIM // 2), 512 // HEAD_DIM)
    blk = jnp.arange(MXU_DIM) // HEAD_DIM
    bd = jnp.where(blk[:, None] == blk[None, :], 1.0 / HEAD_DIM, 0.0).astype(BF16)

    h = x
    for l in range(depth):
        lambda_init = 0.8 - 0.6 * math.exp(-0.3 * l)
        gains = jnp.stack([jnp.tile(g, 512 // HEAD_DIM) for g in (g_qa[l], g_ka[l], g_qb[l], g_kb[l])])
        gains = jnp.concatenate([gains, sign[None, :], jnp.ones((3, 512), F32)], axis=0)
        w_all, w_vt = _rearranged_w_in(w_in[l])
        qa, qi, ks, wi, qb, kb, vbt, vat = _inproj(
            h.reshape(b * s, d), g_mix[l][None, :], w_all, w_vt, cos_d, sin_d, bd, gains, tm_in, s)
        r3 = lambda t: t.reshape(b, s, t.shape[-1])
        key_bound = lambda g: jnp.full((1, LANES), HEAD_DIM ** 0.5, F32) * jnp.max(jnp.abs(g))
        out_a = _dsa(r3(qa), r3(qi), r3(wi), r3(ks), vat, key_bound(g_ka[l]), topk, n_bisect=16)
        out_b = _diff(qb, kb, vbt, key_bound(g_kb[l]), lam_q1[l][None, :], lam_k1[l][None, :],
                      lam_q2[l][None, :], lam_k2[l][None, :], lambda_init)
        kmem, vmem = _memkv(mem, g_mem[l][None, :], w_xk[l].astype(BF16), w_xv[l].astype(BF16),
                            g_xk[l][None, :])
        h = _xattn(h, out_a, out_b, w_out[l].astype(BF16), g_xattn[l][None, :], w_xq[l].astype(BF16),
                   g_xq[l][None, :], kmem, vmem, w_xo[l].astype(BF16), min(1024, s))
        dff = w_ffn_out.shape[1]
        cw = jnp.concatenate([conv_w[l], jnp.zeros((8 - CONV_W, dff), F32)], axis=0)
        h = _ffn(h, g_ffn[l][None, :], w_ffn_in[l].astype(BF16), cw, conv_b[l][None, :],
                 w_ffn_out[l].astype(BF16), tm)
    return h
```

```python
import functools
import math

import jax
import jax.numpy as jnp
from jax import lax
from jax.experimental import pallas as pl
from jax.experimental.pallas import tpu as pltpu

F32 = jnp.float32
BF16 = jnp.bfloat16

EPS = 1e-6
ROPE_THETA = 10000.0
HEAD_DIM = 64
A_HEADS = 8
IDX_HEADS = 4
TOPK_MAX = 256
B_HEADS = 4
X_HEADS = 4
CONV_W = 3
LANES = 128
MXU_DIM = 256
DSA_QBLK = 128
DSA_PAIR = 2
DSA_STEP_PAIRS = 2
DIFF_QBLK = 256
DIFF_STEP_BLOCKS = 4
CAUSAL_STEP = MXU_DIM
VMEM_LIMIT = 56 * 1024 * 1024
LOG2E = 1.4426950408889634
SHIFT_MARGIN = 1.02
DENOM_FLOOR = 2.0 ** -40

_C_QA = 0
_C_QI = 512
_C_KS = 768
_C_QB = 1152
_C_KB = 1664
_C_END = 2176


def _dot(a, b):
    return jnp.dot(a, b, preferred_element_type=F32)


def _dot_nt(a, b):
    return lax.dot_general(a, b, (((1,), (1,)), ((), ())), preferred_element_type=F32)


def _rms_scale(x):
    return lax.rsqrt(jnp.mean(x * x, axis=-1, keepdims=True) + EPS)


def _const_spec(shape):
    zeros = (0,) * len(shape)
    return pl.BlockSpec(shape, lambda *_: zeros, pipeline_mode=pl.Buffered(1))


def _lane_tile(t, width):
    reps = width // t.shape[1]
    return t if reps == 1 else jnp.concatenate([t] * reps, axis=1)


def _col_reduce(x, reduce_fn):
    rows, cols = x.shape
    slab = 8 * max(1, 8 * LANES // cols)
    if rows % slab or rows == slab:
        return reduce_fn(x, axis=0, keepdims=True)
    part = reduce_fn(x.reshape(rows // slab, slab, cols), axis=0)
    return reduce_fn(part, axis=0, keepdims=True)


def _inproj_body(x_ref, gmix_ref, w_ref, wvt_ref, cos_ref, sin_ref, bd_ref, gains_ref,
                 qa_ref, qi_ref, ks_ref, wi_ref, qb_ref, kb_ref, vbt_ref, vat_ref):
    x = x_ref[...]
    hn = (x * _rms_scale(x) * gmix_ref[...]).astype(BF16)

    def spread(t):
        turned = [t] + [pltpu.roll(t, 32 * k, 1) for k in range(1, 4)]
        group = lax.broadcasted_iota(jnp.int32, t.shape, 1) >> 5
        parts = []
        for j in range(4):
            d = (group - j) & 3
            parts.append(jnp.where(d == 0, turned[0], jnp.where(d == 1, turned[1],
                                                               jnp.where(d == 2, turned[2], turned[3]))))
        return jnp.concatenate(parts, axis=0)

    cos = spread(cos_ref[...])
    sin = spread(sin_ref[...]) * gains_ref[4:5, 0:LANES]
    bd = bd_ref[...]

    def proj(c0, width):
        return _dot(hn, w_ref[:, c0:c0 + width])

    def group_rms_scale(p):
        sq = (p * p).astype(BF16)
        outs = []
        for j in range(p.shape[1] // MXU_DIM):
            outs.append(_dot(sq[:, MXU_DIM * j:MXU_DIM * (j + 1)], bd))
        ms = outs[0] if len(outs) == 1 else jnp.concatenate(outs, axis=1)
        return lax.rsqrt(ms + EPS)

    def rope(y):
        width = y.shape[1]
        lane = lax.broadcasted_iota(jnp.int32, y.shape, 1)
        first_half = (lane & (HEAD_DIM - 1)) < (HEAD_DIM // 2)
        swapped = jnp.where(first_half, pltpu.roll(y, width - HEAD_DIM // 2, 1),
                            pltpu.roll(y, HEAD_DIM // 2, 1))
        return y * _lane_tile(cos, width) + swapped * _lane_tile(sin, width)

    sm_scale = HEAD_DIM ** -0.5 * LOG2E

    p = proj(_C_QA, 512)
    qa_ref[...] = (rope(p * group_rms_scale(p) * gains_ref[0:1, :]) * sm_scale).astype(BF16)

    p = proj(_C_QI, 256)
    qi_ref[...] = rope(p).astype(BF16)

    p = proj(_C_KS, 384)
    p01 = p[:, 0:256]
    lane = lax.broadcasted_iota(jnp.int32, p01.shape, 1)
    y01 = jnp.where(lane < LANES, p01 * group_rms_scale(p01) * gains_ref[1:2, 0:256], p01)
    y01 = rope(y01)
    p2 = p[:, 256:384]
    lane = lax.broadcasted_iota(jnp.int32, p2.shape, 1)
    p2 = jnp.where(lane < HEAD_DIM, p2, p2 * (IDX_HEADS ** -0.5 * HEAD_DIM ** -0.5))
    ks_ref[:, 0:256] = y01.astype(BF16)
    ks_ref[:, 256:384] = p2.astype(BF16)
    wi_ref[...] = p2

    def store_heads(ref, y):
        for h in range(B_HEADS):
            ref[0, h] = y[:, LANES * h:LANES * (h + 1)].astype(BF16)

    p = proj(_C_QB, 512)
    store_heads(qb_ref, rope(p * group_rms_scale(p) * gains_ref[2:3, :]) * sm_scale)

    p = proj(_C_KB, 512)
    store_heads(kb_ref, rope(p * group_rms_scale(p) * gains_ref[3:4, :]))

    vt = _dot_nt(wvt_ref[...], hn)
    for h in range(B_HEADS):
        vbt_ref[0, h] = vt[LANES * h:LANES * (h + 1), :].astype(BF16)
    vat_ref[0] = vt[LANES * B_HEADS:LANES * (B_HEADS + 1), :].astype(BF16)


def _inproj(x2, gmix, w_all, w_vt, cos128, sin128, bd, gains, tm, seq):
    n, d = x2.shape
    row = lambda i: (i, 0)
    const = lambda i: (0, 0)
    outs = [(512, BF16), (256, BF16), (384, BF16), (LANES, F32)]
    tiles = seq // tm
    head_spec = pl.BlockSpec((1, B_HEADS, tm, LANES), lambda i: (i // tiles, 0, i % tiles, 0))
    head_shape = jax.ShapeDtypeStruct((n // seq, B_HEADS, seq, LANES), BF16)
    vbt_spec = pl.BlockSpec((1, B_HEADS, LANES, tm), lambda i: (i // tiles, 0, 0, i % tiles))
    vbt_shape = jax.ShapeDtypeStruct((n // seq, B_HEADS, LANES, seq), BF16)
    vat_spec = pl.BlockSpec((1, LANES, tm), lambda i: (i // tiles, 0, i % tiles))
    vat_shape = jax.ShapeDtypeStruct((n // seq, LANES, seq), BF16)
    return pl.pallas_call(
        _inproj_body,
        grid=(n // tm,),
        in_specs=[
            pl.BlockSpec((tm, d), row),
            pl.BlockSpec((1, d), const),
            pl.BlockSpec(w_all.shape, const),
            pl.BlockSpec(w_vt.shape, const),
            pl.BlockSpec((tm // 4, LANES), row),
            pl.BlockSpec((tm // 4, LANES), row),
            pl.BlockSpec(bd.shape, const),
            pl.BlockSpec(gains.shape, const),
        ],
        out_specs=([pl.BlockSpec((tm, w), row) for w, _ in outs]
                   + [head_spec, head_spec, vbt_spec, vat_spec]),
        out_shape=([jax.ShapeDtypeStruct((n, w), dt) for w, dt in outs]
                   + [head_shape, head_shape, vbt_shape, vat_shape]),
        compiler_params=pltpu.CompilerParams(
            dimension_semantics=("arbitrary",), vmem_limit_bytes=VMEM_LIMIT),
        name="inproj",
    )(x2, gmix, w_all, w_vt, cos128, sin128, bd, gains)


def _group_reduce(x, reduce_fn):
    g, rows, cols = x.shape
    slab = 8 * max(1, 8 * LANES // cols)
    if rows % slab or rows == slab:
        return reduce_fn(x, axis=1, keepdims=True)
    combine = {jnp.sum: jnp.add, jnp.max: jnp.maximum, jnp.min: jnp.minimum}[reduce_fn]
    part = x[:, 0:slab]
    for i in range(1, rows // slab):
        part = combine(part, x[:, i * slab:(i + 1) * slab])
    return reduce_fn(part, axis=1, keepdims=True)


def _dsa_queries(qa_ref, rows):
    qa = qa_ref[0, rows, :]
    lane = lax.broadcasted_iota(jnp.int32, (DSA_QBLK, LANES), 1)
    rows = []
    for h in range(A_HEADS):
        slab = qa[:, LANES * (h // 2):LANES * (h // 2 + 1)]
        keep = (lane >= HEAD_DIM) if h % 2 else (lane < HEAD_DIM)
        rows.append(jnp.where(keep, slab, jnp.zeros_like(slab)))
    return jnp.concatenate(rows, axis=0)


def _dsa_write(ot, l, o_ref, ot_ref, rows):
    tq = DSA_QBLK
    ot = ot * (1.0 / l)
    for h in range(A_HEADS):
        ot_ref[HEAD_DIM * h:HEAD_DIM * (h + 1), :] = ot[:, tq * h:tq * (h + 1)]
    o_ref[0, rows, :] = ot_ref[...].T.astype(BF16)


def _dsa_rows(block):
    if isinstance(block, int):
        return pl.ds(block * DSA_QBLK, DSA_QBLK)
    return pl.ds(pl.multiple_of(block * DSA_QBLK, DSA_QBLK), DSA_QBLK)


def _dsa_keys(klen, pair, sub, qa_ref, qi_ref, wi_ref, ks_ref, o_ref, vt_ref, sc_ref, bias_ref, ot_ref,
              kmax_ref, flag_ref, topk, n_bisect):
    tq = DSA_QBLK
    seq = ks_ref.shape[1]
    tail = klen - CAUSAL_STEP
    neg_inf = -jnp.inf

    lane_q = lax.broadcasted_iota(jnp.int32, (tq, LANES), 1)
    first_head = lane_q < HEAD_DIM

    def head_rows(slab, odd):
        return jnp.where(first_head != odd, slab, jnp.zeros_like(slab))

    shape_t = (DSA_PAIR, CAUSAL_STEP, tq)
    qpos_t = ((pair * DSA_PAIR + lax.broadcasted_iota(jnp.int32, shape_t, 0)) * tq
              + lax.broadcasted_iota(jnp.int32, shape_t, 2))
    causal_t = tail + lax.broadcasted_iota(jnp.int32, shape_t, 1) <= qpos_t

    def indexer(blk, carry):
        rows = _dsa_rows(sub * DSA_PAIR + blk)
        qi = qi_ref[0, rows, :]
        qi_stack = jnp.concatenate(
            [head_rows(qi[:, LANES * (h // 2):LANES * (h // 2 + 1)], bool(h % 2)) for h in range(IDX_HEADS)],
            axis=0)
        lg = _dot_nt(ks_ref[0, 0:klen, 128:256], qi_stack)
        w_t = wi_ref[0, rows, :].T
        sc = None
        for h in range(IDX_HEADS):
            term = jnp.maximum(lg[:, tq * h:tq * (h + 1)], 0.0) * w_t[HEAD_DIM + h:HEAD_DIM + h + 1, :]
            sc = term if sc is None else sc + term
        sc_ref[blk, 0:klen, :] = sc
        return carry

    lax.fori_loop(0, DSA_PAIR, indexer, 0)
    sc_ref[:, tail:klen, :] = jnp.where(causal_t, sc_ref[:, tail:klen, :], neg_inf)

    kf = float(topk)
    search = qpos_t[:, 0:1, :] >= topk

    slab = 64

    def fold(per_slab, combine, reduce_fn, start=0, stop=klen):
        acc = per_slab(sc_ref[:, start:start + slab, :], start)
        for r in range(start + slab, stop, slab):
            acc = combine(acc, per_slab(sc_ref[:, r:r + slab, :], r))
        return reduce_fn(acc, axis=1, keepdims=True)

    def count(pred):
        return fold(lambda s, _: jnp.where(pred(s), 1.0, 0.0), jnp.add, jnp.sum)

    hi0 = fold(lambda s, _: s, jnp.maximum, jnp.max)
    lo0 = fold(lambda s, r: jnp.where(causal_t[:, r - tail:r - tail + slab], s, jnp.inf),
               jnp.minimum, jnp.min, start=tail)
    if tail:
        lo0 = jnp.minimum(lo0, fold(lambda s, _: s, jnp.minimum, jnp.min, stop=tail))

    def bisect(_, carry):
        lo, hi = carry
        mid = 0.5 * (lo + hi)
        ge = count(lambda s: s >= mid) >= kf
        return jnp.where(ge, mid, lo), jnp.where(ge, hi, mid)

    lo, _ = lax.fori_loop(0, n_bisect, bisect, (lo0, hi0))

    def too_low(n_gt):
        return jnp.max(jnp.where(jnp.logical_and(search, n_gt >= kf), 1.0, 0.0))

    def climb(carry):
        thr, n_gt, _ = carry
        nxt = fold(lambda s, _: jnp.where(s > thr, s, jnp.inf), jnp.minimum, jnp.min)
        thr = jnp.where(jnp.logical_and(search, n_gt >= kf), nxt, thr)
        n_gt = count(lambda s: s > thr)
        return thr, n_gt, too_low(n_gt)

    thr0 = fold(lambda s, _: jnp.where(s >= lo, s, jnp.inf), jnp.minimum, jnp.min)
    n_gt0 = count(lambda s: s > thr0)
    thr, n_gt, _ = lax.while_loop(lambda c: c[2] > 0.0, climb, (thr0, n_gt0, too_low(n_gt0)))

    need = kf - n_gt
    open_row = jnp.where(search, neg_inf, 0.0)
    n_blk = klen // MXU_DIM
    tie_cat = jnp.concatenate(
        [jnp.where(sc_ref[g, MXU_DIM * j:MXU_DIM * (j + 1), :] == thr[g], 1.0, 0.0).astype(BF16)
         for g in range(DSA_PAIR) for j in range(n_blk)], axis=1)
    tri = jnp.where(lax.broadcasted_iota(jnp.int32, (MXU_DIM, MXU_DIM), 0)
                    >= lax.broadcasted_iota(jnp.int32, (MXU_DIM, MXU_DIM), 1), 1.0, 0.0).astype(BF16)
    prefix = _dot(tri, tie_cat)
    for g in range(DSA_PAIR):
        before = jnp.zeros((1, tq), F32)
        for j in range(n_blk):
            rows = slice(MXU_DIM * j, MXU_DIM * (j + 1))
            col = (g * n_blk + j) * tq
            rank = prefix[:, col:col + tq] + before
            before = before + prefix[MXU_DIM - 1:MXU_DIM, col:col + tq]
            sj = sc_ref[g, rows, :]
            admitted = jnp.where(sj == thr[g], jnp.where(rank <= need[g], 0.0, neg_inf), neg_inf)
            bias = jnp.maximum(jnp.where(sj > thr[g], 0.0, admitted), open_row[g])
            if j == n_blk - 1:
                bias = jnp.where(causal_t[g], bias, neg_inf)
            bias_ref[g, rows, :] = bias

    def attend(blk, underflow):
        rows = _dsa_rows(sub * DSA_PAIR + blk)
        q_all = _dsa_queries(qa_ref, rows)
        qsq = q_all.astype(F32)
        qn2 = _dot_nt(jnp.ones((8, LANES), BF16), (qsq * qsq).astype(BF16))[0:1, :]
        shift = jnp.sqrt(qn2) * (_lane_tile(kmax_ref[...], A_HEADS * tq) * SHIFT_MARGIN)
        kl = klen - tq * (DSA_PAIR - 1 - blk)
        bias = bias_ref[blk, 0:kl, :]
        st = _dot_nt(ks_ref[0, 0:kl, 0:128], q_all) + jnp.concatenate([bias] * A_HEADS, axis=1) - shift
        e = jnp.exp2(st)
        l = _col_reduce(e, jnp.sum)
        _dsa_write(_dot(vt_ref[0, 0:HEAD_DIM, 0:kl], e.astype(BF16)), l, o_ref, ot_ref, rows)
        return jnp.maximum(underflow, jnp.where(jnp.min(l) >= DENOM_FLOOR, 0, 1))

    underflow = 0
    for blk in range(DSA_PAIR):
        underflow = attend(blk, underflow)

    @pl.when(underflow != 0)
    def _():
        if klen < seq:
            bias_ref[:, klen:seq, :] = jnp.full((DSA_PAIR, seq - klen, tq), neg_inf, F32)
        flag_ref[sub] = 1


def _dsa_body(qa_ref, qi_ref, wi_ref, ks_ref, vt_ref, kmax_ref, o_ref, sc_ref, bias_ref, ot_ref,
              flag_ref, *, topk, n_bisect):
    step = pl.program_id(1)
    seq = ks_ref.shape[1]
    step_pairs = flag_ref.shape[0]
    for sub in range(step_pairs):
        flag_ref[sub] = 0

    for j in range(seq // (CAUSAL_STEP * step_pairs)):
        @pl.when(step == j)
        def _(j=j):
            for sub in range(step_pairs):
                c = step_pairs * j + sub
                _dsa_keys(CAUSAL_STEP * (c + 1), c, sub, qa_ref, qi_ref, wi_ref, ks_ref, o_ref, vt_ref,
                          sc_ref, bias_ref.at[sub], ot_ref, kmax_ref, flag_ref, topk, n_bisect)

    for sub in range(step_pairs):
        @pl.when(flag_ref[sub] != 0)
        def _(sub=sub):
            def redo(blk, carry):
                rows = _dsa_rows(sub * DSA_PAIR + blk)
                bias = bias_ref[sub, blk]
                st = (_dot_nt(ks_ref[0, :, 0:128], _dsa_queries(qa_ref, rows))
                      + jnp.concatenate([bias] * A_HEADS, axis=1))
                e = jnp.exp2(st - _col_reduce(st, jnp.max))
                _dsa_write(_dot(vt_ref[0, 0:HEAD_DIM, :], e.astype(BF16)), _col_reduce(e, jnp.sum), o_ref,
                           ot_ref, rows)
                return carry

            lax.fori_loop(0, DSA_PAIR, redo, 0)


def _dsa(qa, qi, wi, ks, vat, kmax, topk, n_bisect):
    b, s, _ = qa.shape
    tq = DSA_QBLK
    assert DSA_PAIR * tq == CAUSAL_STEP
    step_pairs = min(DSA_STEP_PAIRS, s // CAUSAL_STEP)
    step = step_pairs * CAUSAL_STEP
    assert s % step == 0
    blk = lambda bi, qi_: (bi, qi_, 0)
    return pl.pallas_call(
        functools.partial(_dsa_body, topk=topk, n_bisect=n_bisect),
        grid=(b, s // step),
        in_specs=[
            pl.BlockSpec((1, step, 512), blk),
            pl.BlockSpec((1, step, 256), blk),
            pl.BlockSpec((1, step, LANES), blk),
            pl.BlockSpec((1, s, 384), lambda bi, qi_: (bi, 0, 0)),
            pl.BlockSpec((1, LANES, s), lambda bi, qi_: (bi, 0, 0)),
            pl.BlockSpec((1, LANES), lambda bi, qi_: (0, 0)),
        ],
        out_specs=pl.BlockSpec((1, step, 512), blk),
        out_shape=jax.ShapeDtypeStruct((b, s, 512), BF16),
        scratch_shapes=[
            pltpu.VMEM((DSA_PAIR, s, tq), F32),
            pltpu.VMEM((step_pairs, DSA_PAIR, s, tq), F32),
            pltpu.VMEM((A_HEADS * HEAD_DIM, tq), F32),
            pltpu.SMEM((step_pairs,), jnp.int32),
        ],
        compiler_params=pltpu.CompilerParams(
            dimension_semantics=("arbitrary", "arbitrary"), vmem_limit_bytes=VMEM_LIMIT),
        name="dsa_attention",
    )(qa, qi, wi, ks, vat, kmax)


def _diff_body(q_ref, k_ref, vt_ref, kmax_ref, lq1_ref, lk1_ref, lq2_ref, lk2_ref, o_ref,
               flag_ref, *, lambda_init):
    step = pl.program_id(1)
    seq = k_ref.shape[2]
    n_sub = q_ref.shape[2] // DIFF_QBLK
    tq = q_ref.shape[2] // n_sub

    lam =(jnp.exp(jnp.sum(lq1_ref[...] * lk1_ref[...], axis=1, keepdims=True))
           - jnp.exp(jnp.sum(lq2_ref[...] * lk2_ref[...], axis=1, keepdims=True)) + lambda_init)
    lane = lax.broadcasted_iota(jnp.int32, (tq, LANES), 1)
    diag = (lax.broadcasted_iota(jnp.int32, (tq, 2 * tq), 0)
            <= (lax.broadcasted_iota(jnp.int32, (tq, 2 * tq), 1) & (tq - 1)))

    def queries(h, rows):
        q = q_ref[0, h, rows, :]
        zero = jnp.zeros_like(q)
        return jnp.concatenate([jnp.where(lane < HEAD_DIM, q, zero), jnp.where(lane >= HEAD_DIM, q, zero)],
                               axis=0)

    def write(h, rows, acc, l):
        acc = acc * (1.0 / l)
        ot = acc[:, 0:tq] - lam * acc[:, tq:2 * tq]
        ot = ot * lax.rsqrt(jnp.mean(ot * ot, axis=0, keepdims=True) + EPS) * (1.0 - lambda_init)
        o_ref[0, h, rows, :] = ot.T.astype(BF16)

    def attend(klen, h, rows, underflow):
        tail = klen - tq
        q2 = queries(h, rows)
        qsq = q2.astype(F32)
        qn2 = _dot_nt(jnp.ones((8, LANES), BF16), (qsq * qsq).astype(BF16))[0:1, :]
        shift = jnp.sqrt(qn2) * (_lane_tile(kmax_ref[...], 2 * tq) * SHIFT_MARGIN)
        st = _dot_nt(k_ref[0, h, 0:klen, :], q2) - shift
        e = jnp.exp2(jnp.where(diag, st[tail:klen], -jnp.inf))
        l = _col_reduce(e, jnp.sum)
        acc = _dot(vt_ref[0, h, :, tail:klen], e.astype(BF16))
        if tail:
            e = jnp.exp2(st[0:tail])
            l = l + _col_reduce(e, jnp.sum)
            acc = acc + _dot(vt_ref[0, h, :, 0:tail], e.astype(BF16))
        write(h, rows, acc, l)
        return jnp.maximum(underflow, jnp.where(jnp.min(l) >= DENOM_FLOOR, 0, 1))

    flag_ref[0] = 0
    for j in range(seq // (n_sub * tq)):
        @pl.when(step == j)
        def _(j=j):
            underflow = 0
            for i in range(n_sub):
                for h in range(B_HEADS):
                    underflow = attend(tq * (n_sub * j + i + 1), h, slice(tq * i, tq * (i + 1)), underflow)
            flag_ref[0] = underflow

    @pl.when(flag_ref[0] != 0)
    def _():
        kpos = lax.broadcasted_iota(jnp.int32, (seq, 2 * tq), 0)
        lane_q = lax.broadcasted_iota(jnp.int32, (seq, 2 * tq), 1) & (tq - 1)
        for i in range(n_sub):
            rows = slice(tq * i, tq * (i + 1))
            qpos = (step * n_sub + i) * tq + lane_q

            def redo(h, carry, rows=rows, qpos=qpos):
                st = jnp.where(kpos <= qpos, _dot_nt(k_ref[0, h], queries(h, rows)), -jnp.inf)
                e = jnp.exp2(st - _col_reduce(st, jnp.max))
                write(h, rows, _dot(vt_ref[0, h], e.astype(BF16)), _col_reduce(e, jnp.sum))
                return carry

            lax.fori_loop(0, B_HEADS, redo, 0)


def _diff(qb, kb, vbt, kmax, lq1, lk1, lq2, lk2, lambda_init):
    b, _, s, _ = qb.shape
    tq = min(DIFF_STEP_BLOCKS, s // DIFF_QBLK) * DIFF_QBLK
    assert s % tq == 0
    vec = pl.BlockSpec((1, HEAD_DIM), lambda bi, qi_: (0, 0))
    keys = pl.BlockSpec((1, B_HEADS, s, LANES), lambda bi, qi_: (bi, 0, 0, 0))
    vals = pl.BlockSpec((1, B_HEADS, LANES, s), lambda bi, qi_: (bi, 0, 0, 0))
    blk = pl.BlockSpec((1, B_HEADS, tq, LANES), lambda bi, qi_: (bi, 0, qi_, 0))
    return pl.pallas_call(
        functools.partial(_diff_body, lambda_init=lambda_init),
        grid=(b, s // tq),
        in_specs=[blk, keys, vals, pl.BlockSpec((1, LANES), lambda bi, qi_: (0, 0)), vec, vec, vec, vec],
        out_specs=blk,
        out_shape=jax.ShapeDtypeStruct((b, B_HEADS, s, LANES), BF16),
        scratch_shapes=[pltpu.SMEM((1,), jnp.int32)],
        compiler_params=pltpu.CompilerParams(
            dimension_semantics=("arbitrary", "arbitrary"), vmem_limit_bytes=VMEM_LIMIT),
        name="diff_attention",
    )(qb, kb, vbt, kmax, lq1, lk1, lq2, lk2)


def _memkv_body(mem_ref, g_ref, wk_ref, wv_ref, gk_ref, k_ref, v_ref):
    mem = mem_ref[0]
    memn = (mem * _rms_scale(mem) * g_ref[...]).astype(BF16)
    k = _dot(memn, wk_ref[...])
    hd = gk_ref.shape[1]
    for h in range(k.shape[1] // hd):
        kh = k[:, hd * h:hd * (h + 1)]
        k_ref[0, :, hd * h:hd * (h + 1)] = (kh * _rms_scale(kh) * gk_ref[...]).astype(BF16)
    v_ref[0] = _dot(memn, wv_ref[...]).astype(BF16)


def _memkv(mem, g_mem, w_xk, w_xv, g_xk):
    b, m, d = mem.shape
    const = lambda bi: (0, 0)
    blk = pl.BlockSpec((1, m, d), lambda bi: (bi, 0, 0))
    return pl.pallas_call(
        _memkv_body,
        grid=(b,),
        in_specs=[blk, pl.BlockSpec((1, d), const), pl.BlockSpec((d, d), const),
                  pl.BlockSpec((d, d), const), pl.BlockSpec(g_xk.shape, const)],
        out_specs=[blk, blk],
        out_shape=[jax.ShapeDtypeStruct((b, m, d), BF16)] * 2,
        compiler_params=pltpu.CompilerParams(
            dimension_semantics=("arbitrary",), vmem_limit_bytes=VMEM_LIMIT),
        name="mem_kv",
    )(mem, g_mem, w_xk, w_xv, g_xk)


def _xattn_body(x_ref, oa_ref, ob_ref, wo_ref, g_ref, wq_ref, gq_ref, k_ref, v_ref, wxo_ref, h_ref):
    half = oa_ref.shape[2]
    ob = jnp.concatenate([ob_ref[0, h] for h in range(B_HEADS)], axis=1)
    h1 = x_ref[0] + _dot(oa_ref[0], wo_ref[0:half, :]) + _dot(ob, wo_ref[half:2 * half, :])
    hn = (h1 * _rms_scale(h1) * g_ref[...]).astype(BF16)
    q = _dot(hn, wq_ref[...])
    hd = gq_ref.shape[1]
    outs = []
    for h in range(q.shape[1] // hd):
        sl = slice(hd * h, hd * (h + 1))
        qh = q[:, sl]
        qh = (qh * _rms_scale(qh) * gq_ref[...] * (hd ** -0.5)).astype(BF16)
        s = _dot_nt(qh, k_ref[0, :, sl])
        e = jnp.exp(s - jnp.max(s, axis=-1, keepdims=True))
        p = (e * (1.0 / jnp.sum(e, axis=-1, keepdims=True))).astype(BF16)
        outs.append(_dot(p, v_ref[0, :, sl]).astype(BF16))
    o = jnp.concatenate(outs, axis=1)
    h_ref[0] = h1 + _dot(o, wxo_ref[...])


def _xattn(x, oa, ob, w_out, g_x, w_xq, g_xq, kmem, vmem, w_xo, tm):
    b, s, d = x.shape
    m = kmem.shape[1]
    const = lambda bi, ti: (0, 0)
    tok = lambda w: pl.BlockSpec((1, tm, w), lambda bi, ti: (bi, ti, 0))
    memblk = pl.BlockSpec((1, m, d), lambda bi, ti: (bi, 0, 0))
    return pl.pallas_call(
        _xattn_body,
        grid=(b, s // tm),
        in_specs=[tok(d), tok(oa.shape[2]),
                  pl.BlockSpec((1, B_HEADS, tm, LANES), lambda bi, ti: (bi, 0, ti, 0)),
                  pl.BlockSpec(w_out.shape, const), pl.BlockSpec((1, d), const),
                  pl.BlockSpec(w_xq.shape, const), pl.BlockSpec(g_xq.shape, const),
                  memblk, memblk, pl.BlockSpec(w_xo.shape, const)],
        out_specs=tok(d),
        out_shape=jax.ShapeDtypeStruct((b, s, d), F32),
        compiler_params=pltpu.CompilerParams(
            dimension_semantics=("arbitrary", "arbitrary"), vmem_limit_bytes=VMEM_LIMIT),
        name="outproj_xattn",
    )(x, oa, ob, w_out, g_x, w_xq, g_xq, kmem, vmem, w_xo)


HALO = 8


def _ffn_body(h_ref, g_ref, win_ref, cw_ref, cb_ref, wo_ref, o_ref, a_ref):
    tm = h_ref.shape[1]
    dff = wo_ref.shape[0]

    @pl.when(pl.program_id(1) == 0)
    def _():
        a_ref[0:HALO, :] = jnp.zeros((HALO, a_ref.shape[1]), F32)

    h = h_ref[0]
    hn = (h * _rms_scale(h) * g_ref[...]).astype(BF16)
    a_ref[HALO:HALO + tm, :] = _dot(hn, win_ref[:, 0:dff])
    gate = _dot(hn, win_ref[:, dff:2 * dff])
    conv = cb_ref[...]
    for j in range(CONV_W):
        off = HALO - (CONV_W - 1) + j
        conv = conv + a_ref[off:off + tm, :] * cw_ref[j:j + 1, :]
    a_ref[0:HALO, :] = a_ref[tm:tm + HALO, :]
    u = (jax.nn.gelu(conv) * gate).astype(BF16)
    o_ref[0] = h + _dot(u, wo_ref[...])


def _ffn(h, g_ffn, w_in, conv_w, conv_b, w_o, tm):
    b, s, d = h.shape
    dff = w_o.shape[0]
    const = lambda bi, ti: (0, 0)
    tok = pl.BlockSpec((1, tm, d), lambda bi, ti: (bi, ti, 0))
    return pl.pallas_call(
        _ffn_body,
        grid=(b, s // tm),
        in_specs=[tok, pl.BlockSpec((1, d), const), _const_spec(w_in.shape),
                  pl.BlockSpec(conv_w.shape, const),
                  pl.BlockSpec((1, dff), const), _const_spec(w_o.shape)],
        out_specs=tok,
        out_shape=jax.ShapeDtypeStruct((b, s, d), F32),
        scratch_shapes=[pltpu.VMEM((tm + HALO, dff), F32)],
        compiler_params=pltpu.CompilerParams(
            dimension_semantics=("arbitrary", "arbitrary"), vmem_limit_bytes=VMEM_LIMIT),
        name="conv_glu",
    )(h, g_ffn, w_in, conv_w, conv_b, w_o)


def _rearranged_w_in(w_in):
    sizes = (A_HEADS * HEAD_DIM, HEAD_DIM, HEAD_DIM, IDX_HEADS * HEAD_DIM, HEAD_DIM, IDX_HEADS,
             2 * B_HEADS * HEAD_DIM, 2 * B_HEADS * HEAD_DIM, B_HEADS * 2 * HEAD_DIM)
    offs = [0]
    for sz in sizes:
        offs.append(offs[-1] + sz)
    q_a, k_a, v_a, q_i, k_i, w_i, q_b, k_b, v_b = [w_in[:, offs[i]:offs[i + 1]] for i in range(9)]
    pad = jnp.zeros((w_in.shape[0], HEAD_DIM - IDX_HEADS), w_in.dtype)
    w_all = jnp.concatenate([q_a, q_i, k_a, k_a, k_i, k_i, v_a, w_i, pad, q_b, k_b], axis=1)
    assert w_all.shape[1] == _C_END
    w_vt = jnp.concatenate([v_b, v_a, jnp.zeros((w_in.shape[0], HEAD_DIM), w_in.dtype)], axis=1).T
    return w_all.astype(BF16), w_vt.astype(BF16)


def kernel(x, mem, positions, g_mix, w_in, g_qa, g_ka, g_qb, g_kb, lam_q1, lam_k1, lam_q2, lam_k2,
           w_out, g_xattn, g_mem, w_xq, w_xk, w_xv, w_xo, g_xq, g_xk, g_ffn, w_ffn_in, conv_w, conv_b,
           w_ffn_out):
    b, s, d = x.shape
    depth = g_mix.shape[0]
    topk = min(TOPK_MAX, s // 4)
    tm = min(512, s)
    tm_in = min(1024, s)

    inv_freq = 1.0 / (ROPE_THETA ** (jnp.arange(0, HEAD_DIM, 2, dtype=F32) / HEAD_DIM))
    pos = positions.reshape(b * s // tm_in, 4, tm_in // 4).transpose(0, 2, 1).reshape(b * s // 4, 4)
    ang = (pos.astype(F32)[:, :, None] * inv_freq).reshape(b * s // 4, LANES)
    cos_d, sin_d = jnp.cos(ang), jnp.sin(ang)
    sign = jnp.tile(jnp.repeat(jnp.array([-1.0, 1.0], F32), HEAD_DIM // 2), 512 // HEAD_DIM)
    blk = jnp.arange(MXU_DIM) // HEAD_DIM
    bd = jnp.where(blk[:, None] == blk[None, :], 1.0 / HEAD_DIM, 0.0).astype(BF16)

    h = x
    for l in range(depth):
        lambda_init = 0.8 - 0.6 * math.exp(-0.3 * l)
        gains = jnp.stack([jnp.tile(g, 512 // HEAD_DIM) for g in (g_qa[l], g_ka[l], g_qb[l], g_kb[l])])
        gains = jnp.concatenate([gains, sign[None, :], jnp.ones((3, 512), F32)], axis=0)
        w_all, w_vt = _rearranged_w_in(w_in[l])
        qa, qi, ks, wi, qb, kb, vbt, vat = _inproj(
            h.reshape(b * s, d), g_mix[l][None, :], w_all, w_vt, cos_d, sin_d, bd, gains, tm_in, s)
        r3 = lambda t: t.reshape(b, s, t.shape[-1])
        key_bound = lambda g: jnp.full((1, LANES), HEAD_DIM ** 0.5, F32) * jnp.max(jnp.abs(g))
        out_a = _dsa(r3(qa), r3(qi), r3(wi), r3(ks), vat, key_bound(g_ka[l]), topk, n_bisect=16)
        out_b = _diff(qb, kb, vbt, key_bound(g_kb[l]), lam_q1[l][None, :], lam_k1[l][None, :],
                      lam_q2[l][None, :], lam_k2[l][None, :], lambda_init)
        kmem, vmem = _memkv(mem, g_mem[l][None, :], w_xk[l].astype(BF16), w_xv[l].astype(BF16),
                            g_xk[l][None, :])
        h = _xattn(h, out_a, out_b, w_out[l].astype(BF16), g_xattn[l][None, :], w_xq[l].astype(BF16),
                   g_xq[l][None, :], kmem, vmem, w_xo[l].astype(BF16), min(1024, s))
        dff = w_ffn_out.shape[1]
        cw = jnp.concatenate([conv_w[l], jnp.zeros((8 - CONV_W, dff), F32)], axis=0)
        h = _ffn(h, g_ffn[l][None, :], w_ffn_in[l].astype(BF16), cw, conv_b[l][None, :],
                 w_ffn_out[l].astype(BF16), tm)
    return h
```

```python
import functools
import math

import jax
import jax.numpy as jnp
from jax import lax
from jax.experimental import pallas as pl
from jax.experimental.pallas import tpu as pltpu

F32 = jnp.float32
BF16 = jnp.bfloat16

EPS = 1e-6
ROPE_THETA = 10000.0
HEAD_DIM = 64
A_HEADS = 8
IDX_HEADS = 4
TOPK_MAX = 256
B_HEADS = 4
X_HEADS = 4
CONV_W = 3
LANES = 128
MXU_DIM = 256
DSA_QBLK = 128
DSA_PAIR = 2
DSA_STEP_PAIRS = 4
DIFF_QBLK = 256
DIFF_STEP_BLOCKS = 8
CAUSAL_STEP = MXU_DIM
VMEM_LIMIT = 56 * 1024 * 1024
LOG2E = 1.4426950408889634
SHIFT_MARGIN = 1.02
DENOM_FLOOR = 2.0 ** -40

_C_QA = 0
_C_QI = 512
_C_KS = 768
_C_QB = 1152
_C_KB = 1664
_C_END = 2176


def _dot(a, b):
    return jnp.dot(a, b, preferred_element_type=F32)


def _dot_nt(a, b):
    return lax.dot_general(a, b, (((1,), (1,)), ((), ())), preferred_element_type=F32)


def _rms_scale(x):
    return lax.rsqrt(jnp.mean(x * x, axis=-1, keepdims=True) + EPS)


def _const_spec(shape):
    zeros = (0,) * len(shape)
    return pl.BlockSpec(shape, lambda *_: zeros, pipeline_mode=pl.Buffered(1))


def _lane_tile(t, width):
    reps = width // t.shape[1]
    return t if reps == 1 else jnp.concatenate([t] * reps, axis=1)


def _col_reduce(x, reduce_fn):
    rows, cols = x.shape
    slab = 8 * max(1, 8 * LANES // cols)
    if rows % slab or rows == slab:
        return reduce_fn(x, axis=0, keepdims=True)
    part = reduce_fn(x.reshape(rows // slab, slab, cols), axis=0)
    return reduce_fn(part, axis=0, keepdims=True)


def _inproj_body(x_ref, gmix_ref, w_ref, wvt_ref, cos_ref, sin_ref, bd_ref, gains_ref,
                 qa_ref, qi_ref, ks_ref, wi_ref, qb_ref, kb_ref, vbt_ref, vat_ref):
    x = x_ref[...]
    hn = (x * _rms_scale(x) * gmix_ref[...]).astype(BF16)

    def spread(t):
        turned = [t] + [pltpu.roll(t, 32 * k, 1) for k in range(1, 4)]
        group = lax.broadcasted_iota(jnp.int32, t.shape, 1) >> 5
        parts = []
        for j in range(4):
            d = (group - j) & 3
            parts.append(jnp.where(d == 0, turned[0], jnp.where(d == 1, turned[1],
                                                               jnp.where(d == 2, turned[2], turned[3]))))
        return jnp.concatenate(parts, axis=0)

    cos = spread(cos_ref[...])
    sin = spread(sin_ref[...]) * gains_ref[4:5, 0:LANES]
    bd = bd_ref[...]

    def proj(c0, width):
        return _dot(hn, w_ref[:, c0:c0 + width])

    def group_rms_scale(p):
        sq = (p * p).astype(BF16)
        outs = []
        for j in range(p.shape[1] // MXU_DIM):
            outs.append(_dot(sq[:, MXU_DIM * j:MXU_DIM * (j + 1)], bd))
        ms = outs[0] if len(outs) == 1 else jnp.concatenate(outs, axis=1)
        return lax.rsqrt(ms + EPS)

    def rope(y):
        width = y.shape[1]
        lane = lax.broadcasted_iota(jnp.int32, y.shape, 1)
        first_half = (lane & (HEAD_DIM - 1)) < (HEAD_DIM // 2)
        swapped = jnp.where(first_half, pltpu.roll(y, width - HEAD_DIM // 2, 1),
                            pltpu.roll(y, HEAD_DIM // 2, 1))
        return y * _lane_tile(cos, width) + swapped * _lane_tile(sin, width)

    sm_scale = HEAD_DIM ** -0.5 * LOG2E

    p = proj(_C_QA, 512)
    qa_ref[...] = (rope(p * group_rms_scale(p) * gains_ref[0:1, :]) * sm_scale).astype(BF16)

    p = proj(_C_QI, 256)
    qi_ref[...] = rope(p).astype(BF16)

    p = proj(_C_KS, 384)
    p01 = p[:, 0:256]
    lane = lax.broadcasted_iota(jnp.int32, p01.shape, 1)
    y01 = jnp.where(lane < LANES, p01 * group_rms_scale(p01) * gains_ref[1:2, 0:256], p01)
    y01 = rope(y01)
    p2 = p[:, 256:384]
    lane = lax.broadcasted_iota(jnp.int32, p2.shape, 1)
    p2 = jnp.where(lane < HEAD_DIM, p2, p2 * (IDX_HEADS ** -0.5 * HEAD_DIM ** -0.5))
    ks_ref[:, 0:256] = y01.astype(BF16)
    ks_ref[:, 256:384] = p2.astype(BF16)
    wi_ref[...] = p2

    def store_heads(ref, y):
        for h in range(B_HEADS):
            ref[0, h] = y[:, LANES * h:LANES * (h + 1)].astype(BF16)

    p = proj(_C_QB, 512)
    store_heads(qb_ref, rope(p * group_rms_scale(p) * gains_ref[2:3, :]) * sm_scale)

    p = proj(_C_KB, 512)
    store_heads(kb_ref, rope(p * group_rms_scale(p) * gains_ref[3:4, :]))

    vt = _dot_nt(wvt_ref[...], hn)
    for h in range(B_HEADS):
        vbt_ref[0, h] = vt[LANES * h:LANES * (h + 1), :].astype(BF16)
    vat_ref[0] = vt[LANES * B_HEADS:LANES * (B_HEADS + 1), :].astype(BF16)


def _inproj(x2, gmix, w_all, w_vt, cos128, sin128, bd, gains, tm, seq):
    n, d = x2.shape
    row = lambda i: (i, 0)
    const = lambda i: (0, 0)
    outs = [(512, BF16), (256, BF16), (384, BF16), (LANES, F32)]
    tiles = seq // tm
    head_spec = pl.BlockSpec((1, B_HEADS, tm, LANES), lambda i: (i // tiles, 0, i % tiles, 0))
    head_shape = jax.ShapeDtypeStruct((n // seq, B_HEADS, seq, LANES), BF16)
    vbt_spec = pl.BlockSpec((1, B_HEADS, LANES, tm), lambda i: (i // tiles, 0, 0, i % tiles))
    vbt_shape = jax.ShapeDtypeStruct((n // seq, B_HEADS, LANES, seq), BF16)
    vat_spec = pl.BlockSpec((1, LANES, tm), lambda i: (i // tiles, 0, i % tiles))
    vat_shape = jax.ShapeDtypeStruct((n // seq, LANES, seq), BF16)
    return pl.pallas_call(
        _inproj_body,
        grid=(n // tm,),
        in_specs=[
            pl.BlockSpec((tm, d), row),
            pl.BlockSpec((1, d), const),
            pl.BlockSpec(w_all.shape, const),
            pl.BlockSpec(w_vt.shape, const),
            pl.BlockSpec((tm // 4, LANES), row),
            pl.BlockSpec((tm // 4, LANES), row),
            pl.BlockSpec(bd.shape, const),
            pl.BlockSpec(gains.shape, const),
        ],
        out_specs=([pl.BlockSpec((tm, w), row) for w, _ in outs]
                   + [head_spec, head_spec, vbt_spec, vat_spec]),
        out_shape=([jax.ShapeDtypeStruct((n, w), dt) for w, dt in outs]
                   + [head_shape, head_shape, vbt_shape, vat_shape]),
        compiler_params=pltpu.CompilerParams(
            dimension_semantics=("arbitrary",), vmem_limit_bytes=VMEM_LIMIT),
        name="inproj",
    )(x2, gmix, w_all, w_vt, cos128, sin128, bd, gains)


def _group_reduce(x, reduce_fn):
    g, rows, cols = x.shape
    slab = 8 * max(1, 8 * LANES // cols)
    if rows % slab or rows == slab:
        return reduce_fn(x, axis=1, keepdims=True)
    combine = {jnp.sum: jnp.add, jnp.max: jnp.maximum, jnp.min: jnp.minimum}[reduce_fn]
    part = x[:, 0:slab]
    for i in range(1, rows // slab):
        part = combine(part, x[:, i * slab:(i + 1) * slab])
    return reduce_fn(part, axis=1, keepdims=True)


def _dsa_queries(qa_ref, rows):
    qa = qa_ref[0, rows, :]
    lane = lax.broadcasted_iota(jnp.int32, (DSA_QBLK, LANES), 1)
    rows = []
    for h in range(A_HEADS):
        slab = qa[:, LANES * (h // 2):LANES * (h // 2 + 1)]
        keep = (lane >= HEAD_DIM) if h % 2 else (lane < HEAD_DIM)
        rows.append(jnp.where(keep, slab, jnp.zeros_like(slab)))
    return jnp.concatenate(rows, axis=0)


def _dsa_write(ot, l, o_ref, ot_ref, rows):
    tq = DSA_QBLK
    ot = ot * (1.0 / l)
    for h in range(A_HEADS):
        ot_ref[HEAD_DIM * h:HEAD_DIM * (h + 1), :] = ot[:, tq * h:tq * (h + 1)]
    o_ref[0, rows, :] = ot_ref[...].T.astype(BF16)


def _dsa_rows(block):
    if isinstance(block, int):
        return pl.ds(block * DSA_QBLK, DSA_QBLK)
    return pl.ds(pl.multiple_of(block * DSA_QBLK, DSA_QBLK), DSA_QBLK)


def _dsa_keys(klen, pair, sub, qa_ref, qi_ref, wi_ref, ks_ref, o_ref, vt_ref, sc_ref, bias_ref, ot_ref,
              kmax_ref, flag_ref, topk, n_bisect):
    tq = DSA_QBLK
    seq = ks_ref.shape[1]
    tail = klen - CAUSAL_STEP
    neg_inf = -jnp.inf

    lane_q = lax.broadcasted_iota(jnp.int32, (tq, LANES), 1)
    first_head = lane_q < HEAD_DIM

    def head_rows(slab, odd):
        return jnp.where(first_head != odd, slab, jnp.zeros_like(slab))

    shape_t = (DSA_PAIR, CAUSAL_STEP, tq)
    qpos_t = ((pair * DSA_PAIR + lax.broadcasted_iota(jnp.int32, shape_t, 0)) * tq
              + lax.broadcasted_iota(jnp.int32, shape_t, 2))
    causal_t = tail + lax.broadcasted_iota(jnp.int32, shape_t, 1) <= qpos_t

    def indexer(blk, carry):
        rows = _dsa_rows(sub * DSA_PAIR + blk)
        qi = qi_ref[0, rows, :]
        qi_stack = jnp.concatenate(
            [head_rows(qi[:, LANES * (h // 2):LANES * (h // 2 + 1)], bool(h % 2)) for h in range(IDX_HEADS)],
            axis=0)
        lg = _dot_nt(ks_ref[0, 0:klen, 128:256], qi_stack)
        w_t = wi_ref[0, rows, :].T
        sc = None
        for h in range(IDX_HEADS):
            term = jnp.maximum(lg[:, tq * h:tq * (h + 1)], 0.0) * w_t[HEAD_DIM + h:HEAD_DIM + h + 1, :]
            sc = term if sc is None else sc + term
        sc_ref[blk, 0:klen, :] = sc
        return carry

    lax.fori_loop(0, DSA_PAIR, indexer, 0)
    sc_ref[:, tail:klen, :] = jnp.where(causal_t, sc_ref[:, tail:klen, :], neg_inf)

    kf = float(topk)
    search = qpos_t[:, 0:1, :] >= topk

    slab = 64

    def fold(per_slab, combine, reduce_fn, start=0, stop=klen):
        acc = per_slab(sc_ref[:, start:start + slab, :], start)
        for r in range(start + slab, stop, slab):
            acc = combine(acc, per_slab(sc_ref[:, r:r + slab, :], r))
        return reduce_fn(acc, axis=1, keepdims=True)

    def count(pred):
        return fold(lambda s, _: jnp.where(pred(s), 1.0, 0.0), jnp.add, jnp.sum)

    hi0 = fold(lambda s, _: s, jnp.maximum, jnp.max)
    lo0 = fold(lambda s, r: jnp.where(causal_t[:, r - tail:r - tail + slab], s, jnp.inf),
               jnp.minimum, jnp.min, start=tail)
    if tail:
        lo0 = jnp.minimum(lo0, fold(lambda s, _: s, jnp.minimum, jnp.min, stop=tail))

    def bisect(_, carry):
        lo, hi = carry
        mid = 0.5 * (lo + hi)
        ge = count(lambda s: s >= mid) >= kf
        return jnp.where(ge, mid, lo), jnp.where(ge, hi, mid)

    lo, _ = lax.fori_loop(0, n_bisect, bisect, (lo0, hi0))

    def too_low(n_gt):
        return jnp.max(jnp.where(jnp.logical_and(search, n_gt >= kf), 1.0, 0.0))

    def climb(carry):
        thr, n_gt, _ = carry
        nxt = fold(lambda s, _: jnp.where(s > thr, s, jnp.inf), jnp.minimum, jnp.min)
        thr = jnp.where(jnp.logical_and(search, n_gt >= kf), nxt, thr)
        n_gt = count(lambda s: s > thr)
        return thr, n_gt, too_low(n_gt)

    thr0 = fold(lambda s, _: jnp.where(s >= lo, s, jnp.inf), jnp.minimum, jnp.min)
    n_gt0 = count(lambda s: s > thr0)
    thr, n_gt, _ = lax.while_loop(lambda c: c[2] > 0.0, climb, (thr0, n_gt0, too_low(n_gt0)))

    need = kf - n_gt
    open_row = jnp.where(search, neg_inf, 0.0)
    n_blk = klen // MXU_DIM
    tie_cat = jnp.concatenate(
        [jnp.where(sc_ref[g, MXU_DIM * j:MXU_DIM * (j + 1), :] == thr[g], 1.0, 0.0).astype(BF16)
         for g in range(DSA_PAIR) for j in range(n_blk)], axis=1)
    tri = jnp.where(lax.broadcasted_iota(jnp.int32, (MXU_DIM, MXU_DIM), 0)
                    >= lax.broadcasted_iota(jnp.int32, (MXU_DIM, MXU_DIM), 1), 1.0, 0.0).astype(BF16)
    prefix = _dot(tri, tie_cat)
    for g in range(DSA_PAIR):
        before = jnp.zeros((1, tq), F32)
        for j in range(n_blk):
            rows = slice(MXU_DIM * j, MXU_DIM * (j + 1))
            col = (g * n_blk + j) * tq
            rank = prefix[:, col:col + tq] + before
            before = before + prefix[MXU_DIM - 1:MXU_DIM, col:col + tq]
            sj = sc_ref[g, rows, :]
            admitted = jnp.where(sj == thr[g], jnp.where(rank <= need[g], 0.0, neg_inf), neg_inf)
            bias = jnp.maximum(jnp.where(sj > thr[g], 0.0, admitted), open_row[g])
            if j == n_blk - 1:
                bias = jnp.where(causal_t[g], bias, neg_inf)
            bias_ref[g, rows, :] = bias

    def attend(blk, underflow):
        rows = _dsa_rows(sub * DSA_PAIR + blk)
        q_all = _dsa_queries(qa_ref, rows)
        qsq = q_all.astype(F32)
        qn2 = _dot_nt(jnp.ones((8, LANES), BF16), (qsq * qsq).astype(BF16))[0:1, :]
        shift = jnp.sqrt(qn2) * (_lane_tile(kmax_ref[...], A_HEADS * tq) * SHIFT_MARGIN)
        kl = klen - tq * (DSA_PAIR - 1 - blk)
        bias = bias_ref[blk, 0:kl, :]
        st = _dot_nt(ks_ref[0, 0:kl, 0:128], q_all) + jnp.concatenate([bias] * A_HEADS, axis=1) - shift
        e = jnp.exp2(st)
        l = _col_reduce(e, jnp.sum)
        _dsa_write(_dot(vt_ref[0, 0:HEAD_DIM, 0:kl], e.astype(BF16)), l, o_ref, ot_ref, rows)
        return jnp.maximum(underflow, jnp.where(jnp.min(l) >= DENOM_FLOOR, 0, 1))

    underflow = 0
    for blk in range(DSA_PAIR):
        underflow = attend(blk, underflow)

    @pl.when(underflow != 0)
    def _():
        if klen < seq:
            bias_ref[:, klen:seq, :] = jnp.full((DSA_PAIR, seq - klen, tq), neg_inf, F32)
        flag_ref[sub] = 1


def _dsa_body(qa_ref, qi_ref, wi_ref, ks_ref, vt_ref, kmax_ref, o_ref, sc_ref, bias_ref, ot_ref,
              flag_ref, *, topk, n_bisect):
    step = pl.program_id(1)
    seq = ks_ref.shape[1]
    step_pairs = flag_ref.shape[0]
    for sub in range(step_pairs):
        flag_ref[sub] = 0

    for j in range(seq // (CAUSAL_STEP * step_pairs)):
        @pl.when(step == j)
        def _(j=j):
            for sub in range(step_pairs):
                c = step_pairs * j + sub
                _dsa_keys(CAUSAL_STEP * (c + 1), c, sub, qa_ref, qi_ref, wi_ref, ks_ref, o_ref, vt_ref,
                          sc_ref, bias_ref.at[sub], ot_ref, kmax_ref, flag_ref, topk, n_bisect)

    def redo(i, carry):
        @pl.when(flag_ref[i // DSA_PAIR] != 0)
        def _():
            rows = _dsa_rows(i)
            bias = bias_ref[i // DSA_PAIR, i % DSA_PAIR]
            st = (_dot_nt(ks_ref[0, :, 0:128], _dsa_queries(qa_ref, rows))
                  + jnp.concatenate([bias] * A_HEADS, axis=1))
            e = jnp.exp2(st - _col_reduce(st, jnp.max))
            _dsa_write(_dot(vt_ref[0, 0:HEAD_DIM, :], e.astype(BF16)), _col_reduce(e, jnp.sum), o_ref,
                       ot_ref, rows)
        return carry

    lax.fori_loop(0, step_pairs * DSA_PAIR, redo, 0)


def _dsa(qa, qi, wi, ks, vat, kmax, topk, n_bisect):
    b, s, _ = qa.shape
    tq = DSA_QBLK
    assert DSA_PAIR * tq == CAUSAL_STEP
    step_pairs = min(DSA_STEP_PAIRS, s // CAUSAL_STEP)
    step = step_pairs * CAUSAL_STEP
    assert s % step == 0
    blk = lambda bi, qi_: (bi, qi_, 0)
    return pl.pallas_call(
        functools.partial(_dsa_body, topk=topk, n_bisect=n_bisect),
        grid=(b, s // step),
        in_specs=[
            pl.BlockSpec((1, step, 512), blk),
            pl.BlockSpec((1, step, 256), blk),
            pl.BlockSpec((1, step, LANES), blk),
            pl.BlockSpec((1, s, 384), lambda bi, qi_: (bi, 0, 0)),
            pl.BlockSpec((1, LANES, s), lambda bi, qi_: (bi, 0, 0)),
            pl.BlockSpec((1, LANES), lambda bi, qi_: (0, 0)),
        ],
        out_specs=pl.BlockSpec((1, step, 512), blk),
        out_shape=jax.ShapeDtypeStruct((b, s, 512), BF16),
        scratch_shapes=[
            pltpu.VMEM((DSA_PAIR, s, tq), F32),
            pltpu.VMEM((step_pairs, DSA_PAIR, s, tq), F32),
            pltpu.VMEM((A_HEADS * HEAD_DIM, tq), F32),
            pltpu.SMEM((step_pairs,), jnp.int32),
        ],
        compiler_params=pltpu.CompilerParams(
            dimension_semantics=("arbitrary", "arbitrary"), vmem_limit_bytes=VMEM_LIMIT),
        name="dsa_attention",
    )(qa, qi, wi, ks, vat, kmax)


def _diff_body(q_ref, k_ref, vt_ref, kmax_ref, lq1_ref, lk1_ref, lq2_ref, lk2_ref, o_ref,
               flag_ref, *, lambda_init):
    step = pl.program_id(1)
    seq = k_ref.shape[2]
    n_sub = q_ref.shape[2] // DIFF_QBLK
    tq = q_ref.shape[2] // n_sub

    lam =(jnp.exp(jnp.sum(lq1_ref[...] * lk1_ref[...], axis=1, keepdims=True))
           - jnp.exp(jnp.sum(lq2_ref[...] * lk2_ref[...], axis=1, keepdims=True)) + lambda_init)
    lane = lax.broadcasted_iota(jnp.int32, (tq, LANES), 1)
    diag = (lax.broadcasted_iota(jnp.int32, (tq, 2 * tq), 0)
            <= (lax.broadcasted_iota(jnp.int32, (tq, 2 * tq), 1) & (tq - 1)))

    def queries(h, rows):
        q = q_ref[0, h, rows, :]
        zero = jnp.zeros_like(q)
        return jnp.concatenate([jnp.where(lane < HEAD_DIM, q, zero), jnp.where(lane >= HEAD_DIM, q, zero)],
                               axis=0)

    def write(h, rows, acc, l):
        acc = acc * (1.0 / l)
        ot = acc[:, 0:tq] - lam * acc[:, tq:2 * tq]
        ot = ot * lax.rsqrt(jnp.mean(ot * ot, axis=0, keepdims=True) + EPS) * (1.0 - lambda_init)
        o_ref[0, h, rows, :] = ot.T.astype(BF16)

    def attend(klen, h, rows, underflow):
        tail = klen - tq
        q2 = queries(h, rows)
        qsq = q2.astype(F32)
        qn2 = _dot_nt(jnp.ones((8, LANES), BF16), (qsq * qsq).astype(BF16))[0:1, :]
        shift = jnp.sqrt(qn2) * (_lane_tile(kmax_ref[...], 2 * tq) * SHIFT_MARGIN)
        st = _dot_nt(k_ref[0, h, 0:klen, :], q2) - shift
        e = jnp.exp2(jnp.where(diag, st[tail:klen], -jnp.inf))
        l = _col_reduce(e, jnp.sum)
        acc = _dot(vt_ref[0, h, :, tail:klen], e.astype(BF16))
        if tail:
            e = jnp.exp2(st[0:tail])
            l = l + _col_reduce(e, jnp.sum)
            acc = acc + _dot(vt_ref[0, h, :, 0:tail], e.astype(BF16))
        write(h, rows, acc, l)
        return jnp.maximum(underflow, jnp.where(jnp.min(l) >= DENOM_FLOOR, 0, 1))

    flag_ref[0] = 0
    for j in range(seq // (n_sub * tq)):
        @pl.when(step == j)
        def _(j=j):
            underflow = 0
            for i in range(n_sub):
                for h in range(B_HEADS):
                    underflow = attend(tq * (n_sub * j + i + 1), h, slice(tq * i, tq * (i + 1)), underflow)
            flag_ref[0] = underflow

    @pl.when(flag_ref[0] != 0)
    def _():
        kpos = lax.broadcasted_iota(jnp.int32, (seq, 2 * tq), 0)
        lane_q = lax.broadcasted_iota(jnp.int32, (seq, 2 * tq), 1) & (tq - 1)
        for i in range(n_sub):
            rows = slice(tq * i, tq * (i + 1))
            qpos = (step * n_sub + i) * tq + lane_q

            def redo(h, carry, rows=rows, qpos=qpos):
                st = jnp.where(kpos <= qpos, _dot_nt(k_ref[0, h], queries(h, rows)), -jnp.inf)
                e = jnp.exp2(st - _col_reduce(st, jnp.max))
                write(h, rows, _dot(vt_ref[0, h], e.astype(BF16)), _col_reduce(e, jnp.sum))
                return carry

            lax.fori_loop(0, B_HEADS, redo, 0)


def _diff(qb, kb, vbt, kmax, lq1, lk1, lq2, lk2, lambda_init):
    b, _, s, _ = qb.shape
    tq = min(DIFF_STEP_BLOCKS, s // DIFF_QBLK) * DIFF_QBLK
    assert s % tq == 0
    vec = pl.BlockSpec((1, HEAD_DIM), lambda bi, qi_: (0, 0))
    keys = pl.BlockSpec((1, B_HEADS, s, LANES), lambda bi, qi_: (bi, 0, 0, 0))
    vals = pl.BlockSpec((1, B_HEADS, LANES, s), lambda bi, qi_: (bi, 0, 0, 0))
    blk = pl.BlockSpec((1, B_HEADS, tq, LANES), lambda bi, qi_: (bi, 0, qi_, 0))
    return pl.pallas_call(
        functools.partial(_diff_body, lambda_init=lambda_init),
        grid=(b, s // tq),
        in_specs=[blk, keys, vals, pl.BlockSpec((1, LANES), lambda bi, qi_: (0, 0)), vec, vec, vec, vec],
        out_specs=blk,
        out_shape=jax.ShapeDtypeStruct((b, B_HEADS, s, LANES), BF16),
        scratch_shapes=[pltpu.SMEM((1,), jnp.int32)],
        compiler_params=pltpu.CompilerParams(
            dimension_semantics=("arbitrary", "arbitrary"), vmem_limit_bytes=VMEM_LIMIT),
        name="diff_attention",
    )(qb, kb, vbt, kmax, lq1, lk1, lq2, lk2)


def _memkv_body(mem_ref, g_ref, wk_ref, wv_ref, gk_ref, k_ref, v_ref):
    mem = mem_ref[0]
    memn = (mem * _rms_scale(mem) * g_ref[...]).astype(BF16)
    k = _dot(memn, wk_ref[...])
    hd = gk_ref.shape[1]
    for h in range(k.shape[1] // hd):
        kh = k[:, hd * h:hd * (h + 1)]
        k_ref[0, :, hd * h:hd * (h + 1)] = (kh * _rms_scale(kh) * gk_ref[...]).astype(BF16)
    v_ref[0] = _dot(memn, wv_ref[...]).astype(BF16)


def _memkv(mem, g_mem, w_xk, w_xv, g_xk):
    b, m, d = mem.shape
    const = lambda bi: (0, 0)
    blk = pl.BlockSpec((1, m, d), lambda bi: (bi, 0, 0))
    return pl.pallas_call(
        _memkv_body,
        grid=(b,),
        in_specs=[blk, pl.BlockSpec((1, d), const), pl.BlockSpec((d, d), const),
                  pl.BlockSpec((d, d), const), pl.BlockSpec(g_xk.shape, const)],
        out_specs=[blk, blk],
        out_shape=[jax.ShapeDtypeStruct((b, m, d), BF16)] * 2,
        compiler_params=pltpu.CompilerParams(
            dimension_semantics=("arbitrary",), vmem_limit_bytes=VMEM_LIMIT),
        name="mem_kv",
    )(mem, g_mem, w_xk, w_xv, g_xk)


def _xattn_body(x_ref, oa_ref, ob_ref, wo_ref, g_ref, wq_ref, gq_ref, k_ref, v_ref, wxo_ref, h_ref):
    half = oa_ref.shape[2]
    ob = jnp.concatenate([ob_ref[0, h] for h in range(B_HEADS)], axis=1)
    h1 = x_ref[0] + _dot(oa_ref[0], wo_ref[0:half, :]) + _dot(ob, wo_ref[half:2 * half, :])
    hn = (h1 * _rms_scale(h1) * g_ref[...]).astype(BF16)
    q = _dot(hn, wq_ref[...])
    hd = gq_ref.shape[1]
    outs = []
    for h in range(q.shape[1] // hd):
        sl = slice(hd * h, hd * (h + 1))
        qh = q[:, sl]
        qh = (qh * _rms_scale(qh) * gq_ref[...] * (hd ** -0.5)).astype(BF16)
        s = _dot_nt(qh, k_ref[0, :, sl])
        e = jnp.exp(s - jnp.max(s, axis=-1, keepdims=True))
        p = (e * (1.0 / jnp.sum(e, axis=-1, keepdims=True))).astype(BF16)
        outs.append(_dot(p, v_ref[0, :, sl]).astype(BF16))
    o = jnp.concatenate(outs, axis=1)
    h_ref[0] = h1 + _dot(o, wxo_ref[...])


def _xattn(x, oa, ob, w_out, g_x, w_xq, g_xq, kmem, vmem, w_xo, tm):
    b, s, d = x.shape
    m = kmem.shape[1]
    const = lambda bi, ti: (0, 0)
    tok = lambda w: pl.BlockSpec((1, tm, w), lambda bi, ti: (bi, ti, 0))
    memblk = pl.BlockSpec((1, m, d), lambda bi, ti: (bi, 0, 0))
    return pl.pallas_call(
        _xattn_body,
        grid=(b, s // tm),
        in_specs=[tok(d), tok(oa.shape[2]),
                  pl.BlockSpec((1, B_HEADS, tm, LANES), lambda bi, ti: (bi, 0, ti, 0)),
                  pl.BlockSpec(w_out.shape, const), pl.BlockSpec((1, d), const),
                  pl.BlockSpec(w_xq.shape, const), pl.BlockSpec(g_xq.shape, const),
                  memblk, memblk, pl.BlockSpec(w_xo.shape, const)],
        out_specs=tok(d),
        out_shape=jax.ShapeDtypeStruct((b, s, d), F32),
        compiler_params=pltpu.CompilerParams(
            dimension_semantics=("arbitrary", "arbitrary"), vmem_limit_bytes=VMEM_LIMIT),
        name="outproj_xattn",
    )(x, oa, ob, w_out, g_x, w_xq, g_xq, kmem, vmem, w_xo)


HALO = 8


def _ffn_body(h_ref, g_ref, win_ref, cw_ref, cb_ref, wo_ref, o_ref, a_ref):
    tm = h_ref.shape[1]
    dff = wo_ref.shape[0]

    @pl.when(pl.program_id(1) == 0)
    def _():
        a_ref[0:HALO, :] = jnp.zeros((HALO, a_ref.shape[1]), F32)

    h = h_ref[0]
    hn = (h * _rms_scale(h) * g_ref[...]).astype(BF16)
    a_ref[HALO:HALO + tm, :] = _dot(hn, win_ref[:, 0:dff])
    gate = _dot(hn, win_ref[:, dff:2 * dff])
    conv = cb_ref[...]
    for j in range(CONV_W):
        off = HALO - (CONV_W - 1) + j
        conv = conv + a_ref[off:off + tm, :] * cw_ref[j:j + 1, :]
    a_ref[0:HALO, :] = a_ref[tm:tm + HALO, :]
    u = (jax.nn.gelu(conv) * gate).astype(BF16)
    o_ref[0] = h + _dot(u, wo_ref[...])


def _ffn(h, g_ffn, w_in, conv_w, conv_b, w_o, tm):
    b, s, d = h.shape
    dff = w_o.shape[0]
    const = lambda bi, ti: (0, 0)
    tok = pl.BlockSpec((1, tm, d), lambda bi, ti: (bi, ti, 0))
    return pl.pallas_call(
        _ffn_body,
        grid=(b, s // tm),
        in_specs=[tok, pl.BlockSpec((1, d), const), _const_spec(w_in.shape),
                  pl.BlockSpec(conv_w.shape, const),
                  pl.BlockSpec((1, dff), const), _const_spec(w_o.shape)],
        out_specs=tok,
        out_shape=jax.ShapeDtypeStruct((b, s, d), F32),
        scratch_shapes=[pltpu.VMEM((tm + HALO, dff), F32)],
        compiler_params=pltpu.CompilerParams(
            dimension_semantics=("arbitrary", "arbitrary"), vmem_limit_bytes=VMEM_LIMIT),
        name="conv_glu",
    )(h, g_ffn, w_in, conv_w, conv_b, w_o)


def _rearranged_w_in(w_in):
    sizes = (A_HEADS * HEAD_DIM, HEAD_DIM, HEAD_DIM, IDX_HEADS * HEAD_DIM, HEAD_DIM, IDX_HEADS,
             2 * B_HEADS * HEAD_DIM, 2 * B_HEADS * HEAD_DIM, B_HEADS * 2 * HEAD_DIM)
    offs = [0]
    for sz in sizes:
        offs.append(offs[-1] + sz)
    q_a, k_a, v_a, q_i, k_i, w_i, q_b, k_b, v_b = [w_in[:, offs[i]:offs[i + 1]] for i in range(9)]
    pad = jnp.zeros((w_in.shape[0], HEAD_DIM - IDX_HEADS), w_in.dtype)
    w_all = jnp.concatenate([q_a, q_i, k_a, k_a, k_i, k_i, v_a, w_i, pad, q_b, k_b], axis=1)
    assert w_all.shape[1] == _C_END
    w_vt = jnp.concatenate([v_b, v_a, jnp.zeros((w_in.shape[0], HEAD_DIM), w_in.dtype)], axis=1).T
    return w_all.astype(BF16), w_vt.astype(BF16)


def kernel(x, mem, positions, g_mix, w_in, g_qa, g_ka, g_qb, g_kb, lam_q1, lam_k1, lam_q2, lam_k2,
           w_out, g_xattn, g_mem, w_xq, w_xk, w_xv, w_xo, g_xq, g_xk, g_ffn, w_ffn_in, conv_w, conv_b,
           w_ffn_out):
    b, s, d = x.shape
    depth = g_mix.shape[0]
    topk = min(TOPK_MAX, s // 4)
    tm = min(512, s)
    tm_in = min(1024, s)

    inv_freq = 1.0 / (ROPE_THETA ** (jnp.arange(0, HEAD_DIM, 2, dtype=F32) / HEAD_DIM))
    pos = positions.reshape(b * s // tm_in, 4, tm_in // 4).transpose(0, 2, 1).reshape(b * s // 4, 4)
    ang = (pos.astype(F32)[:, :, None] * inv_freq).reshape(b * s // 4, LANES)
    cos_d, sin_d = jnp.cos(ang), jnp.sin(ang)
    sign = jnp.tile(jnp.repeat(jnp.array([-1.0, 1.0], F32), HEAD_DIM // 2), 512 // HEAD_DIM)
    blk = jnp.arange(MXU_DIM) // HEAD_DIM
    bd = jnp.where(blk[:, None] == blk[None, :], 1.0 / HEAD_DIM, 0.0).astype(BF16)

    h = x
    for l in range(depth):
        lambda_init = 0.8 - 0.6 * math.exp(-0.3 * l)
        gains = jnp.stack([jnp.tile(g, 512 // HEAD_DIM) for g in (g_qa[l], g_ka[l], g_qb[l], g_kb[l])])
        gains = jnp.concatenate([gains, sign[None, :], jnp.ones((3, 512), F32)], axis=0)
        w_all, w_vt = _rearranged_w_in(w_in[l])
        qa, qi, ks, wi, qb, kb, vbt, vat = _inproj(
            h.reshape(b * s, d), g_mix[l][None, :], w_all, w_vt, cos_d, sin_d, bd, gains, tm_in, s)
        r3 = lambda t: t.reshape(b, s, t.shape[-1])
        key_bound = lambda g: jnp.full((1, LANES), HEAD_DIM ** 0.5, F32) * jnp.max(jnp.abs(g))
        out_a = _dsa(r3(qa), r3(qi), r3(wi), r3(ks), vat, key_bound(g_ka[l]), topk, n_bisect=16)
        out_b = _diff(qb, kb, vbt, key_bound(g_kb[l]), lam_q1[l][None, :], lam_k1[l][None, :],
                      lam_q2[l][None, :], lam_k2[l][None, :], lambda_init)
        kmem, vmem = _memkv(mem, g_mem[l][None, :], w_xk[l].astype(BF16), w_xv[l].astype(BF16),
                            g_xk[l][None, :])
        h = _xattn(h, out_a, out_b, w_out[l].astype(BF16), g_xattn[l][None, :], w_xq[l].astype(BF16),
                   g_xq[l][None, :], kmem, vmem, w_xo[l].astype(BF16), min(1024, s))
        dff = w_ffn_out.shape[1]
        cw = jnp.concatenate([conv_w[l], jnp.zeros((8 - CONV_W, dff), F32)], axis=0)
        h = _ffn(h, g_ffn[l][None, :], w_ffn_in[l].astype(BF16), cw, conv_b[l][None, :],
                 w_ffn_out[l].astype(BF16), tm)
    return h
```

```python
import functools
import math

import jax
import jax.numpy as jnp
from jax import lax
from jax.experimental import pallas as pl
from jax.experimental.pallas import tpu as pltpu

F32 = jnp.float32
BF16 = jnp.bfloat16

EPS = 1e-6
ROPE_THETA = 10000.0
HEAD_DIM = 64
A_HEADS = 8
IDX_HEADS = 4
TOPK_MAX = 256
B_HEADS = 4
X_HEADS = 4
CONV_W = 3
LANES = 128
MXU_DIM = 256
DSA_QBLK = 128
DSA_PAIR = 2
DSA_STEP_PAIRS = 8
DIFF_QBLK = 256
DIFF_STEP_BLOCKS = 8
CAUSAL_STEP = MXU_DIM
VMEM_LIMIT = 56 * 1024 * 1024
LOG2E = 1.4426950408889634
SHIFT_MARGIN = 1.02
DENOM_FLOOR = 2.0 ** -40

_C_QA = 0
_C_QI = 512
_C_KS = 768
_C_QB = 1152
_C_KB = 1664
_C_END = 2176


def _dot(a, b):
    return jnp.dot(a, b, preferred_element_type=F32)


def _dot_nt(a, b):
    return lax.dot_general(a, b, (((1,), (1,)), ((), ())), preferred_element_type=F32)


def _rms_scale(x):
    return lax.rsqrt(jnp.mean(x * x, axis=-1, keepdims=True) + EPS)


def _const_spec(shape):
    zeros = (0,) * len(shape)
    return pl.BlockSpec(shape, lambda *_: zeros, pipeline_mode=pl.Buffered(1))


def _lane_tile(t, width):
    reps = width // t.shape[1]
    return t if reps == 1 else jnp.concatenate([t] * reps, axis=1)


def _col_reduce(x, reduce_fn):
    rows, cols = x.shape
    slab = 8 * max(1, 8 * LANES // cols)
    if rows % slab or rows == slab:
        return reduce_fn(x, axis=0, keepdims=True)
    part = reduce_fn(x.reshape(rows // slab, slab, cols), axis=0)
    return reduce_fn(part, axis=0, keepdims=True)


def _inproj_body(x_ref, gmix_ref, w_ref, wvt_ref, cos_ref, sin_ref, bd_ref, gains_ref,
                 qa_ref, qi_ref, ks_ref, wi_ref, qb_ref, kb_ref, vbt_ref, vat_ref):
    x = x_ref[...]
    hn = (x * _rms_scale(x) * gmix_ref[...]).astype(BF16)

    def spread(t):
        turned = [t] + [pltpu.roll(t, 32 * k, 1) for k in range(1, 4)]
        group = lax.broadcasted_iota(jnp.int32, t.shape, 1) >> 5
        parts = []
        for j in range(4):
            d = (group - j) & 3
            parts.append(jnp.where(d == 0, turned[0], jnp.where(d == 1, turned[1],
                                                               jnp.where(d == 2, turned[2], turned[3]))))
        return jnp.concatenate(parts, axis=0)

    cos = spread(cos_ref[...])
    sin = spread(sin_ref[...]) * gains_ref[4:5, 0:LANES]
    bd = bd_ref[...]

    def proj(c0, width):
        return _dot(hn, w_ref[:, c0:c0 + width])

    def group_rms_scale(p):
        sq = (p * p).astype(BF16)
        outs = []
        for j in range(p.shape[1] // MXU_DIM):
            outs.append(_dot(sq[:, MXU_DIM * j:MXU_DIM * (j + 1)], bd))
        ms = outs[0] if len(outs) == 1 else jnp.concatenate(outs, axis=1)
        return lax.rsqrt(ms + EPS)

    def rope(y):
        width = y.shape[1]
        lane = lax.broadcasted_iota(jnp.int32, y.shape, 1)
        first_half = (lane & (HEAD_DIM - 1)) < (HEAD_DIM // 2)
        swapped = jnp.where(first_half, pltpu.roll(y, width - HEAD_DIM // 2, 1),
                            pltpu.roll(y, HEAD_DIM // 2, 1))
        return y * _lane_tile(cos, width) + swapped * _lane_tile(sin, width)

    sm_scale = HEAD_DIM ** -0.5 * LOG2E

    p = proj(_C_QA, 512)
    qa_ref[...] = (rope(p * group_rms_scale(p) * gains_ref[0:1, :]) * sm_scale).astype(BF16)

    p = proj(_C_QI, 256)
    qi_ref[...] = rope(p).astype(BF16)

    p = proj(_C_KS, 384)
    p01 = p[:, 0:256]
    lane = lax.broadcasted_iota(jnp.int32, p01.shape, 1)
    y01 = jnp.where(lane < LANES, p01 * group_rms_scale(p01) * gains_ref[1:2, 0:256], p01)
    y01 = rope(y01)
    p2 = p[:, 256:384]
    lane = lax.broadcasted_iota(jnp.int32, p2.shape, 1)
    p2 = jnp.where(lane < HEAD_DIM, p2, p2 * (IDX_HEADS ** -0.5 * HEAD_DIM ** -0.5))
    ks_ref[:, 0:256] = y01.astype(BF16)
    ks_ref[:, 256:384] = p2.astype(BF16)
    wi_ref[...] = p2

    def store_heads(ref, y):
        for h in range(B_HEADS):
            ref[0, h] = y[:, LANES * h:LANES * (h + 1)].astype(BF16)

    p = proj(_C_QB, 512)
    store_heads(qb_ref, rope(p * group_rms_scale(p) * gains_ref[2:3, :]) * sm_scale)

    p = proj(_C_KB, 512)
    store_heads(kb_ref, rope(p * group_rms_scale(p) * gains_ref[3:4, :]))

    vt = _dot_nt(wvt_ref[...], hn)
    for h in range(B_HEADS):
        vbt_ref[0, h] = vt[LANES * h:LANES * (h + 1), :].astype(BF16)
    vat_ref[0] = vt[LANES * B_HEADS:LANES * (B_HEADS + 1), :].astype(BF16)


def _inproj(x2, gmix, w_all, w_vt, cos128, sin128, bd, gains, tm, seq):
    n, d = x2.shape
    row = lambda i: (i, 0)
    const = lambda i: (0, 0)
    outs = [(512, BF16), (256, BF16), (384, BF16), (LANES, F32)]
    tiles = seq // tm
    head_spec = pl.BlockSpec((1, B_HEADS, tm, LANES), lambda i: (i // tiles, 0, i % tiles, 0))
    head_shape = jax.ShapeDtypeStruct((n // seq, B_HEADS, seq, LANES), BF16)
    vbt_spec = pl.BlockSpec((1, B_HEADS, LANES, tm), lambda i: (i // tiles, 0, 0, i % tiles))
    vbt_shape = jax.ShapeDtypeStruct((n // seq, B_HEADS, LANES, seq), BF16)
    vat_spec = pl.BlockSpec((1, LANES, tm), lambda i: (i // tiles, 0, i % tiles))
    vat_shape = jax.ShapeDtypeStruct((n // seq, LANES, seq), BF16)
    return pl.pallas_call(
        _inproj_body,
        grid=(n // tm,),
        in_specs=[
            pl.BlockSpec((tm, d), row),
            pl.BlockSpec((1, d), const),
            pl.BlockSpec(w_all.shape, const),
            pl.BlockSpec(w_vt.shape, const),
            pl.BlockSpec((tm // 4, LANES), row),
            pl.BlockSpec((tm // 4, LANES), row),
            pl.BlockSpec(bd.shape, const),
            pl.BlockSpec(gains.shape, const),
        ],
        out_specs=([pl.BlockSpec((tm, w), row) for w, _ in outs]
                   + [head_spec, head_spec, vbt_spec, vat_spec]),
        out_shape=([jax.ShapeDtypeStruct((n, w), dt) for w, dt in outs]
                   + [head_shape, head_shape, vbt_shape, vat_shape]),
        compiler_params=pltpu.CompilerParams(
            dimension_semantics=("arbitrary",), vmem_limit_bytes=VMEM_LIMIT),
        name="inproj",
    )(x2, gmix, w_all, w_vt, cos128, sin128, bd, gains)


def _group_reduce(x, reduce_fn):
    g, rows, cols = x.shape
    slab = 8 * max(1, 8 * LANES // cols)
    if rows % slab or rows == slab:
        return reduce_fn(x, axis=1, keepdims=True)
    combine = {jnp.sum: jnp.add, jnp.max: jnp.maximum, jnp.min: jnp.minimum}[reduce_fn]
    part = x[:, 0:slab]
    for i in range(1, rows // slab):
        part = combine(part, x[:, i * slab:(i + 1) * slab])
    return reduce_fn(part, axis=1, keepdims=True)


def _dsa_queries(qa_ref, rows):
    qa = qa_ref[0, rows, :]
    lane = lax.broadcasted_iota(jnp.int32, (DSA_QBLK, LANES), 1)
    rows = []
    for h in range(A_HEADS):
        slab = qa[:, LANES * (h // 2):LANES * (h // 2 + 1)]
        keep = (lane >= HEAD_DIM) if h % 2 else (lane < HEAD_DIM)
        rows.append(jnp.where(keep, slab, jnp.zeros_like(slab)))
    return jnp.concatenate(rows, axis=0)


def _dsa_write(ot, l, o_ref, ot_ref, rows):
    tq = DSA_QBLK
    ot = ot * (1.0 / l)
    for h in range(A_HEADS):
        ot_ref[HEAD_DIM * h:HEAD_DIM * (h + 1), :] = ot[:, tq * h:tq * (h + 1)]
    o_ref[0, rows, :] = ot_ref[...].T.astype(BF16)


def _dsa_rows(block):
    if isinstance(block, int):
        return pl.ds(block * DSA_QBLK, DSA_QBLK)
    return pl.ds(pl.multiple_of(block * DSA_QBLK, DSA_QBLK), DSA_QBLK)


def _dsa_keys(klen, pair, sub, qa_ref, qi_ref, wi_ref, ks_ref, o_ref, vt_ref, sc_ref, bias_ref, ot_ref,
              kmax_ref, flag_ref, topk, n_bisect):
    tq = DSA_QBLK
    seq = ks_ref.shape[1]
    tail = klen - CAUSAL_STEP
    neg_inf = -jnp.inf

    lane_q = lax.broadcasted_iota(jnp.int32, (tq, LANES), 1)
    first_head = lane_q < HEAD_DIM

    def head_rows(slab, odd):
        return jnp.where(first_head != odd, slab, jnp.zeros_like(slab))

    shape_t = (DSA_PAIR, CAUSAL_STEP, tq)
    qpos_t = ((pair * DSA_PAIR + lax.broadcasted_iota(jnp.int32, shape_t, 0)) * tq
              + lax.broadcasted_iota(jnp.int32, shape_t, 2))
    causal_t = tail + lax.broadcasted_iota(jnp.int32, shape_t, 1) <= qpos_t

    def indexer(blk, carry):
        rows = _dsa_rows(sub * DSA_PAIR + blk)
        qi = qi_ref[0, rows, :]
        qi_stack = jnp.concatenate(
            [head_rows(qi[:, LANES * (h // 2):LANES * (h // 2 + 1)], bool(h % 2)) for h in range(IDX_HEADS)],
            axis=0)
        lg = _dot_nt(ks_ref[0, 0:klen, 128:256], qi_stack)
        w_t = wi_ref[0, rows, :].T
        sc = None
        for h in range(IDX_HEADS):
            term = jnp.maximum(lg[:, tq * h:tq * (h + 1)], 0.0) * w_t[HEAD_DIM + h:HEAD_DIM + h + 1, :]
            sc = term if sc is None else sc + term
        sc_ref[blk, 0:klen, :] = sc
        return carry

    lax.fori_loop(0, DSA_PAIR, indexer, 0)
    sc_ref[:, tail:klen, :] = jnp.where(causal_t, sc_ref[:, tail:klen, :], neg_inf)

    kf = float(topk)
    search = qpos_t[:, 0:1, :] >= topk
    all_search = pair * CAUSAL_STEP >= topk

    slab = 64

    def fold(per_slab, combine, reduce_fn, start=0, stop=klen):
        acc = per_slab(sc_ref[:, start:start + slab, :], start)
        for r in range(start + slab, stop, slab):
            acc = combine(acc, per_slab(sc_ref[:, r:r + slab, :], r))
        return reduce_fn(acc, axis=1, keepdims=True)

    def count(pred):
        return fold(lambda s, _: jnp.where(pred(s), 1.0, 0.0), jnp.add, jnp.sum)

    hi0 = fold(lambda s, _: s, jnp.maximum, jnp.max)
    lo0 = fold(lambda s, r: jnp.where(causal_t[:, r - tail:r - tail + slab], s, jnp.inf),
               jnp.minimum, jnp.min, start=tail)
    if tail:
        lo0 = jnp.minimum(lo0, fold(lambda s, _: s, jnp.minimum, jnp.min, stop=tail))

    def bisect(_, carry):
        lo, hi = carry
        mid = 0.5 * (lo + hi)
        ge = count(lambda s: s >= mid) >= kf
        return jnp.where(ge, mid, lo), jnp.where(ge, hi, mid)

    lo, _ = lax.fori_loop(0, n_bisect, bisect, (lo0, hi0))

    def too_low(n_gt):
        return jnp.max(jnp.where(jnp.logical_and(search, n_gt >= kf), 1.0, 0.0))

    def climb(carry):
        thr, n_gt, _ = carry
        nxt = fold(lambda s, _: jnp.where(s > thr, s, jnp.inf), jnp.minimum, jnp.min)
        thr = jnp.where(jnp.logical_and(search, n_gt >= kf), nxt, thr)
        n_gt = count(lambda s: s > thr)
        return thr, n_gt, too_low(n_gt)

    thr0 = fold(lambda s, _: jnp.where(s >= lo, s, jnp.inf), jnp.minimum, jnp.min)
    n_gt0 = count(lambda s: s > thr0)
    thr, n_gt, _ = lax.while_loop(lambda c: c[2] > 0.0, climb, (thr0, n_gt0, too_low(n_gt0)))

    need = kf - n_gt
    open_row = jnp.where(search, neg_inf, 0.0)
    n_blk = klen // MXU_DIM
    tie_cat = jnp.concatenate(
        [jnp.where(sc_ref[g, MXU_DIM * j:MXU_DIM * (j + 1), :] == thr[g], 1.0, 0.0).astype(BF16)
         for g in range(DSA_PAIR) for j in range(n_blk)], axis=1)
    tri = jnp.where(lax.broadcasted_iota(jnp.int32, (MXU_DIM, MXU_DIM), 0)
                    >= lax.broadcasted_iota(jnp.int32, (MXU_DIM, MXU_DIM), 1), 1.0, 0.0).astype(BF16)
    prefix = _dot(tri, tie_cat)
    for g in range(DSA_PAIR):
        before = jnp.zeros((1, tq), F32)
        for j in range(n_blk):
            rows = slice(MXU_DIM * j, MXU_DIM * (j + 1))
            col = (g * n_blk + j) * tq
            rank = prefix[:, col:col + tq] + before
            before = before + prefix[MXU_DIM - 1:MXU_DIM, col:col + tq]
            sj = sc_ref[g, rows, :]
            admitted = jnp.where(sj == thr[g], jnp.where(rank <= need[g], 0.0, neg_inf), neg_inf)
            bias = jnp.where(sj > thr[g], 0.0, admitted)
            if not all_search:
                bias = jnp.maximum(bias, open_row[g])
            if j == n_blk - 1:
                bias = jnp.where(causal_t[g], bias, neg_inf)
            bias_ref[g, rows, :] = bias

    def attend(blk, underflow):
        rows = _dsa_rows(sub * DSA_PAIR + blk)
        q_all = _dsa_queries(qa_ref, rows)
        qsq = q_all.astype(F32)
        qn2 = _dot_nt(jnp.ones((8, LANES), BF16), (qsq * qsq).astype(BF16))[0:1, :]
        shift = jnp.sqrt(qn2) * (_lane_tile(kmax_ref[...], A_HEADS * tq) * SHIFT_MARGIN)
        kl = klen - tq * (DSA_PAIR - 1 - blk)
        bias = bias_ref[blk, 0:kl, :]
        st = _dot_nt(ks_ref[0, 0:kl, 0:128], q_all) + jnp.concatenate([bias] * A_HEADS, axis=1) - shift
        e = jnp.exp2(st)
        l = _col_reduce(e, jnp.sum)
        _dsa_write(_dot(vt_ref[0, 0:HEAD_DIM, 0:kl], e.astype(BF16)), l, o_ref, ot_ref, rows)
        return jnp.maximum(underflow, jnp.where(jnp.min(l) >= DENOM_FLOOR, 0, 1))

    underflow = 0
    for blk in range(DSA_PAIR):
        underflow = attend(blk, underflow)

    @pl.when(underflow != 0)
    def _():
        if klen < seq:
            bias_ref[:, klen:seq, :] = jnp.full((DSA_PAIR, seq - klen, tq), neg_inf, F32)
        flag_ref[sub] = 1


def _dsa_body(qa_ref, qi_ref, wi_ref, ks_ref, vt_ref, kmax_ref, o_ref, sc_ref, bias_ref, ot_ref,
              flag_ref, *, topk, n_bisect):
    step = pl.program_id(1)
    seq = ks_ref.shape[1]
    step_pairs = flag_ref.shape[0]
    for sub in range(step_pairs):
        flag_ref[sub] = 0

    for j in range(seq // (CAUSAL_STEP * step_pairs)):
        @pl.when(step == j)
        def _(j=j):
            for sub in range(step_pairs):
                c = step_pairs * j + sub
                _dsa_keys(CAUSAL_STEP * (c + 1), c, sub, qa_ref, qi_ref, wi_ref, ks_ref, o_ref, vt_ref,
                          sc_ref, bias_ref.at[sub], ot_ref, kmax_ref, flag_ref, topk, n_bisect)

    def redo(i, carry):
        @pl.when(flag_ref[i // DSA_PAIR] != 0)
        def _():
            rows = _dsa_rows(i)
            bias = bias_ref[i // DSA_PAIR, i % DSA_PAIR]
            st = (_dot_nt(ks_ref[0, :, 0:128], _dsa_queries(qa_ref, rows))
                  + jnp.concatenate([bias] * A_HEADS, axis=1))
            e = jnp.exp2(st - _col_reduce(st, jnp.max))
            _dsa_write(_dot(vt_ref[0, 0:HEAD_DIM, :], e.astype(BF16)), _col_reduce(e, jnp.sum), o_ref,
                       ot_ref, rows)
        return carry

    lax.fori_loop(0, step_pairs * DSA_PAIR, redo, 0)


def _dsa(qa, qi, wi, ks, vat, kmax, topk, n_bisect):
    b, s, _ = qa.shape
    tq = DSA_QBLK
    assert DSA_PAIR * tq == CAUSAL_STEP
    step_pairs = min(DSA_STEP_PAIRS, s // CAUSAL_STEP)
    step = step_pairs * CAUSAL_STEP
    assert s % step == 0
    blk = lambda bi, qi_: (bi, qi_, 0)
    return pl.pallas_call(
        functools.partial(_dsa_body, topk=topk, n_bisect=n_bisect),
        grid=(b, s // step),
        in_specs=[
            pl.BlockSpec((1, step, 512), blk),
            pl.BlockSpec((1, step, 256), blk),
            pl.BlockSpec((1, step, LANES), blk),
            pl.BlockSpec((1, s, 384), lambda bi, qi_: (bi, 0, 0)),
            pl.BlockSpec((1, LANES, s), lambda bi, qi_: (bi, 0, 0)),
            pl.BlockSpec((1, LANES), lambda bi, qi_: (0, 0)),
        ],
        out_specs=pl.BlockSpec((1, step, 512), blk),
        out_shape=jax.ShapeDtypeStruct((b, s, 512), BF16),
        scratch_shapes=[
            pltpu.VMEM((DSA_PAIR, s, tq), F32),
            pltpu.VMEM((step_pairs, DSA_PAIR, s, tq), F32),
            pltpu.VMEM((A_HEADS * HEAD_DIM, tq), F32),
            pltpu.SMEM((step_pairs,), jnp.int32),
        ],
        compiler_params=pltpu.CompilerParams(
            dimension_semantics=("arbitrary", "arbitrary"), vmem_limit_bytes=VMEM_LIMIT),
        name="dsa_attention",
    )(qa, qi, wi, ks, vat, kmax)


def _diff_body(q_ref, k_ref, vt_ref, kmax_ref, lq1_ref, lk1_ref, lq2_ref, lk2_ref, o_ref,
               flag_ref, *, lambda_init):
    step = pl.program_id(1)
    seq = k_ref.shape[2]
    n_sub = q_ref.shape[2] // DIFF_QBLK
    tq = q_ref.shape[2] // n_sub

    lam =(jnp.exp(jnp.sum(lq1_ref[...] * lk1_ref[...], axis=1, keepdims=True))
           - jnp.exp(jnp.sum(lq2_ref[...] * lk2_ref[...], axis=1, keepdims=True)) + lambda_init)
    lane = lax.broadcasted_iota(jnp.int32, (tq, LANES), 1)
    diag = (lax.broadcasted_iota(jnp.int32, (tq, 2 * tq), 0)
            <= (lax.broadcasted_iota(jnp.int32, (tq, 2 * tq), 1) & (tq - 1)))

    def queries(h, rows):
        q = q_ref[0, h, rows, :]
        zero = jnp.zeros_like(q)
        return jnp.concatenate([jnp.where(lane < HEAD_DIM, q, zero), jnp.where(lane >= HEAD_DIM, q, zero)],
                               axis=0)

    def write(h, rows, acc, l):
        acc = acc * (1.0 / l)
        ot = acc[:, 0:tq] - lam * acc[:, tq:2 * tq]
        ot = ot * lax.rsqrt(jnp.mean(ot * ot, axis=0, keepdims=True) + EPS) * (1.0 - lambda_init)
        o_ref[0, h, rows, :] = ot.T.astype(BF16)

    def attend(klen, h, rows, underflow):
        tail = klen - tq
        q2 = queries(h, rows)
        qsq = q2.astype(F32)
        qn2 = _dot_nt(jnp.ones((8, LANES), BF16), (qsq * qsq).astype(BF16))[0:1, :]
        shift = jnp.sqrt(qn2) * (_lane_tile(kmax_ref[...], 2 * tq) * SHIFT_MARGIN)
        st = _dot_nt(k_ref[0, h, 0:klen, :], q2) - shift
        e = jnp.exp2(jnp.where(diag, st[tail:klen], -jnp.inf))
        l = _col_reduce(e, jnp.sum)
        acc = _dot(vt_ref[0, h, :, tail:klen], e.astype(BF16))
        if tail:
            e = jnp.exp2(st[0:tail])
            l = l + _col_reduce(e, jnp.sum)
            acc = acc + _dot(vt_ref[0, h, :, 0:tail], e.astype(BF16))
        write(h, rows, acc, l)
        return jnp.maximum(underflow, jnp.where(jnp.min(l) >= DENOM_FLOOR, 0, 1))

    flag_ref[0] = 0
    for j in range(seq // (n_sub * tq)):
        @pl.when(step == j)
        def _(j=j):
            underflow = 0
            for i in range(n_sub):
                for h in range(B_HEADS):
                    underflow = attend(tq * (n_sub * j + i + 1), h, slice(tq * i, tq * (i + 1)), underflow)
            flag_ref[0] = underflow

    @pl.when(flag_ref[0] != 0)
    def _():
        kpos = lax.broadcasted_iota(jnp.int32, (seq, 2 * tq), 0)
        lane_q = lax.broadcasted_iota(jnp.int32, (seq, 2 * tq), 1) & (tq - 1)
        for i in range(n_sub):
            rows = slice(tq * i, tq * (i + 1))
            qpos = (step * n_sub + i) * tq + lane_q

            def redo(h, carry, rows=rows, qpos=qpos):
                st = jnp.where(kpos <= qpos, _dot_nt(k_ref[0, h], queries(h, rows)), -jnp.inf)
                e = jnp.exp2(st - _col_reduce(st, jnp.max))
                write(h, rows, _dot(vt_ref[0, h], e.astype(BF16)), _col_reduce(e, jnp.sum))
                return carry

            lax.fori_loop(0, B_HEADS, redo, 0)


def _diff(qb, kb, vbt, kmax, lq1, lk1, lq2, lk2, lambda_init):
    b, _, s, _ = qb.shape
    tq = min(DIFF_STEP_BLOCKS, s // DIFF_QBLK) * DIFF_QBLK
    assert s % tq == 0
    vec = pl.BlockSpec((1, HEAD_DIM), lambda bi, qi_: (0, 0))
    keys = pl.BlockSpec((1, B_HEADS, s, LANES), lambda bi, qi_: (bi, 0, 0, 0))
    vals = pl.BlockSpec((1, B_HEADS, LANES, s), lambda bi, qi_: (bi, 0, 0, 0))
    blk = pl.BlockSpec((1, B_HEADS, tq, LANES), lambda bi, qi_: (bi, 0, qi_, 0))
    return pl.pallas_call(
        functools.partial(_diff_body, lambda_init=lambda_init),
        grid=(b, s // tq),
        in_specs=[blk, keys, vals, pl.BlockSpec((1, LANES), lambda bi, qi_: (0, 0)), vec, vec, vec, vec],
        out_specs=blk,
        out_shape=jax.ShapeDtypeStruct((b, B_HEADS, s, LANES), BF16),
        scratch_shapes=[pltpu.SMEM((1,), jnp.int32)],
        compiler_params=pltpu.CompilerParams(
            dimension_semantics=("arbitrary", "arbitrary"), vmem_limit_bytes=VMEM_LIMIT),
        name="diff_attention",
    )(qb, kb, vbt, kmax, lq1, lk1, lq2, lk2)


def _memkv_body(mem_ref, g_ref, wk_ref, wv_ref, gk_ref, k_ref, v_ref):
    mem = mem_ref[0]
    memn = (mem * _rms_scale(mem) * g_ref[...]).astype(BF16)
    k = _dot(memn, wk_ref[...])
    hd = gk_ref.shape[1]
    for h in range(k.shape[1] // hd):
        kh = k[:, hd * h:hd * (h + 1)]
        k_ref[0, :, hd * h:hd * (h + 1)] = (kh * _rms_scale(kh) * gk_ref[...]).astype(BF16)
    v_ref[0] = _dot(memn, wv_ref[...]).astype(BF16)


def _memkv(mem, g_mem, w_xk, w_xv, g_xk):
    b, m, d = mem.shape
    const = lambda bi: (0, 0)
    blk = pl.BlockSpec((1, m, d), lambda bi: (bi, 0, 0))
    return pl.pallas_call(
        _memkv_body,
        grid=(b,),
        in_specs=[blk, pl.BlockSpec((1, d), const), pl.BlockSpec((d, d), const),
                  pl.BlockSpec((d, d), const), pl.BlockSpec(g_xk.shape, const)],
        out_specs=[blk, blk],
        out_shape=[jax.ShapeDtypeStruct((b, m, d), BF16)] * 2,
        compiler_params=pltpu.CompilerParams(
            dimension_semantics=("arbitrary",), vmem_limit_bytes=VMEM_LIMIT),
        name="mem_kv",
    )(mem, g_mem, w_xk, w_xv, g_xk)


def _xattn_body(x_ref, oa_ref, ob_ref, wo_ref, g_ref, wq_ref, gq_ref, k_ref, v_ref, wxo_ref, h_ref):
    half = oa_ref.shape[2]
    ob = jnp.concatenate([ob_ref[0, h] for h in range(B_HEADS)], axis=1)
    h1 = x_ref[0] + _dot(oa_ref[0], wo_ref[0:half, :]) + _dot(ob, wo_ref[half:2 * half, :])
    hn = (h1 * _rms_scale(h1) * g_ref[...]).astype(BF16)
    q = _dot(hn, wq_ref[...])
    hd = gq_ref.shape[1]
    outs = []
    for h in range(q.shape[1] // hd):
        sl = slice(hd * h, hd * (h + 1))
        qh = q[:, sl]
        qh = (qh * _rms_scale(qh) * gq_ref[...] * (hd ** -0.5)).astype(BF16)
        s = _dot_nt(qh, k_ref[0, :, sl])
        e = jnp.exp(s - jnp.max(s, axis=-1, keepdims=True))
        p = (e * (1.0 / jnp.sum(e, axis=-1, keepdims=True))).astype(BF16)
        outs.append(_dot(p, v_ref[0, :, sl]).astype(BF16))
    o = jnp.concatenate(outs, axis=1)
    h_ref[0] = h1 + _dot(o, wxo_ref[...])


def _xattn(x, oa, ob, w_out, g_x, w_xq, g_xq, kmem, vmem, w_xo, tm):
    b, s, d = x.shape
    m = kmem.shape[1]
    const = lambda bi, ti: (0, 0)
    tok = lambda w: pl.BlockSpec((1, tm, w), lambda bi, ti: (bi, ti, 0))
    memblk = pl.BlockSpec((1, m, d), lambda bi, ti: (bi, 0, 0))
    return pl.pallas_call(
        _xattn_body,
        grid=(b, s // tm),
        in_specs=[tok(d), tok(oa.shape[2]),
                  pl.BlockSpec((1, B_HEADS, tm, LANES), lambda bi, ti: (bi, 0, ti, 0)),
                  pl.BlockSpec(w_out.shape, const), pl.BlockSpec((1, d), const),
                  pl.BlockSpec(w_xq.shape, const), pl.BlockSpec(g_xq.shape, const),
                  memblk, memblk, pl.BlockSpec(w_xo.shape, const)],
        out_specs=tok(d),
        out_shape=jax.ShapeDtypeStruct((b, s, d), F32),
        compiler_params=pltpu.CompilerParams(
            dimension_semantics=("arbitrary", "arbitrary"), vmem_limit_bytes=VMEM_LIMIT),
        name="outproj_xattn",
    )(x, oa, ob, w_out, g_x, w_xq, g_xq, kmem, vmem, w_xo)


HALO = 8


def _ffn_body(h_ref, g_ref, win_ref, cw_ref, cb_ref, wo_ref, o_ref, a_ref):
    tm = h_ref.shape[1]
    dff = wo_ref.shape[0]

    @pl.when(pl.program_id(1) == 0)
    def _():
        a_ref[0:HALO, :] = jnp.zeros((HALO, a_ref.shape[1]), F32)

    h = h_ref[0]
    hn = (h * _rms_scale(h) * g_ref[...]).astype(BF16)
    a_ref[HALO:HALO + tm, :] = _dot(hn, win_ref[:, 0:dff])
    gate = _dot(hn, win_ref[:, dff:2 * dff])
    conv = cb_ref[...]
    for j in range(CONV_W):
        off = HALO - (CONV_W - 1) + j
        conv = conv + a_ref[off:off + tm, :] * cw_ref[j:j + 1, :]
    a_ref[0:HALO, :] = a_ref[tm:tm + HALO, :]
    u = (jax.nn.gelu(conv) * gate).astype(BF16)
    o_ref[0] = h + _dot(u, wo_ref[...])


def _ffn(h, g_ffn, w_in, conv_w, conv_b, w_o, tm):
    b, s, d = h.shape
    dff = w_o.shape[0]
    const = lambda bi, ti: (0, 0)
    tok = pl.BlockSpec((1, tm, d), lambda bi, ti: (bi, ti, 0))
    return pl.pallas_call(
        _ffn_body,
        grid=(b, s // tm),
        in_specs=[tok, pl.BlockSpec((1, d), const), _const_spec(w_in.shape),
                  pl.BlockSpec(conv_w.shape, const),
                  pl.BlockSpec((1, dff), const), _const_spec(w_o.shape)],
        out_specs=tok,
        out_shape=jax.ShapeDtypeStruct((b, s, d), F32),
        scratch_shapes=[pltpu.VMEM((tm + HALO, dff), F32)],
        compiler_params=pltpu.CompilerParams(
            dimension_semantics=("arbitrary", "arbitrary"), vmem_limit_bytes=VMEM_LIMIT),
        name="conv_glu",
    )(h, g_ffn, w_in, conv_w, conv_b, w_o)


def _rearranged_w_in(w_in):
    sizes = (A_HEADS * HEAD_DIM, HEAD_DIM, HEAD_DIM, IDX_HEADS * HEAD_DIM, HEAD_DIM, IDX_HEADS,
             2 * B_HEADS * HEAD_DIM, 2 * B_HEADS * HEAD_DIM, B_HEADS * 2 * HEAD_DIM)
    offs = [0]
    for sz in sizes:
        offs.append(offs[-1] + sz)
    q_a, k_a, v_a, q_i, k_i, w_i, q_b, k_b, v_b = [w_in[:, offs[i]:offs[i + 1]] for i in range(9)]
    pad = jnp.zeros((w_in.shape[0], HEAD_DIM - IDX_HEADS), w_in.dtype)
    w_all = jnp.concatenate([q_a, q_i, k_a, k_a, k_i, k_i, v_a, w_i, pad, q_b, k_b], axis=1)
    assert w_all.shape[1] == _C_END
    w_vt = jnp.concatenate([v_b, v_a, jnp.zeros((w_in.shape[0], HEAD_DIM), w_in.dtype)], axis=1).T
    return w_all.astype(BF16), w_vt.astype(BF16)


def kernel(x, mem, positions, g_mix, w_in, g_qa, g_ka, g_qb, g_kb, lam_q1, lam_k1, lam_q2, lam_k2,
           w_out, g_xattn, g_mem, w_xq, w_xk, w_xv, w_xo, g_xq, g_xk, g_ffn, w_ffn_in, conv_w, conv_b,
           w_ffn_out):
    b, s, d = x.shape
    depth = g_mix.shape[0]
    topk = min(TOPK_MAX, s // 4)
    tm = min(512, s)
    tm_in = min(1024, s)

    inv_freq = 1.0 / (ROPE_THETA ** (jnp.arange(0, HEAD_DIM, 2, dtype=F32) / HEAD_DIM))
    pos = positions.reshape(b * s // tm_in, 4, tm_in // 4).transpose(0, 2, 1).reshape(b * s // 4, 4)
    ang = (pos.astype(F32)[:, :, None] * inv_freq).reshape(b * s // 4, LANES)
    cos_d, sin_d = jnp.cos(ang), jnp.sin(ang)
    sign = jnp.tile(jnp.repeat(jnp.array([-1.0, 1.0], F32), HEAD_DIM // 2), 512 // HEAD_DIM)
    blk = jnp.arange(MXU_DIM) // HEAD_DIM
    bd = jnp.where(blk[:, None] == blk[None, :], 1.0 / HEAD_DIM, 0.0).astype(BF16)

    h = x
    for l in range(depth):
        lambda_init = 0.8 - 0.6 * math.exp(-0.3 * l)
        gains = jnp.stack([jnp.tile(g, 512 // HEAD_DIM) for g in (g_qa[l], g_ka[l], g_qb[l], g_kb[l])])
        gains = jnp.concatenate([gains, sign[None, :], jnp.ones((3, 512), F32)], axis=0)
        w_all, w_vt = _rearranged_w_in(w_in[l])
        qa, qi, ks, wi, qb, kb, vbt, vat = _inproj(
            h.reshape(b * s, d), g_mix[l][None, :], w_all, w_vt, cos_d, sin_d, bd, gains, tm_in, s)
        r3 = lambda t: t.reshape(b, s, t.shape[-1])
        key_bound = lambda g: jnp.full((1, LANES), HEAD_DIM ** 0.5, F32) * jnp.max(jnp.abs(g))
        out_a = _dsa(r3(qa), r3(qi), r3(wi), r3(ks), vat, key_bound(g_ka[l]), topk, n_bisect=16)
        out_b = _diff(qb, kb, vbt, key_bound(g_kb[l]), lam_q1[l][None, :], lam_k1[l][None, :],
                      lam_q2[l][None, :], lam_k2[l][None, :], lambda_init)
        kmem, vmem = _memkv(mem, g_mem[l][None, :], w_xk[l].astype(BF16), w_xv[l].astype(BF16),
                            g_xk[l][None, :])
        h = _xattn(h, out_a, out_b, w_out[l].astype(BF16), g_xattn[l][None, :], w_xq[l].astype(BF16),
                   g_xq[l][None, :], kmem, vmem, w_xo[l].astype(BF16), min(1024, s))
        dff = w_ffn_out.shape[1]
        cw = jnp.concatenate([conv_w[l], jnp.zeros((8 - CONV_W, dff), F32)], axis=0)
        h = _ffn(h, g_ffn[l][None, :], w_ffn_in[l].astype(BF16), cw, conv_b[l][None, :],
                 w_ffn_out[l].astype(BF16), tm)
    return h
```
